```python
import math
import jax, jax.numpy as jnp
from jax import lax
import numpy as np

D_MODEL = 1024
BATCH = 4
SEQ = 4096
DEPTH = 4

N_MIXERS = 2
DILATED_GROUPS = ((128, 1), (512, 4), (2048, 16))
N_DIL_GROUPS = len(DILATED_GROUPS)
A_HEADS = 16
A_HEAD_DIM = D_MODEL // A_HEADS
ATTN_BLOCK = 128
ROPE_THETA = 10000.0
GDN_HEADS = 8
GDN_HEAD_DIM = D_MODEL // GDN_HEADS
GDN_WIDTH = GDN_HEADS * GDN_HEAD_DIM
CONV_WIDTH = 4
GDN_CHUNK = 64
N_EXPERTS = 64
TOP_K = 8
N_EXPERT_GROUPS = 8
TOPK_GROUPS = 4
EXPERT_DIM = 256
SHARED_DIM = 256
ROUTED_SCALE = 2.5
MOE_BLOCK = 128
ALPHA = (2 * DEPTH) ** 0.25
BETA = (8 * DEPTH) ** -0.25
LN_EPS = 1e-5
RMS_EPS = 1e-6

kernel_name = "hybrid_dilated_attn_gdn_moe_deepnorm"


def layer_norm(x, g, b):
    xf = x.astype(jnp.float32)
    mu = jnp.mean(xf, -1, keepdims=True)
    var = jnp.mean(jnp.square(xf - mu), -1, keepdims=True)
    return ((xf - mu) * lax.rsqrt(var + LN_EPS) * g + b).astype(x.dtype)


def modulate(x, c, w_ada, b_ada):
    shift, scale, gate = jnp.split(c @ w_ada + b_ada, 3, axis=-1)
    h = x * (1.0 + scale[:, None, :]) + shift[:, None, :]
    return h, 1.0 + gate[:, None, :]


def deepnorm_residual(x, y, gate, g, b):
    return layer_norm(ALPHA * x + gate * y, g, b)


def rotary(t, positions):
    half = t.shape[-1] // 2
    inv_freq = ROPE_THETA ** (-jnp.arange(half, dtype=jnp.float32) / half)
    ang = positions.astype(jnp.float32)[..., None] * inv_freq
    cos = jnp.cos(ang)[:, :, None, :]
    sin = jnp.sin(ang)[:, :, None, :]
    t1 = t[..., :half].astype(jnp.float32)
    t2 = t[..., half:].astype(jnp.float32)
    return jnp.concatenate([t1 * cos - t2 * sin, t2 * cos + t1 * sin], axis=-1)


def dilated_window_attention(q, k, v, window, dil):
    B, H, S, hd = q.shape
    steps = window // dil
    assert steps <= ATTN_BLOCK
    span = dil * ATTN_BLOCK
    Sp = -(-S // span) * span
    L = Sp // dil
    nb = L // ATTN_BLOCK
    pad = ((0, 0), (0, 0), (0, Sp - S), (0, 0))

    def by_residue(t):
        return jnp.pad(t, pad).reshape(B, H, L, dil, hd).transpose(0, 1, 3, 2, 4)

    def kv_blocks(t):
        t = jnp.pad(by_residue(t), ((0, 0), (0, 0), (0, 0), (ATTN_BLOCK, 0), (0, 0)))
        t = t.reshape(B, H, dil, nb + 1, ATTN_BLOCK, hd)
        return jnp.concatenate([t[:, :, :, :-1], t[:, :, :, 1:]], axis=-2)

    qb = by_residue(q).reshape(B, H, dil, nb, ATTN_BLOCK, hd)
    kb = kv_blocks(k)
    vb = kv_blocks(v)
    s = jnp.einsum('bhrnqd,bhrnkd->bhrnqk', qb, kb)
    qi = jnp.arange(ATTN_BLOCK)[:, None]
    kj = jnp.arange(2 * ATTN_BLOCK)[None, :]
    dist = qi + ATTN_BLOCK - kj
    band = (dist >= 0) & (dist <= steps)
    before_start = (jnp.arange(nb)[:, None, None] == 0) & (kj[None] < ATTN_BLOCK)
    valid = band[None] & ~before_start
    s = jnp.where(valid, s, -jnp.inf)
    m = jnp.max(s, axis=-1, keepdims=True)
    p = jnp.exp(s - m)
    l = jnp.sum(p, axis=-1, keepdims=True)
    o = jnp.einsum('bhrnqk,bhrnkd->bhrnqd', p, vb) / l
    lse = (m + jnp.log(l))[..., 0]
    o = o.reshape(B, H, dil, L, hd).transpose(0, 1, 3, 2, 4).reshape(B, H, Sp, hd)[:, :, :S]
    lse = lse.reshape(B, H, dil, L).transpose(0, 1, 3, 2).reshape(B, H, Sp)[:, :, :S]
    return o, lse


def dilated_attention_mixer(h, positions, w_in, w_out):
    B, S, D = h.shape
    proj = (h @ w_in).reshape(B, S, N_DIL_GROUPS, 3, A_HEADS, A_HEAD_DIM)
    scale = A_HEAD_DIM ** -0.5
    outs, lses = [], []
    for gi, (window, dil) in enumerate(DILATED_GROUPS):
        q = (rotary(proj[:, :, gi, 0], positions) * scale).transpose(0, 2, 1, 3)
        k = rotary(proj[:, :, gi, 1], positions).transpose(0, 2, 1, 3)
        v = proj[:, :, gi, 2].astype(jnp.float32).transpose(0, 2, 1, 3)
        o, lse = dilated_window_attention(q, k, v, window, dil)
        outs.append(o)
        lses.append(lse)
    wts = jax.nn.softmax(jnp.stack(lses), axis=0)
    o = jnp.sum(wts[..., None] * jnp.stack(outs), axis=0)
    o = o.transpose(0, 2, 1, 3).reshape(B, S, A_HEADS * A_HEAD_DIM).astype(h.dtype)
    return o @ w_out


def causal_depthwise_conv(x, w):
    K = w.shape[0]
    S = x.shape[1]
    xp = jnp.pad(x, ((0, 0), (K - 1, 0), (0, 0)))
    y = xp[:, 0:S] * w[0]
    for j in range(1, K):
        y = y + xp[:, j:j + S] * w[j]
    return y


def l2_normalize(t):
    return t * lax.rsqrt(jnp.sum(jnp.square(t), -1, keepdims=True) + RMS_EPS)


def gated_delta_rule(q, k, v, g, beta):
    B, S, H, dk = q.shape
    dv = v.shape[-1]
    C = GDN_CHUNK
    N = S // C

    def chunks(t):
        t = jnp.moveaxis(t.astype(jnp.float32), 2, 1)
        return t.reshape((B, H, N, C) + t.shape[3:])

    q, k, v, g, beta = chunks(q), chunks(k), chunks(v), chunks(g), chunks(beta)
    gc = jnp.cumsum(g, axis=-1)
    causal = jnp.tril(jnp.ones((C, C), bool))
    strict = jnp.tril(jnp.ones((C, C), bool), -1)
    decay = jnp.exp(jnp.where(causal, gc[..., :, None] - gc[..., None, :], -jnp.inf))
    kb = k * beta[..., None]
    lower = jnp.where(strict, jnp.einsum('bhnid,bhnjd->bhnij', kb, k) * decay, 0.0)
    a_mat = lower + jnp.eye(C, dtype=jnp.float32)
    rhs = jnp.concatenate([v * beta[..., None], kb * jnp.exp(gc)[..., None]], axis=-1)
    sol = lax.linalg.triangular_solve(a_mat, rhs, left_side=True, lower=True,
                                      unit_diagonal=True)
    u, w = sol[..., :dv], sol[..., dv:]
    qk = jnp.einsum('bhnid,bhnjd->bhnij', q, k) * decay
    qg = q * jnp.exp(gc)[..., None]
    kdec = k * jnp.exp(gc[..., -1:] - gc)[..., None]
    glast = jnp.exp(gc[..., -1])

    def step(state, xs):
        u_n, w_n, qk_n, qg_n, kdec_n, gl_n = xs
        v_new = u_n - jnp.einsum('bhcd,bhde->bhce', w_n, state)
        o = (jnp.einsum('bhcd,bhde->bhce', qg_n, state)
             + jnp.einsum('bhij,bhje->bhie', qk_n, v_new))
        state = state * gl_n[..., None, None] + jnp.einsum('bhcd,bhce->bhde', kdec_n, v_new)
        return state, o

    xs = tuple(jnp.moveaxis(t, 2, 0) for t in (u, w, qk, qg, kdec, glast))
    _, o = lax.scan(step, jnp.zeros((B, H, dk, dv), jnp.float32), xs)
    return o.transpose(1, 0, 3, 2, 4).reshape(B, S, H, dv)


def gated_deltanet_mixer(h, w_in, conv_w, a_log, dt_bias, norm_w, w_out):
    B, S, D = h.shape
    H, dh, W = GDN_HEADS, GDN_HEAD_DIM, GDN_WIDTH
    proj = h @ w_in
    qkv = jax.nn.silu(causal_depthwise_conv(proj[..., :3 * W], conv_w))
    z = proj[..., 3 * W:4 * W].astype(jnp.float32).reshape(B, S, H, dh)
    a = proj[..., 4 * W:4 * W + H].astype(jnp.float32)
    b = proj[..., 4 * W + H:].astype(jnp.float32)
    qkv = qkv.astype(jnp.float32).reshape(B, S, 3, H, dh)
    q = l2_normalize(qkv[:, :, 0]) * dh ** -0.5
    k = l2_normalize(qkv[:, :, 1])
    v = qkv[:, :, 2]
    beta = jax.nn.sigmoid(b)
    g = -jnp.exp(a_log.astype(jnp.float32)) * jax.nn.softplus(a + dt_bias)
    o = gated_delta_rule(q, k, v, g, beta)
    o = o * lax.rsqrt(jnp.mean(jnp.square(o), -1, keepdims=True) + RMS_EPS) * norm_w
    o = o * jax.nn.silu(z)
    return o.reshape(B, S, W).astype(h.dtype) @ w_out


def swiglu(x, w1, w3, w2):
    return (jax.nn.silu(x @ w1) * (x @ w3)) @ w2


def moe(h, router_w, router_bias, w1, w3, w2, ws1, ws3, ws2):
    B, S, D = h.shape
    T = B * S
    xt = h.reshape(T, D)
    scores = jax.nn.sigmoid((xt @ router_w).astype(jnp.float32))
    biased = scores + router_bias.astype(jnp.float32)
    grp = biased.reshape(T, N_EXPERT_GROUPS, N_EXPERTS // N_EXPERT_GROUPS)
    gscore = jnp.sum(lax.top_k(grp, 2)[0], axis=-1)
    _, gidx = lax.top_k(gscore, TOPK_GROUPS)
    gmask = jnp.any(gidx[..., :, None] == jnp.arange(N_EXPERT_GROUPS), axis=-2)
    emask = jnp.repeat(gmask, N_EXPERTS // N_EXPERT_GROUPS, axis=-1)
    _, eidx = lax.top_k(jnp.where(emask, biased, -jnp.inf), TOP_K)
    gates = jnp.take_along_axis(scores, eidx, axis=-1)
    gates = gates / jnp.sum(gates, -1, keepdims=True) * ROUTED_SCALE

    n_assign = T * TOP_K
    flat_e = eidx.reshape(-1)
    flat_tok = jnp.repeat(jnp.arange(T, dtype=jnp.int32), TOP_K)
    flat_w = gates.reshape(-1)
    order = jnp.argsort(flat_e)
    se, stok, sw = flat_e[order], flat_tok[order], flat_w[order]
    counts = jnp.bincount(flat_e, length=N_EXPERTS)
    start = jnp.cumsum(counts) - counts
    padded = (counts + MOE_BLOCK - 1) // MOE_BLOCK * MOE_BLOCK
    pend = jnp.cumsum(padded)
    pstart = pend - padded
    dest = pstart[se] + jnp.arange(n_assign) - start[se]
    n_blocks = -(-n_assign // MOE_BLOCK) + N_EXPERTS
    P = n_blocks * MOE_BLOCK
    row_tok = jnp.full((P,), T, jnp.int32).at[dest].set(stok)
    row_w = jnp.zeros((P,), jnp.float32).at[dest].set(sw)
    blk_e = jnp.minimum(jnp.searchsorted(pend, jnp.arange(n_blocks) * MOE_BLOCK, side='right'),
                        N_EXPERTS - 1)
    x_pad = jnp.concatenate([xt, jnp.zeros((1, D), xt.dtype)], axis=0)

    def expert_block(acc, blk):
        tok, wt, e = blk
        y = swiglu(x_pad[tok], w1[e], w3[e], w2[e])
        return acc.at[tok].add(y * wt[:, None].astype(y.dtype)), None

    acc, _ = lax.scan(expert_block, jnp.zeros((T + 1, D), xt.dtype),
                      (row_tok.reshape(n_blocks, MOE_BLOCK),
                       row_w.reshape(n_blocks, MOE_BLOCK), blk_e))
    out = acc[:T] + swiglu(xt, ws1, ws3, ws2)
    return out.reshape(B, S, D)


def setup_inputs(seed: int = 0) -> dict:
    key = jax.random.key(seed)
    ks = jax.random.split(key, 24)
    D = D_MODEL
    n_a = (DEPTH + 1) // 2
    n_b = DEPTH // 2
    H = GDN_HEADS

    def nrm(k, shape, s):
        return jax.random.normal(k, shape, jnp.float32) * s

    x = nrm(ks[0], (BATCH, SEQ, D), 1.0)
    c = nrm(ks[1], (BATCH, D), 1.0)
    positions = jnp.broadcast_to(jnp.arange(SEQ, dtype=jnp.int32), (BATCH, SEQ))
    ada_w = nrm(ks[2], (DEPTH, 2, D, 3 * D), 0.1 * D ** -0.5)
    ada_b = nrm(ks[3], (DEPTH, 2, 3 * D), 0.02)
    ln_g = 1.0 + nrm(ks[4], (DEPTH, 2, D), 0.02)
    ln_b = nrm(ks[5], (DEPTH, 2, D), 0.02)
    attn_w_in = nrm(ks[6], (n_a, D, N_DIL_GROUPS * 3 * A_HEADS * A_HEAD_DIM), D ** -0.5)
    attn_w_out = nrm(ks[7], (n_a, A_HEADS * A_HEAD_DIM, D), BETA * (A_HEADS * A_HEAD_DIM) ** -0.5)
    gdn_w_in = nrm(ks[8], (n_b, D, 4 * GDN_WIDTH + 2 * H), D ** -0.5)
    gdn_conv_w = nrm(ks[9], (n_b, CONV_WIDTH, 3 * GDN_WIDTH), CONV_WIDTH ** -0.5)
    gdn_a_log = jnp.log(jax.random.uniform(ks[10], (n_b, H), jnp.float32, 1.0, 16.0))
    dt = jnp.exp(jax.random.uniform(ks[11], (n_b, H), jnp.float32,
                                    math.log(1e-3), math.log(1e-1)))
    gdn_dt_bias = jnp.log(jnp.expm1(dt))
    gdn_norm_w = 1.0 + nrm(ks[12], (n_b, GDN_HEAD_DIM), 0.02)
    gdn_w_out = nrm(ks[13], (n_b, GDN_WIDTH, D), BETA * GDN_WIDTH ** -0.5)
    router_w = nrm(ks[14], (DEPTH, D, N_EXPERTS), D ** -0.5)
    router_bias = nrm(ks[15], (DEPTH, N_EXPERTS), 0.01)
    expert_w1 = nrm(ks[16], (DEPTH, N_EXPERTS, D, EXPERT_DIM), D ** -0.5)
    expert_w3 = nrm(ks[17], (DEPTH, N_EXPERTS, D, EXPERT_DIM), D ** -0.5)
    expert_w2 = nrm(ks[18], (DEPTH, N_EXPERTS, EXPERT_DIM, D), BETA * EXPERT_DIM ** -0.5)
    shared_w1 = nrm(ks[19], (DEPTH, D, SHARED_DIM), D ** -0.5)
    shared_w3 = nrm(ks[20], (DEPTH, D, SHARED_DIM), D ** -0.5)
    shared_w2 = nrm(ks[21], (DEPTH, SHARED_DIM, D), BETA * SHARED_DIM ** -0.5)
    return {"x": x, "c": c, "positions": positions, "ada_w": ada_w, "ada_b": ada_b,
            "ln_g": ln_g, "ln_b": ln_b, "attn_w_in": attn_w_in, "attn_w_out": attn_w_out,
            "gdn_w_in": gdn_w_in, "gdn_conv_w": gdn_conv_w, "gdn_a_log": gdn_a_log,
            "gdn_dt_bias": gdn_dt_bias, "gdn_norm_w": gdn_norm_w, "gdn_w_out": gdn_w_out,
            "router_w": router_w, "router_bias": router_bias, "expert_w1": expert_w1,
            "expert_w3": expert_w3, "expert_w2": expert_w2, "shared_w1": shared_w1,
            "shared_w3": shared_w3, "shared_w2": shared_w2}


def reference(x, c, positions, ada_w, ada_b, ln_g, ln_b, attn_w_in, attn_w_out,
              gdn_w_in, gdn_conv_w, gdn_a_log, gdn_dt_bias, gdn_norm_w, gdn_w_out,
              router_w, router_bias, expert_w1, expert_w3, expert_w2,
              shared_w1, shared_w3, shared_w2):
    for i in range(DEPTH):
        j = i // N_MIXERS
        h, gate = modulate(x, c, ada_w[i, 0], ada_b[i, 0])
        if i % N_MIXERS == 0:
            y = dilated_attention_mixer(h, positions, attn_w_in[j], attn_w_out[j])
        else:
            y = gated_deltanet_mixer(h, gdn_w_in[j], gdn_conv_w[j], gdn_a_log[j],
                                     gdn_dt_bias[j], gdn_norm_w[j], gdn_w_out[j])
        x = deepnorm_residual(x, y, gate, ln_g[i, 0], ln_b[i, 0])
        h, gate = modulate(x, c, ada_w[i, 1], ada_b[i, 1])
        y = moe(h, router_w[i], router_bias[i], expert_w1[i], expert_w3[i], expert_w2[i],
                shared_w1[i], shared_w3[i], shared_w2[i])
        x = deepnorm_residual(x, y, gate, ln_g[i, 1], ln_b[i, 1])
    return x
```

```python
import functools
import math

import jax
import jax.numpy as jnp
from jax import lax
from jax.experimental import pallas as pl
from jax.experimental.pallas import tpu as pltpu

F32 = jnp.float32
BF16 = jnp.bfloat16

LANES = 128

DILATED_GROUPS = ((128, 1), (512, 4), (2048, 16))
A_HEADS = 16
A_HEAD_DIM = 64
ATTN_BLOCK = 128
ROPE_THETA = 10000.0
GDN_HEADS = 8
GDN_HEAD_DIM = 128
CONV_WIDTH = 4
GDN_CHUNK = 64
N_EXPERTS = 64
TOP_K = 8
N_EXPERT_GROUPS = 8
TOPK_GROUPS = 4
ROUTED_SCALE = 2.5
MOE_BLOCK = 128
LN_EPS = 1e-5
RMS_EPS = 1e-6
NEG_BIG = -1e30

VMEM_LIMIT = 56 * 1024 * 1024


def _params(*sem):
    return pltpu.CompilerParams(dimension_semantics=sem, vmem_limit_bytes=VMEM_LIMIT)


def _ada_kernel(c_ref, w_ref, b_ref, o_ref):
    c = c_ref[...]
    c_hi = c.astype(BF16)
    c_lo = (c - c_hi.astype(F32)).astype(BF16)
    w = w_ref[0]
    w_hi = w.astype(BF16)
    w_lo = (w - w_hi.astype(F32)).astype(BF16)
    acc = jnp.dot(c_hi, w_hi, preferred_element_type=F32)
    acc += jnp.dot(c_hi, w_lo, preferred_element_type=F32)
    acc += jnp.dot(c_lo, w_hi, preferred_element_type=F32)
    o_ref[0] = acc + b_ref[0]


def ada_vectors(c, ada_w, ada_b):
    depth2 = ada_w.shape[0] * ada_w.shape[1]
    b, d = c.shape
    d3 = ada_w.shape[-1]
    tn = d
    w = ada_w.reshape(depth2, d, d3)
    bias = ada_b.reshape(depth2, 1, d3)
    return pl.pallas_call(
        _ada_kernel,
        grid=(depth2, d3 // tn),
        in_specs=[
            pl.BlockSpec((b, d), lambda l, j: (0, 0)),
            pl.BlockSpec((1, d, tn), lambda l, j: (l, 0, j)),
            pl.BlockSpec((1, 1, tn), lambda l, j: (l, 0, j)),
        ],
        out_specs=pl.BlockSpec((1, b, tn), lambda l, j: (l, 0, j)),
        out_shape=jax.ShapeDtypeStruct((depth2, b, d3), F32),
        compiler_params=_params("parallel", "parallel"),
        name="ada_vectors",
    )(c, w, bias)


def _rope_kernel(pos_ref, freq_ref, sign_ref, cos_ref, sin_ref):
    ang = pos_ref[0].astype(F32) * freq_ref[...]
    cos_ref[0] = jnp.cos(ang)
    sin_ref[0] = jnp.sin(ang) * sign_ref[...]


def rope_tables(positions):
    b, s = positions.shape
    half = A_HEAD_DIM // 2
    inv_freq = ROPE_THETA ** (-jnp.arange(half, dtype=F32) / half)
    freq = jnp.tile(inv_freq, LANES // half).reshape(1, LANES)
    sign = jnp.tile(jnp.concatenate([-jnp.ones((half,), F32), jnp.ones((half,), F32)]),
                    LANES // A_HEAD_DIM).reshape(1, LANES)
    ts = min(s, 1024)
    return pl.pallas_call(
        _rope_kernel,
        grid=(b, s // ts),
        in_specs=[
            pl.BlockSpec((1, ts, 1), lambda i, j: (i, j, 0)),
            pl.BlockSpec((1, LANES), lambda i, j: (0, 0)),
            pl.BlockSpec((1, LANES), lambda i, j: (0, 0)),
        ],
        out_specs=[pl.BlockSpec((1, ts, LANES), lambda i, j: (i, j, 0))] * 2,
        out_shape=[jax.ShapeDtypeStruct((b, s, LANES), F32)] * 2,
        compiler_params=_params("parallel", "parallel"),
        name="rope_tables",
    )(positions.reshape(b, s, 1), freq, sign)


def _residue_major(ref, lead, dil, tl):
    if dil == 1:
        return ref[lead] if lead is not None else ref[...]
    parts = []
    for r in range(dil):
        if lead is None:
            parts.append(ref[pl.ds(r, tl, stride=dil), :])
        else:
            parts.append(ref[lead, pl.ds(r, tl, stride=dil), :])
    return jnp.concatenate(parts, axis=0)


def _attn_proj_kernel(x_ref, shift_ref, scale_ref, cos_ref, sin_ref, w_ref,
                      q_ref, k_ref, v_ref, h_scr, *, dil, tm):
    tl = tm // dil
    d = x_ref.shape[-1]
    h = x_ref[0] * (1.0 + scale_ref[0]) + shift_ref[0]
    if dil == 1:
        hb = h.astype(BF16)
    else:
        for cidx in range(d // LANES):
            h_scr[cidx] = h[:, cidx * LANES:(cidx + 1) * LANES]
        hb = jnp.concatenate(
            [_residue_major(h_scr, cidx, dil, tl).astype(BF16) for cidx in range(d // LANES)],
            axis=1)
    cosp = _residue_major(cos_ref, 0, dil, tl)
    sinp = _residue_major(sin_ref, 0, dil, tl)
    lane = lax.broadcasted_iota(jnp.int32, (tm, LANES), 1)
    first_half = (lane % A_HEAD_DIM) < (A_HEAD_DIM // 2)
    qscale = A_HEAD_DIM ** -0.5
    for j, out_ref in enumerate((q_ref, k_ref, v_ref)):
        res = jnp.dot(hb, w_ref[:, j * d:(j + 1) * d], preferred_element_type=F32)
        for hp in range(d // LANES):
            blk = res[:, hp * LANES:(hp + 1) * LANES]
            if j < 2:
                swapped = jnp.where(first_half,
                                    pltpu.roll(blk, LANES - A_HEAD_DIM // 2, 1),
                                    pltpu.roll(blk, A_HEAD_DIM // 2, 1))
                blk = blk * cosp + swapped * sinp
                if j == 0:
                    blk = blk * qscale
            blk = blk.astype(BF16)
            for r in range(dil):
                out_ref[0, hp, r] = blk[r * tl:(r + 1) * tl]


def attn_project(x, shift, scale, cos_t, sin_t, w_g, dil, tm=512):
    b, s, d = x.shape
    tl = tm // dil
    n_hp = d // LANES
    out_sds = jax.ShapeDtypeStruct((b, n_hp, dil, s // dil, LANES), BF16)
    out_spec = pl.BlockSpec((1, n_hp, dil, tl, LANES), lambda i, j: (i, 0, 0, j, 0))
    return pl.pallas_call(
        functools.partial(_attn_proj_kernel, dil=dil, tm=tm),
        grid=(b, s // tm),
        in_specs=[
            pl.BlockSpec((1, tm, d), lambda i, j: (i, j, 0)),
            pl.BlockSpec((1, 1, d), lambda i, j: (i, 0, 0)),
            pl.BlockSpec((1, 1, d), lambda i, j: (i, 0, 0)),
            pl.BlockSpec((1, tm, LANES), lambda i, j: (i, j, 0)),
            pl.BlockSpec((1, tm, LANES), lambda i, j: (i, j, 0)),
            pl.BlockSpec((d, 3 * d), lambda i, j: (0, 0)),
        ],
        out_specs=[out_spec] * 3,
        out_shape=[out_sds] * 3,
        scratch_shapes=[pltpu.VMEM((d // LANES, tm, LANES), F32)],
        compiler_params=_params("parallel", "parallel"),
        name=f"attn_project_d{dil}",
    )(x, shift, scale, cos_t, sin_t, w_g)


def _attn_kernel(*refs, s_len, dils):
    n_g = len(dils)
    qkv = refs[:3 * n_g]
    o_ref = refs[3 * n_g]
    oacc, lacc = refs[3 * n_g + 1:]
    blk = ATTN_BLOCK

    lane = lax.broadcasted_iota(jnp.int32, (blk, LANES), 1)
    head0 = lane < A_HEAD_DIM
    row = lax.broadcasted_iota(jnp.int32, (2 * blk, 2 * blk), 0) % blk
    col = lax.broadcasted_iota(jnp.int32, (2 * blk, 2 * blk), 1)
    bias_rest = jnp.where((col >= row) & (col <= row + blk), 0.0, NEG_BIG).astype(F32)
    bias_first = bias_rest[:, blk:]

    def one_block(g, q_ref, k_ref, v_ref, dil, r, n, first):
        q = q_ref[0, 0, r, pl.ds(pl.multiple_of(n * blk, blk), blk), :]
        zero = jnp.zeros_like(q)
        qq = jnp.concatenate([jnp.where(head0, q, zero), jnp.where(head0, zero, q)], axis=0)
        if first:
            keys = pl.ds(0, blk)
        else:
            keys = pl.ds(pl.multiple_of((n - 1) * blk, blk), 2 * blk)
        kc = k_ref[0, 0, r, keys, :]
        vc = v_ref[0, 0, r, keys, :]
        sc = lax.dot_general(qq, kc, (((1,), (1,)), ((), ())), preferred_element_type=F32)
        sc = sc + (bias_first if first else bias_rest)
        m = jnp.max(sc, axis=1, keepdims=True)
        p = jnp.exp(sc - m)
        l = jnp.sum(p, axis=1, keepdims=True)
        pv = jnp.dot(p.astype(BF16), vc, preferred_element_type=F32)
        pv = pv * (1.0 / l)
        lse = m + jnp.log(l)
        o_blk = jnp.where(head0, pv[:blk], pv[blk:])
        lse_blk = jnp.where(head0, jnp.broadcast_to(lse[:blk], (blk, LANES)),
                            jnp.broadcast_to(lse[blk:], (blk, LANES)))
        start = n * (blk * dil) + r
        if dil == 1:
            idx = pl.ds(pl.multiple_of(start, blk), blk)
        else:
            idx = pl.ds(start, blk, stride=dil)
        oacc[g, idx, :] = o_blk
        lacc[g, idx, :] = lse_blk

    for g, dil in enumerate(dils):
        q_ref, k_ref, v_ref = qkv[3 * g:3 * g + 3]
        nb = s_len // dil // blk

        def per_residue(r, carry, g=g, dil=dil, nb=nb, q_ref=q_ref, k_ref=k_ref, v_ref=v_ref):
            one_block(g, q_ref, k_ref, v_ref, dil, r, 0, True)

            def per_block(n, c2):
                one_block(g, q_ref, k_ref, v_ref, dil, r, n, False)
                return c2

            lax.fori_loop(1, nb, per_block, 0)
            return carry

        lax.fori_loop(0, dil, per_residue, 0)

    tc = 256

    def mix(i, carry):
        rows = pl.ds(pl.multiple_of(i * tc, tc), tc)
        ls = [lacc[g, rows, :] for g in range(n_g)]
        mx = functools.reduce(jnp.maximum, ls)
        es = [jnp.exp(l - mx) for l in ls]
        den = functools.reduce(lambda a, b2: a + b2, es)
        num = es[0] * oacc[0, rows, :]
        for g in range(1, n_g):
            num = num + es[g] * oacc[g, rows, :]
        o_ref[0, 0, rows, :] = (num * (1.0 / den)).astype(o_ref.dtype)
        return carry

    lax.fori_loop(0, s_len // tc, mix, 0)


def attn_core(qkvs, dils, s_len):
    b, n_hp = qkvs[0].shape[:2]
    in_specs = []
    for g, dil in enumerate(dils):
        spec = pl.BlockSpec((1, 1, dil, s_len // dil, LANES), lambda i, j: (i, j, 0, 0, 0))
        in_specs += [spec] * 3
    n_g = len(dils)
    return pl.pallas_call(
        functools.partial(_attn_kernel, s_len=s_len, dils=tuple(dils)),
        grid=(b, n_hp),
        in_specs=in_specs,
        out_specs=pl.BlockSpec((1, 1, s_len, LANES), lambda i, j: (i, j, 0, 0)),
        out_shape=jax.ShapeDtypeStruct((b, n_hp, s_len, LANES), BF16),
        scratch_shapes=[pltpu.VMEM((n_g, s_len, LANES), F32),
                        pltpu.VMEM((n_g, s_len, LANES), F32)],
        compiler_params=_params("parallel", "parallel"),
        name="attn_core",
    )(*qkvs)


def _layer_norm(z, g, b):
    mu = jnp.mean(z, axis=-1, keepdims=True)
    zc = z - mu
    var = jnp.mean(zc * zc, axis=-1, keepdims=True)
    return zc * lax.rsqrt(var + LN_EPS) * g + b


def _out_ln_kernel(o_ref, w_ref, x_ref, gate_ref, g_ref, b_ref, y_ref, *, alpha):
    n_hp = o_ref.shape[1]
    o = jnp.concatenate([o_ref[0, hp] for hp in range(n_hp)], axis=-1)
    y = jnp.dot(o, w_ref[...], preferred_element_type=F32)
    z = alpha * x_ref[0] + gate_ref[0] * y
    y_ref[0] = _layer_norm(z, g_ref[...], b_ref[...])


def out_proj_ln(o_heads, w_out, x, gate1p, ln_g, ln_b, alpha, tm=512):
    b, s, d = x.shape
    n_hp = o_heads.shape[1]
    return pl.pallas_call(
        functools.partial(_out_ln_kernel, alpha=alpha),
        grid=(b, s // tm),
        in_specs=[
            pl.BlockSpec((1, n_hp, tm, LANES), lambda i, j: (i, 0, j, 0)),
            pl.BlockSpec((d, d), lambda i, j: (0, 0)),
            pl.BlockSpec((1, tm, d), lambda i, j: (i, j, 0)),
            pl.BlockSpec((1, 1, d), lambda i, j: (i, 0, 0)),
            pl.BlockSpec((1, d), lambda i, j: (0, 0)),
            pl.BlockSpec((1, d), lambda i, j: (0, 0)),
        ],
        out_specs=pl.BlockSpec((1, tm, d), lambda i, j: (i, j, 0)),
        out_shape=jax.ShapeDtypeStruct((b, s, d), F32),
        compiler_params=_params("parallel", "parallel"),
        name="out_proj_ln",
    )(o_heads, w_out, x, gate1p, ln_g.reshape(1, d), ln_b.reshape(1, d))


def attention_layer(x, mods, cos_t, sin_t, w_in, w_out, ln_g, ln_b, alpha):
    b, s, d = x.shape
    shift = mods[:, None, 0:d]
    scale = mods[:, None, d:2 * d]
    gate1p = 1.0 + mods[:, None, 2 * d:3 * d]
    w_in = w_in.astype(BF16)
    dils = [dil for _, dil in DILATED_GROUPS]
    qkvs = []
    for g, dil in enumerate(dils):
        qkvs += attn_project(x, shift, scale, cos_t, sin_t,
                             w_in[:, g * 3 * d:(g + 1) * 3 * d], dil)
    o = attn_core(qkvs, dils, s)
    return out_proj_ln(o, w_out.astype(BF16), x, gate1p, ln_g, ln_b, alpha)


def _split_bf16(a):
    hi = a.astype(BF16)
    lo = (a - hi.astype(F32)).astype(BF16)
    return hi, lo


def _gdn_proj_kernel(x_ref, shift_ref, scale_ref, w_ref, wab_hi_ref, wab_lo_ref,
                     wabt_hi_ref, wabt_lo_ref, q_ref, k_ref, v_ref, z_ref, ab_ref, abt_ref, *, tm):
    d = x_ref.shape[-1]
    h = x_ref[0] * (1.0 + scale_ref[0]) + shift_ref[0]
    h_hi, h_lo = _split_bf16(h)
    for j, out_ref in enumerate((q_ref, k_ref, v_ref, z_ref)):
        res = jnp.dot(h_hi, w_ref[:, j * d:(j + 1) * d], preferred_element_type=F32)
        for hd in range(d // LANES):
            out_ref[0, hd] = res[:, hd * LANES:(hd + 1) * LANES].astype(BF16)
    ab = jnp.dot(h_hi, wab_hi_ref[...], preferred_element_type=F32)
    ab += jnp.dot(h_lo, wab_hi_ref[...], preferred_element_type=F32)
    ab += jnp.dot(h_hi, wab_lo_ref[...], preferred_element_type=F32)
    ab_ref[0] = ab
    nt = (((1,), (1,)), ((), ()))
    abt = lax.dot_general(wabt_hi_ref[...], h_hi, nt, preferred_element_type=F32)
    abt += lax.dot_general(wabt_hi_ref[...], h_lo, nt, preferred_element_type=F32)
    abt += lax.dot_general(wabt_lo_ref[...], h_hi, nt, preferred_element_type=F32)
    for n in range(tm // GDN_CHUNK):
        abt_ref[0, n] = abt[:, n * GDN_CHUNK:(n + 1) * GDN_CHUNK]


def gdn_project(x, shift, scale, w_in, tm=512):
    b, s, d = x.shape
    n_h = d // LANES
    w_main = w_in[:, :4 * d].astype(BF16)
    w_ab = jnp.pad(w_in[:, 4 * d:], ((0, 0), (0, LANES - 2 * GDN_HEADS)))
    w_abt = w_in[:, 4 * d:].T
    wab_hi, wab_lo = _split_bf16(w_ab)
    wabt_hi, wabt_lo = _split_bf16(w_abt)
    head_sds = jax.ShapeDtypeStruct((b, n_h, s, LANES), BF16)
    head_spec = pl.BlockSpec((1, n_h, tm, LANES), lambda i, j: (i, 0, j, 0))
    n_ab = 2 * GDN_HEADS
    return pl.pallas_call(
        functools.partial(_gdn_proj_kernel, tm=tm),
        grid=(b, s // tm),
        in_specs=[
            pl.BlockSpec((1, tm, d), lambda i, j: (i, j, 0)),
            pl.BlockSpec((1, 1, d), lambda i, j: (i, 0, 0)),
            pl.BlockSpec((1, 1, d), lambda i, j: (i, 0, 0)),
            pl.BlockSpec((d, 4 * d), lambda i, j: (0, 0)),
            pl.BlockSpec((d, LANES), lambda i, j: (0, 0)),
            pl.BlockSpec((d, LANES), lambda i, j: (0, 0)),
            pl.BlockSpec((n_ab, d), lambda i, j: (0, 0)),
            pl.BlockSpec((n_ab, d), lambda i, j: (0, 0)),
        ],
        out_specs=[head_spec] * 4 + [
            pl.BlockSpec((1, tm, LANES), lambda i, j: (i, j, 0)),
            pl.BlockSpec((1, tm // GDN_CHUNK, n_ab, GDN_CHUNK), lambda i, j: (i, j, 0, 0)),
        ],
        out_shape=[head_sds] * 4 + [
            jax.ShapeDtypeStruct((b, s, LANES), F32),
            jax.ShapeDtypeStruct((b, s // GDN_CHUNK, n_ab, GDN_CHUNK), F32),
        ],
        compiler_params=_params("parallel", "parallel"),
        name="gdn_project",
    )(x, shift, scale, w_main, wab_hi, wab_lo, wabt_hi, wabt_lo)


def _softplus(t):
    return jnp.maximum(t, 0.0) + jnp.log(1.0 + jnp.exp(-jnp.abs(t)))


def _sigmoid(t):
    return 1.0 / (1.0 + jnp.exp(-t))


def _dot3_exact_lhs(lhs_bf16, rhs_f32):
    acc = None
    rem = rhs_f32
    for _ in range(3):
        part = rem.astype(BF16)
        rem = rem - part.astype(F32)
        t = jnp.dot(lhs_bf16, part, preferred_element_type=F32)
        acc = t if acc is None else acc + t
    return acc


def _dot3_exact_rhs(lhs_f32, rhs_bf16):
    acc = None
    rem = lhs_f32
    for _ in range(3):
        part = rem.astype(BF16)
        rem = rem - part.astype(F32)
        t = jnp.dot(part, rhs_bf16, preferred_element_type=F32)
        acc = t if acc is None else acc + t
    return acc


def _gdn_kernel(q_ref, k_ref, v_ref, z_ref, ab_ref, abt_ref, convw_ref, alog_l_ref, dtb_l_ref,
                alog_c_ref, dtb_c_ref, normw_ref, o_ref,
                state, carry, qs, ks, vs, gcol_s, beta_s, *, tc):
    n_h = GDN_HEADS
    ch = GDN_CHUNK
    dh = GDN_HEAD_DIM
    halo = 8

    @pl.when(pl.program_id(1) == 0)
    def _():
        state[...] = jnp.zeros_like(state)
        carry[...] = jnp.zeros_like(carry)

    for a, (src, dst) in enumerate(((q_ref, qs), (k_ref, ks), (v_ref, vs))):
        for hd in range(n_h):
            xin = src[0, hd].astype(F32)
            xcat = jnp.concatenate([carry[a, hd], xin], axis=0)
            w = convw_ref[a, hd]
            y = xcat[halo:halo + tc] * w[CONV_WIDTH - 1:CONV_WIDTH]
            for j in range(CONV_WIDTH - 1):
                off = halo - (CONV_WIDTH - 1) + j
                y = y + xcat[off:off + tc] * w[j:j + 1]
            carry[a, hd] = xin[tc - halo:tc]
            y = y * _sigmoid(y)
            if a < 2:
                y = y * lax.rsqrt(jnp.sum(y * y, axis=-1, keepdims=True) + RMS_EPS)
                if a == 0:
                    y = y * (dh ** -0.5)
            dst[hd] = y

    ab = ab_ref[0]
    gcol_s[...] = -jnp.exp(alog_l_ref[...]) * _softplus(ab + dtb_l_ref[...])
    beta_s[...] = _sigmoid(ab)

    ri = lax.broadcasted_iota(jnp.int32, (ch, ch), 0)
    ci = lax.broadcasted_iota(jnp.int32, (ch, ch), 1)
    causal = ri >= ci
    strict = ri > ci
    tril = jnp.where(causal, 1.0, 0.0).astype(BF16)
    triu = jnp.where(ri <= ci, 1.0, 0.0).astype(BF16)
    eye = jnp.where(ri == ci, 1.0, 0.0).astype(F32)
    nt = (((1,), (1,)), ((), ()))
    tn = (((0,), (0,)), ((), ()))

    def chunk(cidx, carry_unused):
        rows = pl.ds(pl.multiple_of(cidx * ch, ch), ch)
        gc = _dot3_exact_lhs(tril, gcol_s[rows, :])
        a_t = abt_ref[0, cidx][0:n_h, :]
        g_t = -jnp.exp(alog_c_ref[...]) * _softplus(a_t + dtb_c_ref[...])
        gc_t = _dot3_exact_rhs(g_t, triu)
        beta = beta_s[rows, :]
        for hd in range(n_h):
            gcol = gc[:, hd:hd + 1]
            grow = gc_t[hd:hd + 1, :]
            bcol = beta[:, n_h + hd:n_h + hd + 1]
            decay = jnp.exp(jnp.where(causal, gcol - grow, NEG_BIG))
            qc = qs[hd, rows, :]
            kc = ks[hd, rows, :]
            vc = vs[hd, rows, :]
            kb = kc * bcol
            kcb = kc.astype(BF16)
            both = lax.dot_general(jnp.concatenate([kb, qc], axis=0).astype(BF16), kcb, nt,
                                   preferred_element_type=F32)
            lmat = jnp.where(strict, both[:ch] * decay, 0.0)
            qk = both[ch:] * decay
            pm = -lmat
            mm = lmat
            for _ in range(5):
                mb = mm.astype(BF16)
                mm = jnp.dot(mb, mb, preferred_element_type=F32)
                mb2 = mm.astype(BF16)
                pm = pm + mm + jnp.dot(pm.astype(BF16), mb2, preferred_element_type=F32)
            eg = jnp.exp(gcol)
            rhs = jnp.concatenate([vc * bcol, kb * eg], axis=1)
            sol = rhs + jnp.dot(pm.astype(BF16), rhs.astype(BF16), preferred_element_type=F32)
            u = sol[:, :dh]
            w = sol[:, dh:]
            glast = gcol[ch - 1:ch, :]
            kdec = kc * jnp.exp(glast - gcol)
            s_prev = state[hd]
            sb = s_prev.astype(BF16)
            wq = jnp.dot(jnp.concatenate([w, qc * eg], axis=0).astype(BF16), sb,
                         preferred_element_type=F32)
            v_new = u - wq[:ch]
            vb = v_new.astype(BF16)
            o = wq[ch:] + jnp.dot(qk.astype(BF16), vb, preferred_element_type=F32)
            state[hd] = s_prev * jnp.exp(glast) + lax.dot_general(
                kdec.astype(BF16), vb, tn, preferred_element_type=F32)
            o = o * lax.rsqrt(jnp.mean(o * o, axis=-1, keepdims=True) + RMS_EPS) * normw_ref[...]
            zc = z_ref[0, hd, rows, :].astype(F32)
            o_ref[0, hd, rows, :] = (o * (zc * _sigmoid(zc))).astype(o_ref.dtype)
        return carry_unused

    lax.fori_loop(0, tc // ch, chunk, 0)


def gdn_core(q, k, v, z, ab, abt, conv_w, a_log, dt_bias, norm_w, tc=512):
    b, n_h, s, dh = q.shape
    n_ab = 2 * n_h
    convw = conv_w.reshape(CONV_WIDTH, 3, n_h, dh).transpose(1, 2, 0, 3)
    pad = (0, LANES - n_h)
    alog_l = jnp.pad(a_log.astype(F32), pad).reshape(1, LANES)
    dtb_l = jnp.pad(dt_bias.astype(F32), pad).reshape(1, LANES)
    alog_c = a_log.astype(F32).reshape(n_h, 1)
    dtb_c = dt_bias.astype(F32).reshape(n_h, 1)
    head_spec = pl.BlockSpec((1, n_h, tc, dh), lambda i, j: (i, 0, j, 0))

    def full(shape):
        return pl.BlockSpec(shape, lambda i, j: (0,) * len(shape))

    return pl.pallas_call(
        functools.partial(_gdn_kernel, tc=tc),
        grid=(b, s // tc),
        in_specs=[head_spec] * 4 + [
            pl.BlockSpec((1, tc, LANES), lambda i, j: (i, j, 0)),
            pl.BlockSpec((1, tc // GDN_CHUNK, n_ab, GDN_CHUNK), lambda i, j: (i, j, 0, 0)),
            full((3, n_h, CONV_WIDTH, dh)),
            full((1, LANES)), full((1, LANES)), full((n_h, 1)), full((n_h, 1)), full((1, dh)),
        ],
        out_specs=head_spec,
        out_shape=jax.ShapeDtypeStruct((b, n_h, s, dh), BF16),
        scratch_shapes=[
            pltpu.VMEM((n_h, dh, dh), F32),
            pltpu.VMEM((3, n_h, 8, dh), F32),
            pltpu.VMEM((n_h, tc, dh), F32),
            pltpu.VMEM((n_h, tc, dh), F32),
            pltpu.VMEM((n_h, tc, dh), F32),
            pltpu.VMEM((tc, LANES), F32),
            pltpu.VMEM((tc, LANES), F32),
        ],
        compiler_params=_params("parallel", "arbitrary"),
        name="gdn_core",
    )(q, k, v, z, ab, abt, convw, alog_l, dtb_l, alog_c, dtb_c, norm_w.reshape(1, dh).astype(F32))


def gdn_layer(x, mods, w_in, conv_w, a_log, dt_bias, norm_w, w_out, ln_g, ln_b, alpha):
    b, s, d = x.shape
    shift = mods[:, None, 0:d]
    scale = mods[:, None, d:2 * d]
    gate1p = 1.0 + mods[:, None, 2 * d:3 * d]
    q, k, v, z, ab, abt = gdn_project(x, shift, scale, w_in)
    o = gdn_core(q, k, v, z, ab, abt, conv_w, a_log, dt_bias, norm_w)
    return out_proj_ln(o, w_out.astype(BF16), x, gate1p, ln_g, ln_b, alpha)


def _first_argmax(vals, idx, n):
    mx = jnp.max(vals, axis=0, keepdims=True)
    first = jnp.min(jnp.where(vals == mx, idx, n), axis=0, keepdims=True)
    return mx, first


def _router_kernel(x_ref, shift_ref, scale_ref, rwt_hi_ref, rwt_lo_ref, bias_ref,
                   eidx_ref, rank_ref, gcol_ref, cnt_ref, cnt_scr, *, tm):
    n_e = N_EXPERTS
    per_g = n_e // N_EXPERT_GROUPS

    @pl.when(pl.program_id(0) == 0)
    def _():
        cnt_scr[...] = jnp.zeros_like(cnt_scr)

    h = x_ref[...] * (1.0 + scale_ref[0]) + shift_ref[0]
    h_hi, h_lo = _split_bf16(h)
    nt = (((1,), (1,)), ((), ()))
    logits = lax.dot_general(rwt_hi_ref[...], h_hi, nt, preferred_element_type=F32)
    logits += lax.dot_general(rwt_hi_ref[...], h_lo, nt, preferred_element_type=F32)
    logits += lax.dot_general(rwt_lo_ref[...], h_hi, nt, preferred_element_type=F32)
    scores = _sigmoid(logits)
    biased = scores + bias_ref[...]

    r8 = lax.broadcasted_iota(jnp.int32, (per_g, tm), 0)
    gscores = []
    for g in range(N_EXPERT_GROUPS):
        blk = biased[g * per_g:(g + 1) * per_g]
        m1, i1 = _first_argmax(blk, r8, per_g)
        m2 = jnp.max(jnp.where(r8 == i1, NEG_BIG, blk), axis=0, keepdims=True)
        gscores.append(m1 + m2)
    vals = jnp.concatenate(gscores, axis=0)
    rg = lax.broadcasted_iota(jnp.int32, (N_EXPERT_GROUPS, tm), 0)
    gsel = jnp.zeros((N_EXPERT_GROUPS, tm), F32)
    for _ in range(TOPK_GROUPS):
        _, idx = _first_argmax(vals, rg, N_EXPERT_GROUPS)
        hit = rg == idx
        gsel = jnp.where(hit, 1.0, gsel)
        vals = jnp.where(hit, NEG_BIG, vals)
    emask = jnp.concatenate(
        [jnp.broadcast_to(gsel[g:g + 1], (per_g, tm)) for g in range(N_EXPERT_GROUPS)], axis=0)
    masked = jnp.where(emask > 0.5, biased, NEG_BIG)

    re = lax.broadcasted_iota(jnp.int32, (n_e, tm), 0)
    hits, eidx, gates = [], [], []
    for _ in range(TOP_K):
        _, idx = _first_argmax(masked, re, n_e)
        hit = re == idx
        gates.append(jnp.sum(jnp.where(hit, scores, 0.0), axis=0, keepdims=True))
        masked = jnp.where(hit, NEG_BIG, masked)
        hits.append(hit)
        eidx.append(idx)
    gates = jnp.concatenate(gates, axis=0)
    gates = gates / jnp.sum(gates, axis=0, keepdims=True) * ROUTED_SCALE
    onehot = jnp.zeros((n_e, tm), F32)
    for hit in hits:
        onehot = jnp.where(hit, 1.0, onehot)

    ti = lax.broadcasted_iota(jnp.int32, (tm, tm), 0)
    tj = lax.broadcasted_iota(jnp.int32, (tm, tm), 1)
    before = jnp.where(ti < tj, 1.0, 0.0).astype(BF16)
    prefix = jnp.dot(onehot.astype(BF16), before, preferred_element_type=F32) + cnt_scr[...]
    ranks = [jnp.sum(jnp.where(hit, prefix, 0.0), axis=0, keepdims=True) for hit in hits]
    cnt_scr[...] = cnt_scr[...] + jnp.sum(onehot, axis=1, keepdims=True)

    eidx_ref[...] = jnp.concatenate(eidx, axis=0)
    rank_ref[...] = jnp.concatenate(ranks, axis=0).astype(jnp.int32)
    cnt_ref[...] = jnp.broadcast_to(cnt_scr[...], cnt_ref.shape)
    gpad = jnp.concatenate([gates, jnp.zeros((LANES - TOP_K, tm), F32)], axis=0)
    gcol_ref[...] = gpad.T


def moe_route(xt, shift, scale, router_w, router_bias, s_len, tm=512):
    t, d = xt.shape
    n_e = router_w.shape[1]
    rwt_hi, rwt_lo = _split_bf16(router_w.T)
    per_b = s_len // tm
    return pl.pallas_call(
        functools.partial(_router_kernel, tm=tm),
        grid=(t // tm,),
        in_specs=[
            pl.BlockSpec((tm, d), lambda i: (i, 0)),
            pl.BlockSpec((1, 1, d), lambda i: (i // per_b, 0, 0)),
            pl.BlockSpec((1, 1, d), lambda i: (i // per_b, 0, 0)),
            pl.BlockSpec((n_e, d), lambda i: (0, 0)),
            pl.BlockSpec((n_e, d), lambda i: (0, 0)),
            pl.BlockSpec((n_e, 1), lambda i: (0, 0)),
        ],
        out_specs=[
            pl.BlockSpec((TOP_K, tm), lambda i: (0, i)),
            pl.BlockSpec((TOP_K, tm), lambda i: (0, i)),
            pl.BlockSpec((tm, LANES), lambda i: (i, 0)),
            pl.BlockSpec((n_e, LANES), lambda i: (0, 0)),
        ],
        out_shape=[
            jax.ShapeDtypeStruct((TOP_K, t), jnp.int32),
            jax.ShapeDtypeStruct((TOP_K, t), jnp.int32),
            jax.ShapeDtypeStruct((t, LANES), F32),
            jax.ShapeDtypeStruct((n_e, LANES), F32),
        ],
        scratch_shapes=[pltpu.VMEM((n_e, 1), F32)],
        compiler_params=_params("arbitrary"),
        name="moe_route",
    )(xt, shift, scale, rwt_hi, rwt_lo, router_bias.reshape(n_e, 1).astype(F32))


def _slots_kernel(cnt_ref, eidx_ref, rank_ref, dest_ref, blk_e_ref, nused_ref, *, block_rows):
    n_e = N_EXPERTS
    cnt = cnt_ref[...].astype(jnp.int32)
    padded = ((cnt + (block_rows - 1)) // block_rows) * block_rows
    ri = lax.broadcasted_iota(jnp.int32, (n_e, n_e), 0)
    ci = lax.broadcasted_iota(jnp.int32, (n_e, n_e), 1)
    tril = jnp.where(ri >= ci, 1.0, 0.0).astype(BF16)
    pend = _dot3_exact_lhs(tril, padded.astype(F32))
    pstart = pend.astype(jnp.int32) - padded
    eidx = eidx_ref[...]
    dest = rank_ref[...]
    for e in range(n_e):
        dest = dest + jnp.where(eidx == e, pstart[e:e + 1, 0:1], 0)
    dest_ref[...] = dest
    nb = blk_e_ref.shape[1]
    first_row = lax.broadcasted_iota(jnp.int32, (n_e, nb), 1) * block_rows
    pend_i = pend.astype(jnp.int32)[:, 0:1]
    owner = jnp.sum(jnp.where(pend_i <= first_row, 1, 0), axis=0, keepdims=True)
    blk_e_ref[...] = jnp.minimum(owner, n_e - 1)
    nused_ref[...] = pend_i[n_e - 1:n_e, :] // block_rows + jnp.zeros(nused_ref.shape, jnp.int32)


def moe_slots(counts, eidx, rank, n_blocks, block_rows, tt=2048):
    t = eidx.shape[1]
    tt = min(tt, t)
    nb_pad = -(-n_blocks // LANES) * LANES
    n_e = counts.shape[0]
    return pl.pallas_call(
        functools.partial(_slots_kernel, block_rows=block_rows),
        grid=(t // tt,),
        in_specs=[
            pl.BlockSpec((n_e, LANES), lambda i: (0, 0)),
            pl.BlockSpec((TOP_K, tt), lambda i: (0, i)),
            pl.BlockSpec((TOP_K, tt), lambda i: (0, i)),
        ],
        out_specs=[
            pl.BlockSpec((TOP_K, tt), lambda i: (0, i)),
            pl.BlockSpec((1, nb_pad), lambda i: (0, 0)),
            pl.BlockSpec((1, LANES), lambda i: (0, 0)),
        ],
        out_shape=[
            jax.ShapeDtypeStruct((TOP_K, t), jnp.int32),
            jax.ShapeDtypeStruct((1, nb_pad), jnp.int32),
            jax.ShapeDtypeStruct((1, LANES), jnp.int32),
        ],
        compiler_params=_params("arbitrary"),
        name="moe_slots",
    )(counts, eidx, rank)


def _row_copy(src, src_row, dst, dst_row, sem):
    return pltpu.make_async_copy(src.at[pl.ds(src_row, 1)], dst.at[pl.ds(dst_row, 1)], sem)


def _dispatch_kernel(dest_ref, x_ref, shift_ref, scale_ref, xs_hbm, hs, sem, *, tm, t_total):
    base = pl.program_id(0) * tm
    hs[...] = x_ref[...] * (1.0 + scale_ref[0]) + shift_ref[0]

    def issue(t, carry):
        for k in range(TOP_K):
            _row_copy(hs, t, xs_hbm, dest_ref[k * t_total + base + t], sem).start()
        return carry

    lax.fori_loop(0, tm, issue, 0)

    def drain(t, carry):
        for k in range(TOP_K):
            _row_copy(hs, 0, xs_hbm, 0, sem).wait()
        return carry

    lax.fori_loop(0, tm, drain, 0)


def moe_dispatch(dest_flat, xt, shift, scale, n_rows, s_len, tm=256):
    t, d = xt.shape
    per_b = s_len // tm
    grid_spec = pltpu.PrefetchScalarGridSpec(
        num_scalar_prefetch=1,
        grid=(t // tm,),
        in_specs=[
            pl.BlockSpec((tm, d), lambda i, dest: (i, 0)),
            pl.BlockSpec((1, 1, d), lambda i, dest: (i // per_b, 0, 0)),
            pl.BlockSpec((1, 1, d), lambda i, dest: (i // per_b, 0, 0)),
        ],
        out_specs=pl.BlockSpec(memory_space=pl.ANY),
        scratch_shapes=[pltpu.VMEM((tm, d), F32), pltpu.SemaphoreType.DMA],
    )
    return pl.pallas_call(
        functools.partial(_dispatch_kernel, tm=tm, t_total=t),
        grid_spec=grid_spec,
        out_shape=jax.ShapeDtypeStruct((n_rows, d), F32),
        compiler_params=_params("arbitrary"),
        name="moe_dispatch",
    )(dest_flat, xt, shift, scale)


def _expert_kernel(blk_e_ref, nused_ref, xs_ref, w1_ref, w3_ref, w2_ref, ys_ref):
    @pl.when(pl.program_id(0) < nused_ref[0])
    def _():
        xb = xs_ref[...].astype(BF16)
        h1 = jnp.dot(xb, w1_ref[0], preferred_element_type=F32)
        h3 = jnp.dot(xb, w3_ref[0], preferred_element_type=F32)
        act = (h1 * _sigmoid(h1)) * h3
        ys_ref[...] = jnp.dot(act.astype(BF16), w2_ref[0], preferred_element_type=F32)


def moe_experts(blk_e, nused, xs, w1, w3, w2, block_rows):
    n_rows, d = xs.shape
    n_blocks = n_rows // block_rows
    de = w1.shape[-1]

    def row_map(i, be, nu):
        return (jnp.minimum(i, nu[0] - 1), 0)

    grid_spec = pltpu.PrefetchScalarGridSpec(
        num_scalar_prefetch=2,
        grid=(n_blocks,),
        in_specs=[
            pl.BlockSpec((block_rows, d), row_map),
            pl.BlockSpec((1, d, de), lambda i, be, nu: (be[i], 0, 0)),
            pl.BlockSpec((1, d, de), lambda i, be, nu: (be[i], 0, 0)),
            pl.BlockSpec((1, de, d), lambda i, be, nu: (be[i], 0, 0)),
        ],
        out_specs=pl.BlockSpec((block_rows, d), row_map),
    )
    return pl.pallas_call(
        _expert_kernel,
        grid_spec=grid_spec,
        out_shape=jax.ShapeDtypeStruct((n_rows, d), F32),
        compiler_params=_params("arbitrary"),
        name="moe_experts",
    )(blk_e, nused, xs, w1, w3, w2)


def _combine_kernel(dest_ref, ys_hbm, x_ref, shift_ref, scale_ref, gate_ref, gcol_ref,
                    ws1_ref, ws3_ref, ws2_ref, g_ref, b_ref, o_ref, buf, sem,
                    *, tm, t_total, alpha):
    base = pl.program_id(0) * tm

    def issue(t, carry):
        for k in range(TOP_K):
            _row_copy(ys_hbm, dest_ref[k * t_total + base + t], buf.at[k], t, sem).start()
        return carry

    lax.fori_loop(0, tm, issue, 0)

    x = x_ref[...]
    hb = (x * (1.0 + scale_ref[0]) + shift_ref[0]).astype(BF16)
    h1 = jnp.dot(hb, ws1_ref[...], preferred_element_type=F32)
    h3 = jnp.dot(hb, ws3_ref[...], preferred_element_type=F32)
    act = (h1 * _sigmoid(h1)) * h3
    y = jnp.dot(act.astype(BF16), ws2_ref[...], preferred_element_type=F32)

    def drain(t, carry):
        for k in range(TOP_K):
            _row_copy(ys_hbm, 0, buf.at[k], 0, sem).wait()
        return carry

    lax.fori_loop(0, tm, drain, 0)

    gcol = gcol_ref[...]
    for k in range(TOP_K):
        y = y + buf[k] * gcol[:, k:k + 1]
    z = alpha * x + gate_ref[0] * y
    o_ref[...] = _layer_norm(z, g_ref[...], b_ref[...])


def moe_combine(dest_flat, ys, xt, shift, scale, gate1p, gcol, ws1, ws3, ws2, ln_g, ln_b,
                s_len, alpha, tm=128):
    t, d = xt.shape
    ds_ = ws1.shape[1]
    per_b = s_len // tm

    def const(shape):
        return pl.BlockSpec(shape, lambda i, dest: (0,) * len(shape))

    mod_spec = pl.BlockSpec((1, 1, d), lambda i, dest: (i // per_b, 0, 0))
    grid_spec = pltpu.PrefetchScalarGridSpec(
        num_scalar_prefetch=1,
        grid=(t // tm,),
        in_specs=[
            pl.BlockSpec(memory_space=pl.ANY),
            pl.BlockSpec((tm, d), lambda i, dest: (i, 0)),
            mod_spec, mod_spec, mod_spec,
            pl.BlockSpec((tm, LANES), lambda i, dest: (i, 0)),
            const((d, ds_)), const((d, ds_)), const((ds_, d)),
            const((1, d)), const((1, d)),
        ],
        out_specs=pl.BlockSpec((tm, d), lambda i, dest: (i, 0)),
        scratch_shapes=[pltpu.VMEM((TOP_K, tm, d), F32), pltpu.SemaphoreType.DMA],
    )
    return pl.pallas_call(
        functools.partial(_combine_kernel, tm=tm, t_total=t, alpha=alpha),
        grid_spec=grid_spec,
        out_shape=jax.ShapeDtypeStruct((t, d), F32),
        compiler_params=_params("arbitrary"),
        name="moe_combine",
    )(dest_flat, ys, xt, shift, scale, gate1p, gcol, ws1, ws3, ws2,
      ln_g.reshape(1, d), ln_b.reshape(1, d))


def moe_layer(x, mods, router_w, router_bias, w1, w3, w2, ws1, ws3, ws2, ln_g, ln_b, alpha):
    b, s, d = x.shape
    t = b * s
    shift = mods[:, None, 0:d]
    scale = mods[:, None, d:2 * d]
    gate1p = 1.0 + mods[:, None, 2 * d:3 * d]
    xt = x.reshape(t, d)
    n_blocks = t * TOP_K // MOE_BLOCK + N_EXPERTS
    eidx, rank, gcol, counts = moe_route(xt, shift, scale, router_w, router_bias, s)
    dest, blk_e, nused = moe_slots(counts, eidx, rank, n_blocks, MOE_BLOCK)
    dest_flat = dest.reshape(-1)
    xs = moe_dispatch(dest_flat, xt, shift, scale, n_blocks * MOE_BLOCK, s)
    ys = moe_experts(blk_e.reshape(-1), nused.reshape(-1), xs,
                     w1.astype(BF16), w3.astype(BF16), w2.astype(BF16), MOE_BLOCK)
    out = moe_combine(dest_flat, ys, xt, shift, scale, gate1p, gcol,
                      ws1.astype(BF16), ws3.astype(BF16), ws2.astype(BF16), ln_g, ln_b, s, alpha)
    return out.reshape(b, s, d)


def kernel(x, c, positions, ada_w, ada_b, ln_g, ln_b, attn_w_in, attn_w_out, gdn_w_in, gdn_conv_w,
           gdn_a_log, gdn_dt_bias, gdn_norm_w, gdn_w_out, router_w, router_bias, expert_w1,
           expert_w3, expert_w2, shared_w1, shared_w3, shared_w2):
    depth = ada_w.shape[0]
    alpha = (2 * depth) ** 0.25
    mods = ada_vectors(c, ada_w, ada_b)
    cos_t, sin_t = rope_tables(positions)
    for i in range(depth):
        j = i // 2
        if i % 2 == 0:
            x = attention_layer(x, mods[2 * i], cos_t, sin_t, attn_w_in[j], attn_w_out[j],
                                ln_g[i, 0], ln_b[i, 0], alpha)
        else:
            x = gdn_layer(x, mods[2 * i], gdn_w_in[j], gdn_conv_w[j], gdn_a_log[j],
                          gdn_dt_bias[j], gdn_norm_w[j], gdn_w_out[j], ln_g[i, 0], ln_b[i, 0],
                          alpha)
        x = moe_layer(x, mods[2 * i + 1], router_w[i], router_bias[i], expert_w1[i],
                      expert_w3[i], expert_w2[i], shared_w1[i], shared_w3[i], shared_w2[i],
                      ln_g[i, 1], ln_b[i, 1], alpha)
    return x
```

```python
import functools
import math

import jax
import jax.numpy as jnp
from jax import lax
from jax.experimental import pallas as pl
from jax.experimental.pallas import tpu as pltpu

F32 = jnp.float32
BF16 = jnp.bfloat16

LANES = 128

DILATED_GROUPS = ((128, 1), (512, 4), (2048, 16))
A_HEADS = 16
A_HEAD_DIM = 64
ATTN_BLOCK = 128
ROPE_THETA = 10000.0
GDN_HEADS = 8
GDN_HEAD_DIM = 128
CONV_WIDTH = 4
GDN_CHUNK = 64
N_EXPERTS = 64
TOP_K = 8
N_EXPERT_GROUPS = 8
TOPK_GROUPS = 4
ROUTED_SCALE = 2.5
MOE_BLOCK = 256
LN_EPS = 1e-5
RMS_EPS = 1e-6
NEG_BIG = -1e30

VMEM_LIMIT = 56 * 1024 * 1024


def _params(*sem):
    return pltpu.CompilerParams(dimension_semantics=sem, vmem_limit_bytes=VMEM_LIMIT)


def _ada_kernel(c_ref, w_ref, b_ref, o_ref):
    c = c_ref[...]
    c_hi = c.astype(BF16)
    c_lo = (c - c_hi.astype(F32)).astype(BF16)
    w = w_ref[0]
    w_hi = w.astype(BF16)
    w_lo = (w - w_hi.astype(F32)).astype(BF16)
    acc = jnp.dot(c_hi, w_hi, preferred_element_type=F32)
    acc += jnp.dot(c_hi, w_lo, preferred_element_type=F32)
    acc += jnp.dot(c_lo, w_hi, preferred_element_type=F32)
    o_ref[0] = acc + b_ref[0]


def ada_vectors(c, ada_w, ada_b):
    depth2 = ada_w.shape[0] * ada_w.shape[1]
    b, d = c.shape
    d3 = ada_w.shape[-1]
    tn = d
    w = ada_w.reshape(depth2, d, d3)
    bias = ada_b.reshape(depth2, 1, d3)
    return pl.pallas_call(
        _ada_kernel,
        grid=(depth2, d3 // tn),
        in_specs=[
            pl.BlockSpec((b, d), lambda l, j: (0, 0)),
            pl.BlockSpec((1, d, tn), lambda l, j: (l, 0, j)),
            pl.BlockSpec((1, 1, tn), lambda l, j: (l, 0, j)),
        ],
        out_specs=pl.BlockSpec((1, b, tn), lambda l, j: (l, 0, j)),
        out_shape=jax.ShapeDtypeStruct((depth2, b, d3), F32),
        compiler_params=_params("parallel", "parallel"),
        name="ada_vectors",
    )(c, w, bias)


def _rope_kernel(pos_ref, freq_ref, sign_ref, cos_ref, sin_ref):
    ang = pos_ref[0].astype(F32) * freq_ref[...]
    cos_ref[0] = jnp.cos(ang)
    sin_ref[0] = jnp.sin(ang) * sign_ref[...]


def rope_tables(positions):
    b, s = positions.shape
    half = A_HEAD_DIM // 2
    inv_freq = ROPE_THETA ** (-jnp.arange(half, dtype=F32) / half)
    freq = jnp.tile(inv_freq, LANES // half).reshape(1, LANES)
    sign = jnp.tile(jnp.concatenate([-jnp.ones((half,), F32), jnp.ones((half,), F32)]),
                    LANES // A_HEAD_DIM).reshape(1, LANES)
    ts = min(s, 1024)
    return pl.pallas_call(
        _rope_kernel,
        grid=(b, s // ts),
        in_specs=[
            pl.BlockSpec((1, ts, 1), lambda i, j: (i, j, 0)),
            pl.BlockSpec((1, LANES), lambda i, j: (0, 0)),
            pl.BlockSpec((1, LANES), lambda i, j: (0, 0)),
        ],
        out_specs=[pl.BlockSpec((1, ts, LANES), lambda i, j: (i, j, 0))] * 2,
        out_shape=[jax.ShapeDtypeStruct((b, s, LANES), F32)] * 2,
        compiler_params=_params("parallel", "parallel"),
        name="rope_tables",
    )(positions.reshape(b, s, 1), freq, sign)


def _residue_major(ref, lead, dil, tl):
    if dil == 1:
        return ref[lead] if lead is not None else ref[...]
    parts = []
    for r in range(dil):
        if lead is None:
            parts.append(ref[pl.ds(r, tl, stride=dil), :])
        else:
            parts.append(ref[lead, pl.ds(r, tl, stride=dil), :])
    return jnp.concatenate(parts, axis=0)


def _attn_proj_kernel(x_ref, shift_ref, scale_ref, cos_ref, sin_ref, w_ref,
                      q_ref, k_ref, v_ref, h_scr, *, dil, tm):
    tl = tm // dil
    d = x_ref.shape[-1]
    h = x_ref[0] * (1.0 + scale_ref[0]) + shift_ref[0]
    if dil == 1:
        hb = h.astype(BF16)
    else:
        for cidx in range(d // LANES):
            h_scr[cidx] = h[:, cidx * LANES:(cidx + 1) * LANES]
        hb = jnp.concatenate(
            [_residue_major(h_scr, cidx, dil, tl).astype(BF16) for cidx in range(d // LANES)],
            axis=1)
    cosp = _residue_major(cos_ref, 0, dil, tl)
    sinp = _residue_major(sin_ref, 0, dil, tl)
    lane = lax.broadcasted_iota(jnp.int32, (tm, LANES), 1)
    first_half = (lane % A_HEAD_DIM) < (A_HEAD_DIM // 2)
    qscale = A_HEAD_DIM ** -0.5
    for j, out_ref in enumerate((q_ref, k_ref, v_ref)):
        res = jnp.dot(hb, w_ref[:, j * d:(j + 1) * d], preferred_element_type=F32)
        for hp in range(d // LANES):
            blk = res[:, hp * LANES:(hp + 1) * LANES]
            if j < 2:
                swapped = jnp.where(first_half,
                                    pltpu.roll(blk, LANES - A_HEAD_DIM // 2, 1),
                                    pltpu.roll(blk, A_HEAD_DIM // 2, 1))
                blk = blk * cosp + swapped * sinp
                if j == 0:
                    blk = blk * qscale
            blk = blk.astype(BF16)
            for r in range(dil):
                out_ref[0, hp, r] = blk[r * tl:(r + 1) * tl]


def attn_project(x, shift, scale, cos_t, sin_t, w_g, dil, tm=512):
    b, s, d = x.shape
    tl = tm // dil
    n_hp = d // LANES
    out_sds = jax.ShapeDtypeStruct((b, n_hp, dil, s // dil, LANES), BF16)
    out_spec = pl.BlockSpec((1, n_hp, dil, tl, LANES), lambda i, j: (i, 0, 0, j, 0))
    return pl.pallas_call(
        functools.partial(_attn_proj_kernel, dil=dil, tm=tm),
        grid=(b, s // tm),
        in_specs=[
            pl.BlockSpec((1, tm, d), lambda i, j: (i, j, 0)),
            pl.BlockSpec((1, 1, d), lambda i, j: (i, 0, 0)),
            pl.BlockSpec((1, 1, d), lambda i, j: (i, 0, 0)),
            pl.BlockSpec((1, tm, LANES), lambda i, j: (i, j, 0)),
            pl.BlockSpec((1, tm, LANES), lambda i, j: (i, j, 0)),
            pl.BlockSpec((d, 3 * d), lambda i, j: (0, 0)),
        ],
        out_specs=[out_spec] * 3,
        out_shape=[out_sds] * 3,
        scratch_shapes=[pltpu.VMEM((d // LANES, tm, LANES), F32)],
        compiler_params=_params("parallel", "parallel"),
        name=f"attn_project_d{dil}",
    )(x, shift, scale, cos_t, sin_t, w_g)


def _attn_kernel(*refs, s_len, dils):
    n_g = len(dils)
    qkv = refs[:3 * n_g]
    o_ref = refs[3 * n_g]
    oacc, lacc = refs[3 * n_g + 1:]
    blk = ATTN_BLOCK

    lane = lax.broadcasted_iota(jnp.int32, (blk, LANES), 1)
    head0 = lane < A_HEAD_DIM
    row = lax.broadcasted_iota(jnp.int32, (2 * blk, 2 * blk), 0) % blk
    col = lax.broadcasted_iota(jnp.int32, (2 * blk, 2 * blk), 1)
    bias_rest = jnp.where((col >= row) & (col <= row + blk), 0.0, NEG_BIG).astype(F32)
    bias_first = bias_rest[:, blk:]

    def run_blocks(g, q_ref, k_ref, v_ref, dil, r, specs):
        ids = range(len(specs))
        qq, kc, vc = [], [], []
        for n, first in specs:
            q = q_ref[0, 0, r, pl.ds(pl.multiple_of(n * blk, blk), blk), :]
            zero = jnp.zeros_like(q)
            qq.append(jnp.concatenate([jnp.where(head0, q, zero), jnp.where(head0, zero, q)],
                                      axis=0))
            if first:
                keys = pl.ds(0, blk)
            else:
                keys = pl.ds(pl.multiple_of((n - 1) * blk, blk), 2 * blk)
            kc.append(k_ref[0, 0, r, keys, :])
            vc.append(v_ref[0, 0, r, keys, :])
        sc = [lax.dot_general(qq[i], kc[i], (((1,), (1,)), ((), ())), preferred_element_type=F32)
              + (bias_first if specs[i][1] else bias_rest) for i in ids]
        m = [jnp.max(sc[i], axis=1, keepdims=True) for i in ids]
        p = [jnp.exp(sc[i] - m[i]) for i in ids]
        l = [jnp.sum(p[i], axis=1, keepdims=True) for i in ids]
        pv = [jnp.dot(p[i].astype(BF16), vc[i], preferred_element_type=F32) * (1.0 / l[i])
              for i in ids]
        for i in ids:
            n = specs[i][0]
            lse = m[i] + jnp.log(l[i])
            o_blk = jnp.where(head0, pv[i][:blk], pv[i][blk:])
            lse_blk = jnp.where(head0, jnp.broadcast_to(lse[:blk], (blk, LANES)),
                                jnp.broadcast_to(lse[blk:], (blk, LANES)))
            start = n * (blk * dil) + r
            if dil == 1:
                idx = pl.ds(pl.multiple_of(start, blk), blk)
            else:
                idx = pl.ds(start, blk, stride=dil)
            oacc[g, idx, :] = o_blk
            lacc[g, idx, :] = lse_blk

    for g, dil in enumerate(dils):
        q_ref, k_ref, v_ref = qkv[3 * g:3 * g + 3]
        nb = s_len // dil // blk

        def per_residue(r, carry, g=g, dil=dil, nb=nb, q_ref=q_ref, k_ref=k_ref, v_ref=v_ref):
            if nb == 1:
                run_blocks(g, q_ref, k_ref, v_ref, dil, r, [(0, True)])
                return carry
            run_blocks(g, q_ref, k_ref, v_ref, dil, r, [(0, True), (1, False)])

            def per_pair(i, c2):
                run_blocks(g, q_ref, k_ref, v_ref, dil, r, [(2 * i, False), (2 * i + 1, False)])
                return c2

            lax.fori_loop(1, nb // 2, per_pair, 0)
            return carry

        lax.fori_loop(0, dil, per_residue, 0)

    tc = 256

    def mix(i, carry):
        rows = pl.ds(pl.multiple_of(i * tc, tc), tc)
        ls = [lacc[g, rows, :] for g in range(n_g)]
        mx = functools.reduce(jnp.maximum, ls)
        es = [jnp.exp(l - mx) for l in ls]
        den = functools.reduce(lambda a, b2: a + b2, es)
        num = es[0] * oacc[0, rows, :]
        for g in range(1, n_g):
            num = num + es[g] * oacc[g, rows, :]
        o_ref[0, 0, rows, :] = (num * (1.0 / den)).astype(o_ref.dtype)
        return carry

    lax.fori_loop(0, s_len // tc, mix, 0)


def attn_core(qkvs, dils, s_len):
    b, n_hp = qkvs[0].shape[:2]
    in_specs = []
    for g, dil in enumerate(dils):
        spec = pl.BlockSpec((1, 1, dil, s_len // dil, LANES), lambda i, j: (i, j, 0, 0, 0))
        in_specs += [spec] * 3
    n_g = len(dils)
    return pl.pallas_call(
        functools.partial(_attn_kernel, s_len=s_len, dils=tuple(dils)),
        grid=(b, n_hp),
        in_specs=in_specs,
        out_specs=pl.BlockSpec((1, 1, s_len, LANES), lambda i, j: (i, j, 0, 0)),
        out_shape=jax.ShapeDtypeStruct((b, n_hp, s_len, LANES), BF16),
        scratch_shapes=[pltpu.VMEM((n_g, s_len, LANES), F32),
                        pltpu.VMEM((n_g, s_len, LANES), F32)],
        compiler_params=_params("parallel", "parallel"),
        name="attn_core",
    )(*qkvs)


def _layer_norm(z, g, b):
    mu = jnp.mean(z, axis=-1, keepdims=True)
    zc = z - mu
    var = jnp.mean(zc * zc, axis=-1, keepdims=True)
    return zc * lax.rsqrt(var + LN_EPS) * g + b


def _out_ln_kernel(o_ref, w_ref, x_ref, gate_ref, g_ref, b_ref, y_ref, *, alpha):
    n_hp = o_ref.shape[1]
    o = jnp.concatenate([o_ref[0, hp] for hp in range(n_hp)], axis=-1)
    y = jnp.dot(o, w_ref[...], preferred_element_type=F32)
    z = alpha * x_ref[0] + gate_ref[0] * y
    y_ref[0] = _layer_norm(z, g_ref[...], b_ref[...])


def out_proj_ln(o_heads, w_out, x, gate1p, ln_g, ln_b, alpha, tm=512):
    b, s, d = x.shape
    n_hp = o_heads.shape[1]
    return pl.pallas_call(
        functools.partial(_out_ln_kernel, alpha=alpha),
        grid=(b, s // tm),
        in_specs=[
            pl.BlockSpec((1, n_hp, tm, LANES), lambda i, j: (i, 0, j, 0)),
            pl.BlockSpec((d, d), lambda i, j: (0, 0)),
            pl.BlockSpec((1, tm, d), lambda i, j: (i, j, 0)),
            pl.BlockSpec((1, 1, d), lambda i, j: (i, 0, 0)),
            pl.BlockSpec((1, d), lambda i, j: (0, 0)),
            pl.BlockSpec((1, d), lambda i, j: (0, 0)),
        ],
        out_specs=pl.BlockSpec((1, tm, d), lambda i, j: (i, j, 0)),
        out_shape=jax.ShapeDtypeStruct((b, s, d), F32),
        compiler_params=_params("parallel", "parallel"),
        name="out_proj_ln",
    )(o_heads, w_out, x, gate1p, ln_g.reshape(1, d), ln_b.reshape(1, d))


def attention_layer(x, mods, cos_t, sin_t, w_in, w_out, ln_g, ln_b, alpha):
    b, s, d = x.shape
    shift = mods[:, None, 0:d]
    scale = mods[:, None, d:2 * d]
    gate1p = 1.0 + mods[:, None, 2 * d:3 * d]
    w_in = w_in.astype(BF16)
    dils = [dil for _, dil in DILATED_GROUPS]
    qkvs = []
    for g, dil in enumerate(dils):
        qkvs += attn_project(x, shift, scale, cos_t, sin_t,
                             w_in[:, g * 3 * d:(g + 1) * 3 * d], dil)
    o = attn_core(qkvs, dils, s)
    return out_proj_ln(o, w_out.astype(BF16), x, gate1p, ln_g, ln_b, alpha)


def _split_bf16(a):
    hi = a.astype(BF16)
    lo = (a - hi.astype(F32)).astype(BF16)
    return hi, lo


def _gdn_proj_kernel(x_ref, shift_ref, scale_ref, w_ref, wab_hi_ref, wab_lo_ref,
                     wabt_hi_ref, wabt_lo_ref, q_ref, k_ref, v_ref, z_ref, ab_ref, abt_ref, *, tm):
    d = x_ref.shape[-1]
    h = x_ref[0] * (1.0 + scale_ref[0]) + shift_ref[0]
    h_hi, h_lo = _split_bf16(h)
    for j, out_ref in enumerate((q_ref, k_ref, v_ref, z_ref)):
        res = jnp.dot(h_hi, w_ref[:, j * d:(j + 1) * d], preferred_element_type=F32)
        for hd in range(d // LANES):
            out_ref[0, hd] = res[:, hd * LANES:(hd + 1) * LANES].astype(BF16)
    ab = jnp.dot(h_hi, wab_hi_ref[...], preferred_element_type=F32)
    ab += jnp.dot(h_lo, wab_hi_ref[...], preferred_element_type=F32)
    ab += jnp.dot(h_hi, wab_lo_ref[...], preferred_element_type=F32)
    ab_ref[0] = ab
    nt = (((1,), (1,)), ((), ()))
    abt = lax.dot_general(wabt_hi_ref[...], h_hi, nt, preferred_element_type=F32)
    abt += lax.dot_general(wabt_hi_ref[...], h_lo, nt, preferred_element_type=F32)
    abt += lax.dot_general(wabt_lo_ref[...], h_hi, nt, preferred_element_type=F32)
    for n in range(tm // GDN_CHUNK):
        abt_ref[0, n] = abt[:, n * GDN_CHUNK:(n + 1) * GDN_CHUNK]


def gdn_project(x, shift, scale, w_in, tm=512):
    b, s, d = x.shape
    n_h = d // LANES
    w_main = w_in[:, :4 * d].astype(BF16)
    w_ab = jnp.pad(w_in[:, 4 * d:], ((0, 0), (0, LANES - 2 * GDN_HEADS)))
    w_abt = w_in[:, 4 * d:].T
    wab_hi, wab_lo = _split_bf16(w_ab)
    wabt_hi, wabt_lo = _split_bf16(w_abt)
    head_sds = jax.ShapeDtypeStruct((b, n_h, s, LANES), BF16)
    head_spec = pl.BlockSpec((1, n_h, tm, LANES), lambda i, j: (i, 0, j, 0))
    n_ab = 2 * GDN_HEADS
    return pl.pallas_call(
        functools.partial(_gdn_proj_kernel, tm=tm),
        grid=(b, s // tm),
        in_specs=[
            pl.BlockSpec((1, tm, d), lambda i, j: (i, j, 0)),
            pl.BlockSpec((1, 1, d), lambda i, j: (i, 0, 0)),
            pl.BlockSpec((1, 1, d), lambda i, j: (i, 0, 0)),
            pl.BlockSpec((d, 4 * d), lambda i, j: (0, 0)),
            pl.BlockSpec((d, LANES), lambda i, j: (0, 0)),
            pl.BlockSpec((d, LANES), lambda i, j: (0, 0)),
            pl.BlockSpec((n_ab, d), lambda i, j: (0, 0)),
            pl.BlockSpec((n_ab, d), lambda i, j: (0, 0)),
        ],
        out_specs=[head_spec] * 4 + [
            pl.BlockSpec((1, tm, LANES), lambda i, j: (i, j, 0)),
            pl.BlockSpec((1, tm // GDN_CHUNK, n_ab, GDN_CHUNK), lambda i, j: (i, j, 0, 0)),
        ],
        out_shape=[head_sds] * 4 + [
            jax.ShapeDtypeStruct((b, s, LANES), F32),
            jax.ShapeDtypeStruct((b, s // GDN_CHUNK, n_ab, GDN_CHUNK), F32),
        ],
        compiler_params=_params("parallel", "parallel"),
        name="gdn_project",
    )(x, shift, scale, w_main, wab_hi, wab_lo, wabt_hi, wabt_lo)


def _softplus(t):
    return jnp.maximum(t, 0.0) + jnp.log(1.0 + jnp.exp(-jnp.abs(t)))


def _sigmoid(t):
    return 1.0 / (1.0 + jnp.exp(-t))


def _dot3_exact_lhs(lhs_bf16, rhs_f32):
    acc = None
    rem = rhs_f32
    for _ in range(3):
        part = rem.astype(BF16)
        rem = rem - part.astype(F32)
        t = jnp.dot(lhs_bf16, part, preferred_element_type=F32)
        acc = t if acc is None else acc + t
    return acc


def _dot3_exact_rhs(lhs_f32, rhs_bf16):
    acc = None
    rem = lhs_f32
    for _ in range(3):
        part = rem.astype(BF16)
        rem = rem - part.astype(F32)
        t = jnp.dot(part, rhs_bf16, preferred_element_type=F32)
        acc = t if acc is None else acc + t
    return acc


def _gdn_kernel(q_ref, k_ref, v_ref, z_ref, ab_ref, abt_ref, convw_ref, alog_l_ref, dtb_l_ref,
                alog_c_ref, dtb_c_ref, normw_ref, o_ref,
                state, carry, qs, ks, vs, gcol_s, beta_s, *, tc):
    n_h = GDN_HEADS
    ch = GDN_CHUNK
    dh = GDN_HEAD_DIM
    halo = 8

    @pl.when(pl.program_id(1) == 0)
    def _():
        state[...] = jnp.zeros_like(state)
        carry[:, :, 0:halo, :] = jnp.zeros((3, n_h, halo, dh), F32)

    for a, (src, dst) in enumerate(((q_ref, qs), (k_ref, ks), (v_ref, vs))):
        for hd in range(n_h):
            xb = carry.at[a, hd]
            xb[halo:halo + tc, :] = src[0, hd].astype(F32)
            w = convw_ref[a, hd]
            y = xb[halo:halo + tc, :] * w[CONV_WIDTH - 1:CONV_WIDTH]
            for j in range(CONV_WIDTH - 1):
                off = halo - (CONV_WIDTH - 1) + j
                y = y + xb[off:off + tc, :] * w[j:j + 1]
            xb[0:halo, :] = xb[tc:tc + halo, :]
            y = y * _sigmoid(y)
            if a < 2:
                y = y * lax.rsqrt(jnp.sum(y * y, axis=-1, keepdims=True) + RMS_EPS)
                if a == 0:
                    y = y * (dh ** -0.5)
            dst[hd] = y

    ab = ab_ref[0]
    gcol_s[...] = -jnp.exp(alog_l_ref[...]) * _softplus(ab + dtb_l_ref[...])
    beta_s[...] = _sigmoid(ab)

    ri = lax.broadcasted_iota(jnp.int32, (ch, ch), 0)
    ci = lax.broadcasted_iota(jnp.int32, (ch, ch), 1)
    causal = ri >= ci
    strict = ri > ci
    tril = jnp.where(causal, 1.0, 0.0).astype(BF16)
    triu = jnp.where(ri <= ci, 1.0, 0.0).astype(BF16)
    eye = jnp.where(ri == ci, 1.0, 0.0).astype(F32)
    nt = (((1,), (1,)), ((), ()))
    tn = (((0,), (0,)), ((), ()))

    def chunk(cidx, carry_unused):
        rows = pl.ds(pl.multiple_of(cidx * ch, ch), ch)
        gc = _dot3_exact_lhs(tril, gcol_s[rows, :])
        a_t = abt_ref[0, cidx][0:n_h, :]
        g_t = -jnp.exp(alog_c_ref[...]) * _softplus(a_t + dtb_c_ref[...])
        gc_t = _dot3_exact_rhs(g_t, triu)
        beta = beta_s[rows, :]
        heads = range(n_h)
        gcol = [gc[:, hd:hd + 1] for hd in heads]
        bcol = [beta[:, n_h + hd:n_h + hd + 1] for hd in heads]
        decay = [jnp.exp(jnp.where(causal, gcol[hd] - gc_t[hd:hd + 1, :], NEG_BIG))
                 for hd in heads]
        qc = [qs[hd, rows, :] for hd in heads]
        kc = [ks[hd, rows, :] for hd in heads]
        kb = [kc[hd] * bcol[hd] for hd in heads]
        both = [lax.dot_general(jnp.concatenate([kb[hd], qc[hd]], axis=0).astype(BF16),
                                kc[hd].astype(BF16), nt, preferred_element_type=F32)
                for hd in heads]
        lmat = [jnp.where(strict, both[hd][:ch] * decay[hd], 0.0) for hd in heads]
        qk = [(both[hd][ch:] * decay[hd]).astype(BF16) for hd in heads]
        pm = [-lmat[hd] for hd in heads]
        mm = lmat
        for _ in range(5):
            mb = [mm[hd].astype(BF16) for hd in heads]
            mm = [jnp.dot(mb[hd], mb[hd], preferred_element_type=F32) for hd in heads]
            pm = [pm[hd] + mm[hd] + jnp.dot(pm[hd].astype(BF16), mm[hd].astype(BF16),
                                            preferred_element_type=F32) for hd in heads]
        eg = [jnp.exp(gcol[hd]) for hd in heads]
        rhs = [jnp.concatenate([vs[hd, rows, :] * bcol[hd], kb[hd] * eg[hd]], axis=1)
               for hd in heads]
        sol = [rhs[hd] + jnp.dot(pm[hd].astype(BF16), rhs[hd].astype(BF16),
                                 preferred_element_type=F32) for hd in heads]
        glast = [gcol[hd][ch - 1:ch, :] for hd in heads]
        kdec = [(kc[hd] * jnp.exp(glast[hd] - gcol[hd])).astype(BF16) for hd in heads]
        s_prev = [state[hd] for hd in heads]
        wq = [jnp.dot(jnp.concatenate([sol[hd][:, dh:], qc[hd] * eg[hd]], axis=0).astype(BF16),
                      s_prev[hd].astype(BF16), preferred_element_type=F32)
              for hd in heads]
        vb = [(sol[hd][:, :dh] - wq[hd][:ch]).astype(BF16) for hd in heads]
        o = [wq[hd][ch:] + jnp.dot(qk[hd], vb[hd], preferred_element_type=F32) for hd in heads]
        for hd in heads:
            state[hd] = s_prev[hd] * jnp.exp(glast[hd]) + lax.dot_general(
                kdec[hd], vb[hd], tn, preferred_element_type=F32)
        for hd in heads:
            on = o[hd] * lax.rsqrt(jnp.mean(o[hd] * o[hd], axis=-1, keepdims=True)
                                   + RMS_EPS) * normw_ref[...]
            zc = z_ref[0, hd, rows, :].astype(F32)
            o_ref[0, hd, rows, :] = (on * (zc * _sigmoid(zc))).astype(o_ref.dtype)
        return carry_unused

    lax.fori_loop(0, tc // ch, chunk, 0)


def gdn_core(q, k, v, z, ab, abt, conv_w, a_log, dt_bias, norm_w, tc=512):
    b, n_h, s, dh = q.shape
    n_ab = 2 * n_h
    convw = conv_w.reshape(CONV_WIDTH, 3, n_h, dh).transpose(1, 2, 0, 3)
    pad = (0, LANES - n_h)
    alog_l = jnp.pad(a_log.astype(F32), pad).reshape(1, LANES)
    dtb_l = jnp.pad(dt_bias.astype(F32), pad).reshape(1, LANES)
    alog_c = a_log.astype(F32).reshape(n_h, 1)
    dtb_c = dt_bias.astype(F32).reshape(n_h, 1)
    head_spec = pl.BlockSpec((1, n_h, tc, dh), lambda i, j: (i, 0, j, 0))

    def full(shape):
        return pl.BlockSpec(shape, lambda i, j: (0,) * len(shape))

    return pl.pallas_call(
        functools.partial(_gdn_kernel, tc=tc),
        grid=(b, s // tc),
        in_specs=[head_spec] * 4 + [
            pl.BlockSpec((1, tc, LANES), lambda i, j: (i, j, 0)),
            pl.BlockSpec((1, tc // GDN_CHUNK, n_ab, GDN_CHUNK), lambda i, j: (i, j, 0, 0)),
            full((3, n_h, CONV_WIDTH, dh)),
            full((1, LANES)), full((1, LANES)), full((n_h, 1)), full((n_h, 1)), full((1, dh)),
        ],
        out_specs=head_spec,
        out_shape=jax.ShapeDtypeStruct((b, n_h, s, dh), BF16),
        scratch_shapes=[
            pltpu.VMEM((n_h, dh, dh), F32),
            pltpu.VMEM((3, n_h, 8 + tc, dh), F32),
            pltpu.VMEM((n_h, tc, dh), F32),
            pltpu.VMEM((n_h, tc, dh), F32),
            pltpu.VMEM((n_h, tc, dh), F32),
            pltpu.VMEM((tc, LANES), F32),
            pltpu.VMEM((tc, LANES), F32),
        ],
        compiler_params=_params("parallel", "arbitrary"),
        name="gdn_core",
    )(q, k, v, z, ab, abt, convw, alog_l, dtb_l, alog_c, dtb_c, norm_w.reshape(1, dh).astype(F32))


def gdn_layer(x, mods, w_in, conv_w, a_log, dt_bias, norm_w, w_out, ln_g, ln_b, alpha):
    b, s, d = x.shape
    shift = mods[:, None, 0:d]
    scale = mods[:, None, d:2 * d]
    gate1p = 1.0 + mods[:, None, 2 * d:3 * d]
    q, k, v, z, ab, abt = gdn_project(x, shift, scale, w_in)
    o = gdn_core(q, k, v, z, ab, abt, conv_w, a_log, dt_bias, norm_w)
    return out_proj_ln(o, w_out.astype(BF16), x, gate1p, ln_g, ln_b, alpha)


def _first_argmax(vals, idx, n):
    mx = jnp.max(vals, axis=0, keepdims=True)
    first = jnp.min(jnp.where(vals == mx, idx, n), axis=0, keepdims=True)
    return mx, first


def _router_kernel(x_ref, shift_ref, scale_ref, rwt_hi_ref, rwt_lo_ref, bias_ref,
                   eidx_ref, rank_ref, gcol_ref, cnt_ref, cnt_scr, *, tm):
    n_e = N_EXPERTS
    per_g = n_e // N_EXPERT_GROUPS

    @pl.when(pl.program_id(0) == 0)
    def _():
        cnt_scr[...] = jnp.zeros_like(cnt_scr)

    h = x_ref[...] * (1.0 + scale_ref[0]) + shift_ref[0]
    h_hi, h_lo = _split_bf16(h)
    nt = (((1,), (1,)), ((), ()))
    logits = lax.dot_general(rwt_hi_ref[...], h_hi, nt, preferred_element_type=F32)
    logits += lax.dot_general(rwt_hi_ref[...], h_lo, nt, preferred_element_type=F32)
    logits += lax.dot_general(rwt_lo_ref[...], h_hi, nt, preferred_element_type=F32)
    scores = _sigmoid(logits)
    biased = scores + bias_ref[...]

    r8 = lax.broadcasted_iota(jnp.int32, (per_g, tm), 0)
    gscores = []
    for g in range(N_EXPERT_GROUPS):
        blk = biased[g * per_g:(g + 1) * per_g]
        m1, i1 = _first_argmax(blk, r8, per_g)
        m2 = jnp.max(jnp.where(r8 == i1, NEG_BIG, blk), axis=0, keepdims=True)
        gscores.append(m1 + m2)
    vals = jnp.concatenate(gscores, axis=0)
    rg = lax.broadcasted_iota(jnp.int32, (N_EXPERT_GROUPS, tm), 0)
    gsel = jnp.zeros((N_EXPERT_GROUPS, tm), F32)
    for _ in range(TOPK_GROUPS):
        _, idx = _first_argmax(vals, rg, N_EXPERT_GROUPS)
        hit = rg == idx
        gsel = jnp.where(hit, 1.0, gsel)
        vals = jnp.where(hit, NEG_BIG, vals)
    emask = jnp.concatenate(
        [jnp.broadcast_to(gsel[g:g + 1], (per_g, tm)) for g in range(N_EXPERT_GROUPS)], axis=0)
    masked = jnp.where(emask > 0.5, biased, NEG_BIG)

    re = lax.broadcasted_iota(jnp.int32, (n_e, tm), 0)
    hits, eidx, gates = [], [], []
    for _ in range(TOP_K):
        _, idx = _first_argmax(masked, re, n_e)
        hit = re == idx
        gates.append(jnp.sum(jnp.where(hit, scores, 0.0), axis=0, keepdims=True))
        masked = jnp.where(hit, NEG_BIG, masked)
        hits.append(hit)
        eidx.append(idx)
    gates = jnp.concatenate(gates, axis=0)
    gates = gates / jnp.sum(gates, axis=0, keepdims=True) * ROUTED_SCALE
    onehot = jnp.zeros((n_e, tm), F32)
    for hit in hits:
        onehot = jnp.where(hit, 1.0, onehot)

    ti = lax.broadcasted_iota(jnp.int32, (tm, tm), 0)
    tj = lax.broadcasted_iota(jnp.int32, (tm, tm), 1)
    before = jnp.where(ti < tj, 1.0, 0.0).astype(BF16)
    prefix = jnp.dot(onehot.astype(BF16), before, preferred_element_type=F32) + cnt_scr[...]
    ranks = [jnp.sum(jnp.where(hit, prefix, 0.0), axis=0, keepdims=True) for hit in hits]
    cnt_scr[...] = cnt_scr[...] + jnp.sum(onehot, axis=1, keepdims=True)

    eidx_ref[...] = jnp.concatenate(eidx, axis=0)
    rank_ref[...] = jnp.concatenate(ranks, axis=0).astype(jnp.int32)
    cnt_ref[...] = jnp.broadcast_to(cnt_scr[...], cnt_ref.shape)
    gpad = jnp.concatenate([gates, jnp.zeros((LANES - TOP_K, tm), F32)], axis=0)
    gcol_ref[...] = gpad.T


def moe_route(xt, shift, scale, router_w, router_bias, s_len, tm=512):
    t, d = xt.shape
    n_e = router_w.shape[1]
    rwt_hi, rwt_lo = _split_bf16(router_w.T)
    per_b = s_len // tm
    return pl.pallas_call(
        functools.partial(_router_kernel, tm=tm),
        grid=(t // tm,),
        in_specs=[
            pl.BlockSpec((tm, d), lambda i: (i, 0)),
            pl.BlockSpec((1, 1, d), lambda i: (i // per_b, 0, 0)),
            pl.BlockSpec((1, 1, d), lambda i: (i // per_b, 0, 0)),
            pl.BlockSpec((n_e, d), lambda i: (0, 0)),
            pl.BlockSpec((n_e, d), lambda i: (0, 0)),
            pl.BlockSpec((n_e, 1), lambda i: (0, 0)),
        ],
        out_specs=[
            pl.BlockSpec((TOP_K, tm), lambda i: (0, i)),
            pl.BlockSpec((TOP_K, tm), lambda i: (0, i)),
            pl.BlockSpec((tm, LANES), lambda i: (i, 0)),
            pl.BlockSpec((n_e, LANES), lambda i: (0, 0)),
        ],
        out_shape=[
            jax.ShapeDtypeStruct((TOP_K, t), jnp.int32),
            jax.ShapeDtypeStruct((TOP_K, t), jnp.int32),
            jax.ShapeDtypeStruct((t, LANES), F32),
            jax.ShapeDtypeStruct((n_e, LANES), F32),
        ],
        scratch_shapes=[pltpu.VMEM((n_e, 1), F32)],
        compiler_params=_params("arbitrary"),
        name="moe_route",
    )(xt, shift, scale, rwt_hi, rwt_lo, router_bias.reshape(n_e, 1).astype(F32))


def _slots_kernel(cnt_ref, eidx_ref, rank_ref, dest_ref, blk_e_ref, nused_ref, *, block_rows):
    n_e = N_EXPERTS
    cnt = cnt_ref[...].astype(jnp.int32)
    padded = ((cnt + (block_rows - 1)) // block_rows) * block_rows
    ri = lax.broadcasted_iota(jnp.int32, (n_e, n_e), 0)
    ci = lax.broadcasted_iota(jnp.int32, (n_e, n_e), 1)
    tril = jnp.where(ri >= ci, 1.0, 0.0).astype(BF16)
    pend = _dot3_exact_lhs(tril, padded.astype(F32))
    pstart = pend.astype(jnp.int32) - padded
    eidx = eidx_ref[...]
    dest = rank_ref[...]
    for e in range(n_e):
        dest = dest + jnp.where(eidx == e, pstart[e:e + 1, 0:1], 0)
    dest_ref[...] = dest
    nb = blk_e_ref.shape[1]
    first_row = lax.broadcasted_iota(jnp.int32, (n_e, nb), 1) * block_rows
    pend_i = pend.astype(jnp.int32)[:, 0:1]
    owner = jnp.sum(jnp.where(pend_i <= first_row, 1, 0), axis=0, keepdims=True)
    blk_e_ref[...] = jnp.minimum(owner, n_e - 1)
    nused_ref[...] = pend_i[n_e - 1:n_e, :] // block_rows + jnp.zeros(nused_ref.shape, jnp.int32)


def moe_slots(counts, eidx, rank, n_blocks, block_rows, tt=2048):
    t = eidx.shape[1]
    tt = min(tt, t)
    nb_pad = -(-n_blocks // LANES) * LANES
    n_e = counts.shape[0]
    return pl.pallas_call(
        functools.partial(_slots_kernel, block_rows=block_rows),
        grid=(t // tt,),
        in_specs=[
            pl.BlockSpec((n_e, LANES), lambda i: (0, 0)),
            pl.BlockSpec((TOP_K, tt), lambda i: (0, i)),
            pl.BlockSpec((TOP_K, tt), lambda i: (0, i)),
        ],
        out_specs=[
            pl.BlockSpec((TOP_K, tt), lambda i: (0, i)),
            pl.BlockSpec((1, nb_pad), lambda i: (0, 0)),
            pl.BlockSpec((1, LANES), lambda i: (0, 0)),
        ],
        out_shape=[
            jax.ShapeDtypeStruct((TOP_K, t), jnp.int32),
            jax.ShapeDtypeStruct((1, nb_pad), jnp.int32),
            jax.ShapeDtypeStruct((1, LANES), jnp.int32),
        ],
        compiler_params=_params("arbitrary"),
        name="moe_slots",
    )(counts, eidx, rank)


def _row_copy(src, src_row, dst, dst_row, sem):
    return pltpu.make_async_copy(src.at[pl.ds(src_row, 1)], dst.at[pl.ds(dst_row, 1)], sem)


def _dispatch_kernel(dest_ref, x_ref, shift_ref, scale_ref, xs_hbm, hs, sem, *, tm, t_total):
    base = pl.program_id(0) * tm
    hs[...] = x_ref[...] * (1.0 + scale_ref[0]) + shift_ref[0]

    def issue(t, carry):
        for k in range(TOP_K):
            _row_copy(hs, t, xs_hbm, dest_ref[k * t_total + base + t], sem).start()
        return carry

    lax.fori_loop(0, tm, issue, 0, unroll=2)
    for k in range(TOP_K):
        pltpu.make_async_copy(hs, xs_hbm.at[pl.ds(0, tm)], sem).wait()


def moe_dispatch(dest_flat, xt, shift, scale, n_rows, s_len, tm=256):
    t, d = xt.shape
    per_b = s_len // tm
    grid_spec = pltpu.PrefetchScalarGridSpec(
        num_scalar_prefetch=1,
        grid=(t // tm,),
        in_specs=[
            pl.BlockSpec((tm, d), lambda i, dest: (i, 0)),
            pl.BlockSpec((1, 1, d), lambda i, dest: (i // per_b, 0, 0)),
            pl.BlockSpec((1, 1, d), lambda i, dest: (i // per_b, 0, 0)),
        ],
        out_specs=pl.BlockSpec(memory_space=pl.ANY),
        scratch_shapes=[pltpu.VMEM((tm, d), F32), pltpu.SemaphoreType.DMA],
    )
    return pl.pallas_call(
        functools.partial(_dispatch_kernel, tm=tm, t_total=t),
        grid_spec=grid_spec,
        out_shape=jax.ShapeDtypeStruct((n_rows, d), F32),
        compiler_params=_params("arbitrary"),
        name="moe_dispatch",
    )(dest_flat, xt, shift, scale)


def _expert_kernel(blk_e_ref, nused_ref, xs_ref, w1_ref, w3_ref, w2_ref, ys_ref,
                   w1b, w3b, w2b):
    i = pl.program_id(0)
    prev = blk_e_ref[jnp.maximum(i - 1, 0)]

    @pl.when((i == 0) | (blk_e_ref[i] != prev))
    def _():
        w1b[...] = w1_ref[0].astype(BF16)
        w3b[...] = w3_ref[0].astype(BF16)
        w2b[...] = w2_ref[0].astype(BF16)

    @pl.when(i < nused_ref[0])
    def _():
        xb = xs_ref[...].astype(BF16)
        h1 = jnp.dot(xb, w1b[...], preferred_element_type=F32)
        h3 = jnp.dot(xb, w3b[...], preferred_element_type=F32)
        act = (h1 * _sigmoid(h1)) * h3
        ys_ref[...] = jnp.dot(act.astype(BF16), w2b[...], preferred_element_type=F32)


def moe_experts(blk_e, nused, xs, w1, w3, w2, block_rows, first_expert):
    n_rows, d = xs.shape
    n_blocks = n_rows // block_rows
    de = w1.shape[-1]

    def row_map(i, be, nu):
        return (jnp.minimum(i, nu[0] - 1), 0)

    def w_map(i, be, nu):
        return (be[i] + first_expert, 0, 0)

    grid_spec = pltpu.PrefetchScalarGridSpec(
        num_scalar_prefetch=2,
        grid=(n_blocks,),
        in_specs=[
            pl.BlockSpec((block_rows, d), row_map),
            pl.BlockSpec((1, d, de), w_map),
            pl.BlockSpec((1, d, de), w_map),
            pl.BlockSpec((1, de, d), w_map),
        ],
        out_specs=pl.BlockSpec((block_rows, d), row_map),
        scratch_shapes=[pltpu.VMEM((d, de), BF16), pltpu.VMEM((d, de), BF16),
                        pltpu.VMEM((de, d), BF16)],
    )
    return pl.pallas_call(
        _expert_kernel,
        grid_spec=grid_spec,
        out_shape=jax.ShapeDtypeStruct((n_rows, d), F32),
        compiler_params=_params("arbitrary"),
        name="moe_experts",
    )(blk_e, nused, xs, w1, w3, w2)


def _combine_kernel(dest_ref, ys_hbm, x_ref, shift_ref, scale_ref, gate_ref, gcol_ref,
                    ws1_ref, ws3_ref, ws2_ref, g_ref, b_ref, o_ref, buf, sem,
                    *, tm, t_total, alpha):
    base = pl.program_id(0) * tm

    def issue(t, carry):
        for k in range(TOP_K):
            _row_copy(ys_hbm, dest_ref[k * t_total + base + t], buf.at[k], t, sem).start()
        return carry

    lax.fori_loop(0, tm, issue, 0, unroll=2)

    x = x_ref[...]
    hb = (x * (1.0 + scale_ref[0]) + shift_ref[0]).astype(BF16)
    h1 = jnp.dot(hb, ws1_ref[...], preferred_element_type=F32)
    h3 = jnp.dot(hb, ws3_ref[...], preferred_element_type=F32)
    act = (h1 * _sigmoid(h1)) * h3
    y = jnp.dot(act.astype(BF16), ws2_ref[...], preferred_element_type=F32)

    for k in range(TOP_K):
        pltpu.make_async_copy(ys_hbm.at[pl.ds(0, tm)], buf.at[k], sem).wait()

    gcol = gcol_ref[...]
    for k in range(TOP_K):
        y = y + buf[k] * gcol[:, k:k + 1]
    z = alpha * x + gate_ref[0] * y
    o_ref[...] = _layer_norm(z, g_ref[...], b_ref[...])


def moe_combine(dest_flat, ys, xt, shift, scale, gate1p, gcol, ws1, ws3, ws2, ln_g, ln_b,
                s_len, alpha, tm=128):
    t, d = xt.shape
    ds_ = ws1.shape[1]
    per_b = s_len // tm

    def const(shape):
        return pl.BlockSpec(shape, lambda i, dest: (0,) * len(shape))

    mod_spec = pl.BlockSpec((1, 1, d), lambda i, dest: (i // per_b, 0, 0))
    grid_spec = pltpu.PrefetchScalarGridSpec(
        num_scalar_prefetch=1,
        grid=(t // tm,),
        in_specs=[
            pl.BlockSpec(memory_space=pl.ANY),
            pl.BlockSpec((tm, d), lambda i, dest: (i, 0)),
            mod_spec, mod_spec, mod_spec,
            pl.BlockSpec((tm, LANES), lambda i, dest: (i, 0)),
            const((d, ds_)), const((d, ds_)), const((ds_, d)),
            const((1, d)), const((1, d)),
        ],
        out_specs=pl.BlockSpec((tm, d), lambda i, dest: (i, 0)),
        scratch_shapes=[pltpu.VMEM((TOP_K, tm, d), F32), pltpu.SemaphoreType.DMA],
    )
    return pl.pallas_call(
        functools.partial(_combine_kernel, tm=tm, t_total=t, alpha=alpha),
        grid_spec=grid_spec,
        out_shape=jax.ShapeDtypeStruct((t, d), F32),
        compiler_params=_params("arbitrary"),
        name="moe_combine",
    )(dest_flat, ys, xt, shift, scale, gate1p, gcol, ws1, ws3, ws2,
      ln_g.reshape(1, d), ln_b.reshape(1, d))


def moe_layer(x, mods, router_w, router_bias, w1, w3, w2, ws1, ws3, ws2, ln_g, ln_b, alpha,
              layer):
    b, s, d = x.shape
    t = b * s
    shift = mods[:, None, 0:d]
    scale = mods[:, None, d:2 * d]
    gate1p = 1.0 + mods[:, None, 2 * d:3 * d]
    xt = x.reshape(t, d)
    n_blocks = t * TOP_K // MOE_BLOCK + N_EXPERTS
    eidx, rank, gcol, counts = moe_route(xt, shift, scale, router_w, router_bias, s)
    dest, blk_e, nused = moe_slots(counts, eidx, rank, n_blocks, MOE_BLOCK)
    dest_flat = dest.reshape(-1)
    xs = moe_dispatch(dest_flat, xt, shift, scale, n_blocks * MOE_BLOCK, s)
    n_e = w1.shape[1]
    ys = moe_experts(blk_e.reshape(-1), nused.reshape(-1), xs,
                     w1.reshape((-1,) + w1.shape[2:]), w3.reshape((-1,) + w3.shape[2:]),
                     w2.reshape((-1,) + w2.shape[2:]), MOE_BLOCK, layer * n_e)
    out = moe_combine(dest_flat, ys, xt, shift, scale, gate1p, gcol,
                      ws1.astype(BF16), ws3.astype(BF16), ws2.astype(BF16), ln_g, ln_b, s, alpha)
    return out.reshape(b, s, d)


def kernel(x, c, positions, ada_w, ada_b, ln_g, ln_b, attn_w_in, attn_w_out, gdn_w_in, gdn_conv_w,
           gdn_a_log, gdn_dt_bias, gdn_norm_w, gdn_w_out, router_w, router_bias, expert_w1,
           expert_w3, expert_w2, shared_w1, shared_w3, shared_w2):
    depth = ada_w.shape[0]
    alpha = (2 * depth) ** 0.25
    mods = ada_vectors(c, ada_w, ada_b)
    cos_t, sin_t = rope_tables(positions)
    for i in range(depth):
        j = i // 2
        if i % 2 == 0:
            x = attention_layer(x, mods[2 * i], cos_t, sin_t, attn_w_in[j], attn_w_out[j],
                                ln_g[i, 0], ln_b[i, 0], alpha)
        else:
            x = gdn_layer(x, mods[2 * i], gdn_w_in[j], gdn_conv_w[j], gdn_a_log[j],
                          gdn_dt_bias[j], gdn_norm_w[j], gdn_w_out[j], ln_g[i, 0], ln_b[i, 0],
                          alpha)
        x = moe_layer(x, mods[2 * i + 1], router_w[i], router_bias[i], expert_w1,
                      expert_w3, expert_w2, shared_w1[i], shared_w3[i], shared_w2[i],
                      ln_g[i, 1], ln_b[i, 1], alpha, i)
    return x
```

```python
import functools
import math

import jax
import jax.numpy as jnp
from jax import lax
from jax.experimental import pallas as pl
from jax.experimental.pallas import tpu as pltpu

F32 = jnp.float32
BF16 = jnp.bfloat16

LANES = 128

DILATED_GROUPS = ((128, 1), (512, 4), (2048, 16))
A_HEADS = 16
A_HEAD_DIM = 64
ATTN_BLOCK = 128
ROPE_THETA = 10000.0
GDN_HEADS = 8
GDN_HEAD_DIM = 128
CONV_WIDTH = 4
GDN_CHUNK = 64
N_EXPERTS = 64
TOP_K = 8
N_EXPERT_GROUPS = 8
TOPK_GROUPS = 4
ROUTED_SCALE = 2.5
MOE_BLOCK = 256
LN_EPS = 1e-5
RMS_EPS = 1e-6
NEG_BIG = -1e30

VMEM_LIMIT = 56 * 1024 * 1024


def _params(*sem):
    return pltpu.CompilerParams(dimension_semantics=sem, vmem_limit_bytes=VMEM_LIMIT)


def _ada_kernel(c_ref, w_ref, b_ref, o_ref):
    c = c_ref[...]
    c_hi = c.astype(BF16)
    c_lo = (c - c_hi.astype(F32)).astype(BF16)
    w = w_ref[0]
    w_hi = w.astype(BF16)
    w_lo = (w - w_hi.astype(F32)).astype(BF16)
    acc = jnp.dot(c_hi, w_hi, preferred_element_type=F32)
    acc += jnp.dot(c_hi, w_lo, preferred_element_type=F32)
    acc += jnp.dot(c_lo, w_hi, preferred_element_type=F32)
    o_ref[0] = acc + b_ref[0]


def ada_vectors(c, ada_w, ada_b):
    depth2 = ada_w.shape[0] * ada_w.shape[1]
    b, d = c.shape
    d3 = ada_w.shape[-1]
    tn = d
    w = ada_w.reshape(depth2, d, d3)
    bias = ada_b.reshape(depth2, 1, d3)
    return pl.pallas_call(
        _ada_kernel,
        grid=(depth2, d3 // tn),
        in_specs=[
            pl.BlockSpec((b, d), lambda l, j: (0, 0)),
            pl.BlockSpec((1, d, tn), lambda l, j: (l, 0, j)),
            pl.BlockSpec((1, 1, tn), lambda l, j: (l, 0, j)),
        ],
        out_specs=pl.BlockSpec((1, b, tn), lambda l, j: (l, 0, j)),
        out_shape=jax.ShapeDtypeStruct((depth2, b, d3), F32),
        compiler_params=_params("parallel", "parallel"),
        name="ada_vectors",
    )(c, w, bias)


def _rope_kernel(pos_ref, freq_ref, sign_ref, cos_ref, sin_ref):
    ang = pos_ref[0].astype(F32) * freq_ref[...]
    cos_ref[0] = jnp.cos(ang)
    sin_ref[0] = jnp.sin(ang) * sign_ref[...]


def rope_tables(positions):
    b, s = positions.shape
    half = A_HEAD_DIM // 2
    inv_freq = ROPE_THETA ** (-jnp.arange(half, dtype=F32) / half)
    freq = jnp.tile(inv_freq, LANES // half).reshape(1, LANES)
    sign = jnp.tile(jnp.concatenate([-jnp.ones((half,), F32), jnp.ones((half,), F32)]),
                    LANES // A_HEAD_DIM).reshape(1, LANES)
    ts = min(s, 1024)
    return pl.pallas_call(
        _rope_kernel,
        grid=(b, s // ts),
        in_specs=[
            pl.BlockSpec((1, ts, 1), lambda i, j: (i, j, 0)),
            pl.BlockSpec((1, LANES), lambda i, j: (0, 0)),
            pl.BlockSpec((1, LANES), lambda i, j: (0, 0)),
        ],
        out_specs=[pl.BlockSpec((1, ts, LANES), lambda i, j: (i, j, 0))] * 2,
        out_shape=[jax.ShapeDtypeStruct((b, s, LANES), F32)] * 2,
        compiler_params=_params("parallel", "parallel"),
        name="rope_tables",
    )(positions.reshape(b, s, 1), freq, sign)


def _residue_major(ref, lead, dil, tl):
    if dil == 1:
        return ref[lead] if lead is not None else ref[...]
    parts = []
    for r in range(dil):
        if lead is None:
            parts.append(ref[pl.ds(r, tl, stride=dil), :])
        else:
            parts.append(ref[lead, pl.ds(r, tl, stride=dil), :])
    return jnp.concatenate(parts, axis=0)


def _attn_proj_kernel(x_ref, shift_ref, scale_ref, cos_ref, sin_ref, w_ref,
                      q_ref, k_ref, v_ref, h_scr, *, dil, tm):
    tl = tm // dil
    d = x_ref.shape[-1]
    h = x_ref[0] * (1.0 + scale_ref[0]) + shift_ref[0]
    if dil == 1:
        hb = h.astype(BF16)
    else:
        for cidx in range(d // LANES):
            h_scr[cidx] = h[:, cidx * LANES:(cidx + 1) * LANES]
        hb = jnp.concatenate(
            [_residue_major(h_scr, cidx, dil, tl).astype(BF16) for cidx in range(d // LANES)],
            axis=1)
    cosp = _residue_major(cos_ref, 0, dil, tl)
    sinp = _residue_major(sin_ref, 0, dil, tl)
    lane = lax.broadcasted_iota(jnp.int32, (tm, LANES), 1)
    first_half = (lane % A_HEAD_DIM) < (A_HEAD_DIM // 2)
    qscale = A_HEAD_DIM ** -0.5
    for j, out_ref in enumerate((q_ref, k_ref, v_ref)):
        res = jnp.dot(hb, w_ref[:, j * d:(j + 1) * d], preferred_element_type=F32)
        for hp in range(d // LANES):
            blk = res[:, hp * LANES:(hp + 1) * LANES]
            if j < 2:
                swapped = jnp.where(first_half,
                                    pltpu.roll(blk, LANES - A_HEAD_DIM // 2, 1),
                                    pltpu.roll(blk, A_HEAD_DIM // 2, 1))
                blk = blk * cosp + swapped * sinp
                if j == 0:
                    blk = blk * qscale
            blk = blk.astype(BF16)
            for r in range(dil):
                out_ref[0, hp, r] = blk[r * tl:(r + 1) * tl]


def attn_project(x, shift, scale, cos_t, sin_t, w_g, dil, tm=512):
    b, s, d = x.shape
    tl = tm // dil
    n_hp = d // LANES
    out_sds = jax.ShapeDtypeStruct((b, n_hp, dil, s // dil, LANES), BF16)
    out_spec = pl.BlockSpec((1, n_hp, dil, tl, LANES), lambda i, j: (i, 0, 0, j, 0))
    return pl.pallas_call(
        functools.partial(_attn_proj_kernel, dil=dil, tm=tm),
        grid=(b, s // tm),
        in_specs=[
            pl.BlockSpec((1, tm, d), lambda i, j: (i, j, 0)),
            pl.BlockSpec((1, 1, d), lambda i, j: (i, 0, 0)),
            pl.BlockSpec((1, 1, d), lambda i, j: (i, 0, 0)),
            pl.BlockSpec((1, tm, LANES), lambda i, j: (i, j, 0)),
            pl.BlockSpec((1, tm, LANES), lambda i, j: (i, j, 0)),
            pl.BlockSpec((d, 3 * d), lambda i, j: (0, 0)),
        ],
        out_specs=[out_spec] * 3,
        out_shape=[out_sds] * 3,
        scratch_shapes=[pltpu.VMEM((d // LANES, tm, LANES), F32)],
        compiler_params=_params("parallel", "parallel"),
        name=f"attn_project_d{dil}",
    )(x, shift, scale, cos_t, sin_t, w_g)


def _attn_kernel(*refs, s_len, dils):
    n_g = len(dils)
    qkv = refs[:3 * n_g]
    o_ref = refs[3 * n_g]
    oacc, lacc = refs[3 * n_g + 1:]
    blk = ATTN_BLOCK

    lane = lax.broadcasted_iota(jnp.int32, (blk, LANES), 1)
    head0 = lane < A_HEAD_DIM
    row = lax.broadcasted_iota(jnp.int32, (2 * blk, 2 * blk), 0) % blk
    col = lax.broadcasted_iota(jnp.int32, (2 * blk, 2 * blk), 1)
    bias_rest = jnp.where((col >= row) & (col <= row + blk), 0.0, NEG_BIG).astype(F32)
    bias_first = bias_rest[:, blk:]

    def run_blocks(g, q_ref, k_ref, v_ref, dil, r, specs):
        ids = range(len(specs))
        qq, kc, vc = [], [], []
        for n, first in specs:
            q = q_ref[0, 0, r, pl.ds(pl.multiple_of(n * blk, blk), blk), :]
            zero = jnp.zeros_like(q)
            qq.append(jnp.concatenate([jnp.where(head0, q, zero), jnp.where(head0, zero, q)],
                                      axis=0))
            if first:
                keys = pl.ds(0, blk)
            else:
                keys = pl.ds(pl.multiple_of((n - 1) * blk, blk), 2 * blk)
            kc.append(k_ref[0, 0, r, keys, :])
            vc.append(v_ref[0, 0, r, keys, :])
        sc = [lax.dot_general(qq[i], kc[i], (((1,), (1,)), ((), ())), preferred_element_type=F32)
              + (bias_first if specs[i][1] else bias_rest) for i in ids]
        m = [jnp.max(sc[i], axis=1, keepdims=True) for i in ids]
        p = [jnp.exp(sc[i] - m[i]) for i in ids]
        l = [jnp.sum(p[i], axis=1, keepdims=True) for i in ids]
        pv = [jnp.dot(p[i].astype(BF16), vc[i], preferred_element_type=F32) * (1.0 / l[i])
              for i in ids]
        for i in ids:
            n = specs[i][0]
            lse = m[i] + jnp.log(l[i])
            o_blk = jnp.where(head0, pv[i][:blk], pv[i][blk:])
            lse_blk = jnp.where(head0, jnp.broadcast_to(lse[:blk], (blk, LANES)),
                                jnp.broadcast_to(lse[blk:], (blk, LANES)))
            start = n * (blk * dil) + r
            if dil == 1:
                idx = pl.ds(pl.multiple_of(start, blk), blk)
            else:
                idx = pl.ds(start, blk, stride=dil)
            oacc[g, idx, :] = o_blk
            lacc[g, idx, :] = lse_blk

    for g, dil in enumerate(dils):
        q_ref, k_ref, v_ref = qkv[3 * g:3 * g + 3]
        nb = s_len // dil // blk

        def per_residue(r, carry, g=g, dil=dil, nb=nb, q_ref=q_ref, k_ref=k_ref, v_ref=v_ref):
            if nb == 1:
                run_blocks(g, q_ref, k_ref, v_ref, dil, r, [(0, True)])
                return carry
            run_blocks(g, q_ref, k_ref, v_ref, dil, r, [(0, True), (1, False)])

            def per_pair(i, c2):
                run_blocks(g, q_ref, k_ref, v_ref, dil, r, [(2 * i, False), (2 * i + 1, False)])
                return c2

            lax.fori_loop(1, nb // 2, per_pair, 0)
            return carry

        lax.fori_loop(0, dil, per_residue, 0)

    tc = 256

    def mix(i, carry):
        rows = pl.ds(pl.multiple_of(i * tc, tc), tc)
        ls = [lacc[g, rows, :] for g in range(n_g)]
        mx = functools.reduce(jnp.maximum, ls)
        es = [jnp.exp(l - mx) for l in ls]
        den = functools.reduce(lambda a, b2: a + b2, es)
        num = es[0] * oacc[0, rows, :]
        for g in range(1, n_g):
            num = num + es[g] * oacc[g, rows, :]
        o_ref[0, 0, rows, :] = (num * (1.0 / den)).astype(o_ref.dtype)
        return carry

    lax.fori_loop(0, s_len // tc, mix, 0)


def attn_core(qkvs, dils, s_len):
    b, n_hp = qkvs[0].shape[:2]
    in_specs = []
    for g, dil in enumerate(dils):
        spec = pl.BlockSpec((1, 1, dil, s_len // dil, LANES), lambda i, j: (i, j, 0, 0, 0))
        in_specs += [spec] * 3
    n_g = len(dils)
    return pl.pallas_call(
        functools.partial(_attn_kernel, s_len=s_len, dils=tuple(dils)),
        grid=(b, n_hp),
        in_specs=in_specs,
        out_specs=pl.BlockSpec((1, 1, s_len, LANES), lambda i, j: (i, j, 0, 0)),
        out_shape=jax.ShapeDtypeStruct((b, n_hp, s_len, LANES), BF16),
        scratch_shapes=[pltpu.VMEM((n_g, s_len, LANES), F32),
                        pltpu.VMEM((n_g, s_len, LANES), F32)],
        compiler_params=_params("parallel", "parallel"),
        name="attn_core",
    )(*qkvs)


def _layer_norm(z, g, b):
    mu = jnp.mean(z, axis=-1, keepdims=True)
    zc = z - mu
    var = jnp.mean(zc * zc, axis=-1, keepdims=True)
    return zc * lax.rsqrt(var + LN_EPS) * g + b


def _out_ln_kernel(o_ref, w_ref, x_ref, gate_ref, g_ref, b_ref, y_ref, *, alpha):
    n_hp = o_ref.shape[1]
    o = jnp.concatenate([o_ref[0, hp] for hp in range(n_hp)], axis=-1)
    y = jnp.dot(o, w_ref[...], preferred_element_type=F32)
    z = alpha * x_ref[0] + gate_ref[0] * y
    y_ref[0] = _layer_norm(z, g_ref[...], b_ref[...])


def out_proj_ln(o_heads, w_out, x, gate1p, ln_g, ln_b, alpha, tm=512):
    b, s, d = x.shape
    n_hp = o_heads.shape[1]
    return pl.pallas_call(
        functools.partial(_out_ln_kernel, alpha=alpha),
        grid=(b, s // tm),
        in_specs=[
            pl.BlockSpec((1, n_hp, tm, LANES), lambda i, j: (i, 0, j, 0)),
            pl.BlockSpec((d, d), lambda i, j: (0, 0)),
            pl.BlockSpec((1, tm, d), lambda i, j: (i, j, 0)),
            pl.BlockSpec((1, 1, d), lambda i, j: (i, 0, 0)),
            pl.BlockSpec((1, d), lambda i, j: (0, 0)),
            pl.BlockSpec((1, d), lambda i, j: (0, 0)),
        ],
        out_specs=pl.BlockSpec((1, tm, d), lambda i, j: (i, j, 0)),
        out_shape=jax.ShapeDtypeStruct((b, s, d), F32),
        compiler_params=_params("parallel", "parallel"),
        name="out_proj_ln",
    )(o_heads, w_out, x, gate1p, ln_g.reshape(1, d), ln_b.reshape(1, d))


def attention_layer(x, mods, cos_t, sin_t, w_in, w_out, ln_g, ln_b, alpha):
    b, s, d = x.shape
    shift = mods[:, None, 0:d]
    scale = mods[:, None, d:2 * d]
    gate1p = 1.0 + mods[:, None, 2 * d:3 * d]
    w_in = w_in.astype(BF16)
    dils = [dil for _, dil in DILATED_GROUPS]
    qkvs = []
    for g, dil in enumerate(dils):
        qkvs += attn_project(x, shift, scale, cos_t, sin_t,
                             w_in[:, g * 3 * d:(g + 1) * 3 * d], dil)
    o = attn_core(qkvs, dils, s)
    return out_proj_ln(o, w_out.astype(BF16), x, gate1p, ln_g, ln_b, alpha)


def _split_bf16(a):
    hi = a.astype(BF16)
    lo = (a - hi.astype(F32)).astype(BF16)
    return hi, lo


def _gdn_proj_kernel(x_ref, shift_ref, scale_ref, w_ref, wab_hi_ref, wab_lo_ref,
                     wabt_hi_ref, wabt_lo_ref, q_ref, k_ref, v_ref, z_ref, ab_ref, abt_ref, *, tm):
    d = x_ref.shape[-1]
    h = x_ref[0] * (1.0 + scale_ref[0]) + shift_ref[0]
    h_hi, h_lo = _split_bf16(h)
    for j, out_ref in enumerate((q_ref, k_ref, v_ref, z_ref)):
        res = jnp.dot(h_hi, w_ref[:, j * d:(j + 1) * d], preferred_element_type=F32)
        for hd in range(d // LANES):
            out_ref[0, hd] = res[:, hd * LANES:(hd + 1) * LANES].astype(BF16)
    ab = jnp.dot(h_hi, wab_hi_ref[...], preferred_element_type=F32)
    ab += jnp.dot(h_lo, wab_hi_ref[...], preferred_element_type=F32)
    ab += jnp.dot(h_hi, wab_lo_ref[...], preferred_element_type=F32)
    ab_ref[0] = ab
    nt = (((1,), (1,)), ((), ()))
    abt = lax.dot_general(wabt_hi_ref[...], h_hi, nt, preferred_element_type=F32)
    abt += lax.dot_general(wabt_hi_ref[...], h_lo, nt, preferred_element_type=F32)
    abt += lax.dot_general(wabt_lo_ref[...], h_hi, nt, preferred_element_type=F32)
    for n in range(tm // GDN_CHUNK):
        abt_ref[0, n] = abt[:, n * GDN_CHUNK:(n + 1) * GDN_CHUNK]


def gdn_project(x, shift, scale, w_in, tm=512):
    b, s, d = x.shape
    n_h = d // LANES
    w_main = w_in[:, :4 * d].astype(BF16)
    w_ab = jnp.pad(w_in[:, 4 * d:], ((0, 0), (0, LANES - 2 * GDN_HEADS)))
    w_abt = w_in[:, 4 * d:].T
    wab_hi, wab_lo = _split_bf16(w_ab)
    wabt_hi, wabt_lo = _split_bf16(w_abt)
    head_sds = jax.ShapeDtypeStruct((b, n_h, s, LANES), BF16)
    head_spec = pl.BlockSpec((1, n_h, tm, LANES), lambda i, j: (i, 0, j, 0))
    n_ab = 2 * GDN_HEADS
    return pl.pallas_call(
        functools.partial(_gdn_proj_kernel, tm=tm),
        grid=(b, s // tm),
        in_specs=[
            pl.BlockSpec((1, tm, d), lambda i, j: (i, j, 0)),
            pl.BlockSpec((1, 1, d), lambda i, j: (i, 0, 0)),
            pl.BlockSpec((1, 1, d), lambda i, j: (i, 0, 0)),
            pl.BlockSpec((d, 4 * d), lambda i, j: (0, 0)),
            pl.BlockSpec((d, LANES), lambda i, j: (0, 0)),
            pl.BlockSpec((d, LANES), lambda i, j: (0, 0)),
            pl.BlockSpec((n_ab, d), lambda i, j: (0, 0)),
            pl.BlockSpec((n_ab, d), lambda i, j: (0, 0)),
        ],
        out_specs=[head_spec] * 4 + [
            pl.BlockSpec((1, tm, LANES), lambda i, j: (i, j, 0)),
            pl.BlockSpec((1, tm // GDN_CHUNK, n_ab, GDN_CHUNK), lambda i, j: (i, j, 0, 0)),
        ],
        out_shape=[head_sds] * 4 + [
            jax.ShapeDtypeStruct((b, s, LANES), F32),
            jax.ShapeDtypeStruct((b, s // GDN_CHUNK, n_ab, GDN_CHUNK), F32),
        ],
        compiler_params=_params("parallel", "parallel"),
        name="gdn_project",
    )(x, shift, scale, w_main, wab_hi, wab_lo, wabt_hi, wabt_lo)


def _softplus(t):
    return jnp.maximum(t, 0.0) + jnp.log(1.0 + jnp.exp(-jnp.abs(t)))


def _sigmoid(t):
    return 1.0 / (1.0 + jnp.exp(-t))


def _dot3_exact_lhs(lhs_bf16, rhs_f32):
    acc = None
    rem = rhs_f32
    for _ in range(3):
        part = rem.astype(BF16)
        rem = rem - part.astype(F32)
        t = jnp.dot(lhs_bf16, part, preferred_element_type=F32)
        acc = t if acc is None else acc + t
    return acc


def _dot3_exact_rhs(lhs_f32, rhs_bf16):
    acc = None
    rem = lhs_f32
    for _ in range(3):
        part = rem.astype(BF16)
        rem = rem - part.astype(F32)
        t = jnp.dot(part, rhs_bf16, preferred_element_type=F32)
        acc = t if acc is None else acc + t
    return acc


def _gdn_kernel(q_ref, k_ref, v_ref, z_ref, ab_ref, abt_ref, convw_ref, alog_l_ref, dtb_l_ref,
                alog_c_ref, dtb_c_ref, normw_ref, o_ref,
                state, carry, qs, ks, vs, gcol_s, beta_s, *, tc):
    n_h = GDN_HEADS
    ch = GDN_CHUNK
    dh = GDN_HEAD_DIM
    halo = 8

    @pl.when(pl.program_id(1) == 0)
    def _():
        state[...] = jnp.zeros_like(state)
        carry[:, :, 0:halo, :] = jnp.zeros((3, n_h, halo, dh), F32)

    for a, (src, dst) in enumerate(((q_ref, qs), (k_ref, ks), (v_ref, vs))):
        for hd in range(n_h):
            xb = carry.at[a, hd]
            xb[halo:halo + tc, :] = src[0, hd].astype(F32)
            w = convw_ref[a, hd]
            y = xb[halo:halo + tc, :] * w[CONV_WIDTH - 1:CONV_WIDTH]
            for j in range(CONV_WIDTH - 1):
                off = halo - (CONV_WIDTH - 1) + j
                y = y + xb[off:off + tc, :] * w[j:j + 1]
            xb[0:halo, :] = xb[tc:tc + halo, :]
            y = y * _sigmoid(y)
            if a < 2:
                y = y * lax.rsqrt(jnp.sum(y * y, axis=-1, keepdims=True) + RMS_EPS)
                if a == 0:
                    y = y * (dh ** -0.5)
            dst[hd] = y

    ab = ab_ref[0]
    gcol_s[...] = -jnp.exp(alog_l_ref[...]) * _softplus(ab + dtb_l_ref[...])
    beta_s[...] = _sigmoid(ab)

    ri = lax.broadcasted_iota(jnp.int32, (ch, ch), 0)
    ci = lax.broadcasted_iota(jnp.int32, (ch, ch), 1)
    causal = ri >= ci
    strict = ri > ci
    tril = jnp.where(causal, 1.0, 0.0).astype(BF16)
    triu = jnp.where(ri <= ci, 1.0, 0.0).astype(BF16)
    eye = jnp.where(ri == ci, 1.0, 0.0).astype(F32)
    nt = (((1,), (1,)), ((), ()))
    tn = (((0,), (0,)), ((), ()))

    def chunk(cidx, carry_unused):
        rows = pl.ds(pl.multiple_of(cidx * ch, ch), ch)
        gc = _dot3_exact_lhs(tril, gcol_s[rows, :])
        a_t = abt_ref[0, cidx][0:n_h, :]
        g_t = -jnp.exp(alog_c_ref[...]) * _softplus(a_t + dtb_c_ref[...])
        gc_t = _dot3_exact_rhs(g_t, triu)
        beta = beta_s[rows, :]
        heads = range(n_h)
        gcol = [gc[:, hd:hd + 1] for hd in heads]
        bcol = [beta[:, n_h + hd:n_h + hd + 1] for hd in heads]
        decay = [jnp.exp(jnp.where(causal, gcol[hd] - gc_t[hd:hd + 1, :], NEG_BIG))
                 for hd in heads]
        qc = [qs[hd, rows, :] for hd in heads]
        kc = [ks[hd, rows, :] for hd in heads]
        kb = [kc[hd] * bcol[hd] for hd in heads]
        both = [lax.dot_general(jnp.concatenate([kb[hd], qc[hd]], axis=0).astype(BF16),
                                kc[hd].astype(BF16), nt, preferred_element_type=F32)
                for hd in heads]
        lmat = [jnp.where(strict, both[hd][:ch] * decay[hd], 0.0) for hd in heads]
        qk = [(both[hd][ch:] * decay[hd]).astype(BF16) for hd in heads]
        pm = [-lmat[hd] for hd in heads]
        mm = lmat
        for _ in range(5):
            mb = [mm[hd].astype(BF16) for hd in heads]
            mm = [jnp.dot(mb[hd], mb[hd], preferred_element_type=F32) for hd in heads]
            pm = [pm[hd] + mm[hd] + jnp.dot(pm[hd].astype(BF16), mm[hd].astype(BF16),
                                            preferred_element_type=F32) for hd in heads]
        eg = [jnp.exp(gcol[hd]) for hd in heads]
        rhs = [jnp.concatenate([vs[hd, rows, :] * bcol[hd], kb[hd] * eg[hd]], axis=1)
               for hd in heads]
        sol = [rhs[hd] + jnp.dot(pm[hd].astype(BF16), rhs[hd].astype(BF16),
                                 preferred_element_type=F32) for hd in heads]
        glast = [gcol[hd][ch - 1:ch, :] for hd in heads]
        kdec = [(kc[hd] * jnp.exp(glast[hd] - gcol[hd])).astype(BF16) for hd in heads]
        s_prev = [state[hd] for hd in heads]
        wq = [jnp.dot(jnp.concatenate([sol[hd][:, dh:], qc[hd] * eg[hd]], axis=0).astype(BF16),
                      s_prev[hd].astype(BF16), preferred_element_type=F32)
              for hd in heads]
        vb = [(sol[hd][:, :dh] - wq[hd][:ch]).astype(BF16) for hd in heads]
        o = [wq[hd][ch:] + jnp.dot(qk[hd], vb[hd], preferred_element_type=F32) for hd in heads]
        for hd in heads:
            state[hd] = s_prev[hd] * jnp.exp(glast[hd]) + lax.dot_general(
                kdec[hd], vb[hd], tn, preferred_element_type=F32)
        for hd in heads:
            on = o[hd] * lax.rsqrt(jnp.mean(o[hd] * o[hd], axis=-1, keepdims=True)
                                   + RMS_EPS) * normw_ref[...]
            zc = z_ref[0, hd, rows, :].astype(F32)
            o_ref[0, hd, rows, :] = (on * (zc * _sigmoid(zc))).astype(o_ref.dtype)
        return carry_unused

    lax.fori_loop(0, tc // ch, chunk, 0)


def gdn_core(q, k, v, z, ab, abt, conv_w, a_log, dt_bias, norm_w, tc=512):
    b, n_h, s, dh = q.shape
    n_ab = 2 * n_h
    convw = conv_w.reshape(CONV_WIDTH, 3, n_h, dh).transpose(1, 2, 0, 3)
    pad = (0, LANES - n_h)
    alog_l = jnp.pad(a_log.astype(F32), pad).reshape(1, LANES)
    dtb_l = jnp.pad(dt_bias.astype(F32), pad).reshape(1, LANES)
    alog_c = a_log.astype(F32).reshape(n_h, 1)
    dtb_c = dt_bias.astype(F32).reshape(n_h, 1)
    head_spec = pl.BlockSpec((1, n_h, tc, dh), lambda i, j: (i, 0, j, 0))

    def full(shape):
        return pl.BlockSpec(shape, lambda i, j: (0,) * len(shape))

    return pl.pallas_call(
        functools.partial(_gdn_kernel, tc=tc),
        grid=(b, s // tc),
        in_specs=[head_spec] * 4 + [
            pl.BlockSpec((1, tc, LANES), lambda i, j: (i, j, 0)),
            pl.BlockSpec((1, tc // GDN_CHUNK, n_ab, GDN_CHUNK), lambda i, j: (i, j, 0, 0)),
            full((3, n_h, CONV_WIDTH, dh)),
            full((1, LANES)), full((1, LANES)), full((n_h, 1)), full((n_h, 1)), full((1, dh)),
        ],
        out_specs=head_spec,
        out_shape=jax.ShapeDtypeStruct((b, n_h, s, dh), BF16),
        scratch_shapes=[
            pltpu.VMEM((n_h, dh, dh), F32),
            pltpu.VMEM((3, n_h, 8 + tc, dh), F32),
            pltpu.VMEM((n_h, tc, dh), F32),
            pltpu.VMEM((n_h, tc, dh), F32),
            pltpu.VMEM((n_h, tc, dh), F32),
            pltpu.VMEM((tc, LANES), F32),
            pltpu.VMEM((tc, LANES), F32),
        ],
        compiler_params=_params("parallel", "arbitrary"),
        name="gdn_core",
    )(q, k, v, z, ab, abt, convw, alog_l, dtb_l, alog_c, dtb_c, norm_w.reshape(1, dh).astype(F32))


def gdn_layer(x, mods, w_in, conv_w, a_log, dt_bias, norm_w, w_out, ln_g, ln_b, alpha):
    b, s, d = x.shape
    shift = mods[:, None, 0:d]
    scale = mods[:, None, d:2 * d]
    gate1p = 1.0 + mods[:, None, 2 * d:3 * d]
    q, k, v, z, ab, abt = gdn_project(x, shift, scale, w_in)
    o = gdn_core(q, k, v, z, ab, abt, conv_w, a_log, dt_bias, norm_w)
    return out_proj_ln(o, w_out.astype(BF16), x, gate1p, ln_g, ln_b, alpha)


def _first_argmax(vals, idx, n):
    mx = jnp.max(vals, axis=0, keepdims=True)
    first = jnp.min(jnp.where(vals == mx, idx, n), axis=0, keepdims=True)
    return mx, first


def _router_kernel(x_ref, shift_ref, scale_ref, rwt_hi_ref, rwt_lo_ref, bias_ref,
                   eidx_ref, rank_ref, gcol_ref, cnt_ref, cnt_scr, *, tm):
    n_e = N_EXPERTS
    per_g = n_e // N_EXPERT_GROUPS

    @pl.when(pl.program_id(0) == 0)
    def _():
        cnt_scr[...] = jnp.zeros_like(cnt_scr)

    h = x_ref[...] * (1.0 + scale_ref[0]) + shift_ref[0]
    h_hi, h_lo = _split_bf16(h)
    nt = (((1,), (1,)), ((), ()))
    logits = lax.dot_general(rwt_hi_ref[...], h_hi, nt, preferred_element_type=F32)
    logits += lax.dot_general(rwt_hi_ref[...], h_lo, nt, preferred_element_type=F32)
    logits += lax.dot_general(rwt_lo_ref[...], h_hi, nt, preferred_element_type=F32)
    scores = _sigmoid(logits)
    biased = scores + bias_ref[...]

    r8 = lax.broadcasted_iota(jnp.int32, (per_g, tm), 0)
    gscores = []
    for g in range(N_EXPERT_GROUPS):
        blk = biased[g * per_g:(g + 1) * per_g]
        m1, i1 = _first_argmax(blk, r8, per_g)
        m2 = jnp.max(jnp.where(r8 == i1, NEG_BIG, blk), axis=0, keepdims=True)
        gscores.append(m1 + m2)
    vals = jnp.concatenate(gscores, axis=0)
    rg = lax.broadcasted_iota(jnp.int32, (N_EXPERT_GROUPS, tm), 0)
    gsel = jnp.zeros((N_EXPERT_GROUPS, tm), F32)
    for _ in range(TOPK_GROUPS):
        _, idx = _first_argmax(vals, rg, N_EXPERT_GROUPS)
        hit = rg == idx
        gsel = jnp.where(hit, 1.0, gsel)
        vals = jnp.where(hit, NEG_BIG, vals)
    emask = jnp.concatenate(
        [jnp.broadcast_to(gsel[g:g + 1], (per_g, tm)) for g in range(N_EXPERT_GROUPS)], axis=0)
    masked = jnp.where(emask > 0.5, biased, NEG_BIG)

    re = lax.broadcasted_iota(jnp.int32, (n_e, tm), 0)
    hits, eidx, gates = [], [], []
    for _ in range(TOP_K):
        _, idx = _first_argmax(masked, re, n_e)
        hit = re == idx
        gates.append(jnp.sum(jnp.where(hit, scores, 0.0), axis=0, keepdims=True))
        masked = jnp.where(hit, NEG_BIG, masked)
        hits.append(hit)
        eidx.append(idx)
    gates = jnp.concatenate(gates, axis=0)
    gates = gates / jnp.sum(gates, axis=0, keepdims=True) * ROUTED_SCALE
    onehot = jnp.zeros((n_e, tm), F32)
    for hit in hits:
        onehot = jnp.where(hit, 1.0, onehot)

    ti = lax.broadcasted_iota(jnp.int32, (tm, tm), 0)
    tj = lax.broadcasted_iota(jnp.int32, (tm, tm), 1)
    before = jnp.where(ti < tj, 1.0, 0.0).astype(BF16)
    prefix = jnp.dot(onehot.astype(BF16), before, preferred_element_type=F32) + cnt_scr[...]
    ranks = [jnp.sum(jnp.where(hit, prefix, 0.0), axis=0, keepdims=True) for hit in hits]
    cnt_scr[...] = cnt_scr[...] + jnp.sum(onehot, axis=1, keepdims=True)

    eidx_ref[...] = jnp.concatenate(eidx, axis=0)
    rank_ref[...] = jnp.concatenate(ranks, axis=0).astype(jnp.int32)
    cnt_ref[...] = jnp.broadcast_to(cnt_scr[...], cnt_ref.shape)
    gpad = jnp.concatenate([gates, jnp.zeros((LANES - TOP_K, tm), F32)], axis=0)
    gcol_ref[...] = gpad.T


def moe_route(xt, shift, scale, router_w, router_bias, s_len, tm=512):
    t, d = xt.shape
    n_e = router_w.shape[1]
    rwt_hi, rwt_lo = _split_bf16(router_w.T)
    per_b = s_len // tm
    return pl.pallas_call(
        functools.partial(_router_kernel, tm=tm),
        grid=(t // tm,),
        in_specs=[
            pl.BlockSpec((tm, d), lambda i: (i, 0)),
            pl.BlockSpec((1, 1, d), lambda i: (i // per_b, 0, 0)),
            pl.BlockSpec((1, 1, d), lambda i: (i // per_b, 0, 0)),
            pl.BlockSpec((n_e, d), lambda i: (0, 0)),
            pl.BlockSpec((n_e, d), lambda i: (0, 0)),
            pl.BlockSpec((n_e, 1), lambda i: (0, 0)),
        ],
        out_specs=[
            pl.BlockSpec((TOP_K, tm), lambda i: (0, i)),
            pl.BlockSpec((TOP_K, tm), lambda i: (0, i)),
            pl.BlockSpec((tm, LANES), lambda i: (i, 0)),
            pl.BlockSpec((n_e, LANES), lambda i: (0, 0)),
        ],
        out_shape=[
            jax.ShapeDtypeStruct((TOP_K, t), jnp.int32),
            jax.ShapeDtypeStruct((TOP_K, t), jnp.int32),
            jax.ShapeDtypeStruct((t, LANES), F32),
            jax.ShapeDtypeStruct((n_e, LANES), F32),
        ],
        scratch_shapes=[pltpu.VMEM((n_e, 1), F32)],
        compiler_params=_params("arbitrary"),
        name="moe_route",
    )(xt, shift, scale, rwt_hi, rwt_lo, router_bias.reshape(n_e, 1).astype(F32))


def _slots_kernel(cnt_ref, eidx_ref, rank_ref, dest_ref, blk_e_ref, nused_ref, *, block_rows):
    n_e = N_EXPERTS
    cnt = cnt_ref[...].astype(jnp.int32)
    padded = ((cnt + (block_rows - 1)) // block_rows) * block_rows
    ri = lax.broadcasted_iota(jnp.int32, (n_e, n_e), 0)
    ci = lax.broadcasted_iota(jnp.int32, (n_e, n_e), 1)
    tril = jnp.where(ri >= ci, 1.0, 0.0).astype(BF16)
    pend = _dot3_exact_lhs(tril, padded.astype(F32))
    pstart = pend.astype(jnp.int32) - padded
    eidx = eidx_ref[...]
    dest = rank_ref[...]
    for e in range(n_e):
        dest = dest + jnp.where(eidx == e, pstart[e:e + 1, 0:1], 0)
    dest_ref[...] = dest
    nb = blk_e_ref.shape[1]
    first_row = lax.broadcasted_iota(jnp.int32, (n_e, nb), 1) * block_rows
    pend_i = pend.astype(jnp.int32)[:, 0:1]
    owner = jnp.sum(jnp.where(pend_i <= first_row, 1, 0), axis=0, keepdims=True)
    blk_e_ref[...] = jnp.minimum(owner, n_e - 1)
    nused_ref[...] = pend_i[n_e - 1:n_e, :] // block_rows + jnp.zeros(nused_ref.shape, jnp.int32)


def moe_slots(counts, eidx, rank, n_blocks, block_rows, tt=2048):
    t = eidx.shape[1]
    tt = min(tt, t)
    nb_pad = -(-n_blocks // LANES) * LANES
    n_e = counts.shape[0]
    return pl.pallas_call(
        functools.partial(_slots_kernel, block_rows=block_rows),
        grid=(t // tt,),
        in_specs=[
            pl.BlockSpec((n_e, LANES), lambda i: (0, 0)),
            pl.BlockSpec((TOP_K, tt), lambda i: (0, i)),
            pl.BlockSpec((TOP_K, tt), lambda i: (0, i)),
        ],
        out_specs=[
            pl.BlockSpec((TOP_K, tt), lambda i: (0, i)),
            pl.BlockSpec((1, nb_pad), lambda i: (0, 0)),
            pl.BlockSpec((1, LANES), lambda i: (0, 0)),
        ],
        out_shape=[
            jax.ShapeDtypeStruct((TOP_K, t), jnp.int32),
            jax.ShapeDtypeStruct((1, nb_pad), jnp.int32),
            jax.ShapeDtypeStruct((1, LANES), jnp.int32),
        ],
        compiler_params=_params("arbitrary"),
        name="moe_slots",
    )(counts, eidx, rank)


TILE_ROWS = 8


def _tile_copy(src, src_tok, dst, dst_tok, sem):
    return pltpu.make_async_copy(
        src.at[pl.ds(pl.multiple_of(src_tok * TILE_ROWS, TILE_ROWS), TILE_ROWS)],
        dst.at[pl.ds(pl.multiple_of(dst_tok * TILE_ROWS, TILE_ROWS), TILE_ROWS)], sem)


def _to_token_tiles(ref, val, n_tok):
    for cidx in range(TILE_ROWS):
        ref[pl.ds(cidx, n_tok, stride=TILE_ROWS), :] = val[:, cidx * LANES:(cidx + 1) * LANES]


def _from_token_tiles(ref, n_tok):
    return jnp.concatenate(
        [ref[pl.ds(cidx, n_tok, stride=TILE_ROWS), :] for cidx in range(TILE_ROWS)], axis=1)


def _dispatch_kernel(dest_ref, x_ref, shift_ref, scale_ref, xs_hbm, hs, sem, *, tm, t_total):
    base = pl.program_id(0) * tm
    _to_token_tiles(hs, x_ref[...] * (1.0 + scale_ref[0]) + shift_ref[0], tm)

    def issue(t, carry):
        for k in range(TOP_K):
            _tile_copy(hs, t, xs_hbm, dest_ref[k * t_total + base + t], sem).start()
        return carry

    lax.fori_loop(0, tm, issue, 0, unroll=2)
    for k in range(TOP_K):
        pltpu.make_async_copy(hs, xs_hbm.at[pl.ds(0, tm * TILE_ROWS)], sem).wait()


def moe_dispatch(dest_flat, xt, shift, scale, n_rows, s_len, tm=256):
    t, d = xt.shape
    per_b = s_len // tm
    grid_spec = pltpu.PrefetchScalarGridSpec(
        num_scalar_prefetch=1,
        grid=(t // tm,),
        in_specs=[
            pl.BlockSpec((tm, d), lambda i, dest: (i, 0)),
            pl.BlockSpec((1, 1, d), lambda i, dest: (i // per_b, 0, 0)),
            pl.BlockSpec((1, 1, d), lambda i, dest: (i // per_b, 0, 0)),
        ],
        out_specs=pl.BlockSpec(memory_space=pl.ANY),
        scratch_shapes=[pltpu.VMEM((tm * TILE_ROWS, LANES), F32), pltpu.SemaphoreType.DMA],
    )
    assert d == TILE_ROWS * LANES
    return pl.pallas_call(
        functools.partial(_dispatch_kernel, tm=tm, t_total=t),
        grid_spec=grid_spec,
        out_shape=jax.ShapeDtypeStruct((n_rows * TILE_ROWS, LANES), F32),
        compiler_params=_params("arbitrary"),
        name="moe_dispatch",
    )(dest_flat, xt, shift, scale)


def _expert_kernel(blk_e_ref, nused_ref, xs_ref, w1_ref, w3_ref, w2_ref, ys_ref,
                   w1b, w3b, w2b, *, block_rows):
    i = pl.program_id(0)
    prev = blk_e_ref[jnp.maximum(i - 1, 0)]

    @pl.when((i == 0) | (blk_e_ref[i] != prev))
    def _():
        w1b[...] = w1_ref[0].astype(BF16)
        w3b[...] = w3_ref[0].astype(BF16)
        w2b[...] = w2_ref[0].astype(BF16)

    @pl.when(i < nused_ref[0])
    def _():
        xb = _from_token_tiles(xs_ref, block_rows).astype(BF16)
        h1 = jnp.dot(xb, w1b[...], preferred_element_type=F32)
        h3 = jnp.dot(xb, w3b[...], preferred_element_type=F32)
        act = (h1 * _sigmoid(h1)) * h3
        y = jnp.dot(act.astype(BF16), w2b[...], preferred_element_type=F32)
        _to_token_tiles(ys_ref, y, block_rows)


def moe_experts(blk_e, nused, xs, w1, w3, w2, block_rows, first_expert):
    d, de = w1.shape[-2:]
    n_rows = xs.shape[0] // TILE_ROWS
    n_blocks = n_rows // block_rows

    def row_map(i, be, nu):
        return (jnp.minimum(i, nu[0] - 1), 0)

    def w_map(i, be, nu):
        return (be[i] + first_expert, 0, 0)

    grid_spec = pltpu.PrefetchScalarGridSpec(
        num_scalar_prefetch=2,
        grid=(n_blocks,),
        in_specs=[
            pl.BlockSpec((block_rows * TILE_ROWS, LANES), row_map),
            pl.BlockSpec((1, d, de), w_map),
            pl.BlockSpec((1, d, de), w_map),
            pl.BlockSpec((1, de, d), w_map),
        ],
        out_specs=pl.BlockSpec((block_rows * TILE_ROWS, LANES), row_map),
        scratch_shapes=[pltpu.VMEM((d, de), BF16), pltpu.VMEM((d, de), BF16),
                        pltpu.VMEM((de, d), BF16)],
    )
    return pl.pallas_call(
        functools.partial(_expert_kernel, block_rows=block_rows),
        grid_spec=grid_spec,
        out_shape=jax.ShapeDtypeStruct((n_rows * TILE_ROWS, LANES), F32),
        compiler_params=_params("arbitrary"),
        name="moe_experts",
    )(blk_e, nused, xs, w1, w3, w2)


def _combine_kernel(dest_ref, ys_hbm, x_ref, shift_ref, scale_ref, gate_ref, gcol_ref,
                    ws1_ref, ws3_ref, ws2_ref, g_ref, b_ref, o_ref, buf, sem,
                    *, tm, t_total, alpha):
    base = pl.program_id(0) * tm

    def issue(t, carry):
        for k in range(TOP_K):
            _tile_copy(ys_hbm, dest_ref[k * t_total + base + t], buf.at[k], t, sem).start()
        return carry

    lax.fori_loop(0, tm, issue, 0, unroll=2)

    x = x_ref[...]
    hb = (x * (1.0 + scale_ref[0]) + shift_ref[0]).astype(BF16)
    h1 = jnp.dot(hb, ws1_ref[...], preferred_element_type=F32)
    h3 = jnp.dot(hb, ws3_ref[...], preferred_element_type=F32)
    act = (h1 * _sigmoid(h1)) * h3
    y = jnp.dot(act.astype(BF16), ws2_ref[...], preferred_element_type=F32)

    for k in range(TOP_K):
        pltpu.make_async_copy(ys_hbm.at[pl.ds(0, tm * TILE_ROWS)], buf.at[k], sem).wait()

    gcol = gcol_ref[...]
    for k in range(TOP_K):
        y = y + _from_token_tiles(buf.at[k], tm) * gcol[:, k:k + 1]
    z = alpha * x + gate_ref[0] * y
    o_ref[...] = _layer_norm(z, g_ref[...], b_ref[...])


def moe_combine(dest_flat, ys, xt, shift, scale, gate1p, gcol, ws1, ws3, ws2, ln_g, ln_b,
                s_len, alpha, tm=128):
    t, d = xt.shape
    ds_ = ws1.shape[1]
    per_b = s_len // tm

    def const(shape):
        return pl.BlockSpec(shape, lambda i, dest: (0,) * len(shape))

    mod_spec = pl.BlockSpec((1, 1, d), lambda i, dest: (i // per_b, 0, 0))
    grid_spec = pltpu.PrefetchScalarGridSpec(
        num_scalar_prefetch=1,
        grid=(t // tm,),
        in_specs=[
            pl.BlockSpec(memory_space=pl.ANY),
            pl.BlockSpec((tm, d), lambda i, dest: (i, 0)),
            mod_spec, mod_spec, mod_spec,
            pl.BlockSpec((tm, LANES), lambda i, dest: (i, 0)),
            const((d, ds_)), const((d, ds_)), const((ds_, d)),
            const((1, d)), const((1, d)),
        ],
        out_specs=pl.BlockSpec((tm, d), lambda i, dest: (i, 0)),
        scratch_shapes=[pltpu.VMEM((TOP_K, tm * TILE_ROWS, LANES), F32),
                        pltpu.SemaphoreType.DMA],
    )
    return pl.pallas_call(
        functools.partial(_combine_kernel, tm=tm, t_total=t, alpha=alpha),
        grid_spec=grid_spec,
        out_shape=jax.ShapeDtypeStruct((t, d), F32),
        compiler_params=_params("arbitrary"),
        name="moe_combine",
    )(dest_flat, ys, xt, shift, scale, gate1p, gcol, ws1, ws3, ws2,
      ln_g.reshape(1, d), ln_b.reshape(1, d))


def moe_layer(x, mods, router_w, router_bias, w1, w3, w2, ws1, ws3, ws2, ln_g, ln_b, alpha,
              layer):
    b, s, d = x.shape
    t = b * s
    shift = mods[:, None, 0:d]
    scale = mods[:, None, d:2 * d]
    gate1p = 1.0 + mods[:, None, 2 * d:3 * d]
    xt = x.reshape(t, d)
    n_blocks = t * TOP_K // MOE_BLOCK + N_EXPERTS
    eidx, rank, gcol, counts = moe_route(xt, shift, scale, router_w, router_bias, s)
    dest, blk_e, nused = moe_slots(counts, eidx, rank, n_blocks, MOE_BLOCK)
    dest_flat = dest.reshape(-1)
    xs = moe_dispatch(dest_flat, xt, shift, scale, n_blocks * MOE_BLOCK, s)
    n_e = w1.shape[1]
    ys = moe_experts(blk_e.reshape(-1), nused.reshape(-1), xs,
                     w1.reshape((-1,) + w1.shape[2:]), w3.reshape((-1,) + w3.shape[2:]),
                     w2.reshape((-1,) + w2.shape[2:]), MOE_BLOCK, layer * n_e)
    out = moe_combine(dest_flat, ys, xt, shift, scale, gate1p, gcol,
                      ws1.astype(BF16), ws3.astype(BF16), ws2.astype(BF16), ln_g, ln_b, s, alpha)
    return out.reshape(b, s, d)


def kernel(x, c, positions, ada_w, ada_b, ln_g, ln_b, attn_w_in, attn_w_out, gdn_w_in, gdn_conv_w,
           gdn_a_log, gdn_dt_bias, gdn_norm_w, gdn_w_out, router_w, router_bias, expert_w1,
           expert_w3, expert_w2, shared_w1, shared_w3, shared_w2):
    depth = ada_w.shape[0]
    alpha = (2 * depth) ** 0.25
    mods = ada_vectors(c, ada_w, ada_b)
    cos_t, sin_t = rope_tables(positions)
    for i in range(depth):
        j = i // 2
        if i % 2 == 0:
            x = attention_layer(x, mods[2 * i], cos_t, sin_t, attn_w_in[j], attn_w_out[j],
                                ln_g[i, 0], ln_b[i, 0], alpha)
        else:
            x = gdn_layer(x, mods[2 * i], gdn_w_in[j], gdn_conv_w[j], gdn_a_log[j],
                          gdn_dt_bias[j], gdn_norm_w[j], gdn_w_out[j], ln_g[i, 0], ln_b[i, 0],
                          alpha)
        x = moe_layer(x, mods[2 * i + 1], router_w[i], router_bias[i], expert_w1,
                      expert_w3, expert_w2, shared_w1[i], shared_w3[i], shared_w2[i],
                      ln_g[i, 1], ln_b[i, 1], alpha, i)
    return x
```

```python
import functools
import math

import jax
import jax.numpy as jnp
from jax import lax
from jax.experimental import pallas as pl
from jax.experimental.pallas import tpu as pltpu

F32 = jnp.float32
BF16 = jnp.bfloat16

LANES = 128

DILATED_GROUPS = ((128, 1), (512, 4), (2048, 16))
A_HEADS = 16
A_HEAD_DIM = 64
ATTN_BLOCK = 128
ROPE_THETA = 10000.0
GDN_HEADS = 8
GDN_HEAD_DIM = 128
CONV_WIDTH = 4
GDN_CHUNK = 64
N_EXPERTS = 64
TOP_K = 8
N_EXPERT_GROUPS = 8
TOPK_GROUPS = 4
ROUTED_SCALE = 2.5
MOE_BLOCK = 512
LN_EPS = 1e-5
RMS_EPS = 1e-6
NEG_BIG = -1e30

VMEM_LIMIT = 56 * 1024 * 1024


def _params(*sem):
    return pltpu.CompilerParams(dimension_semantics=sem, vmem_limit_bytes=VMEM_LIMIT)


def _ada_kernel(c_ref, w_ref, b_ref, o_ref):
    c = c_ref[...]
    c_hi = c.astype(BF16)
    c_lo = (c - c_hi.astype(F32)).astype(BF16)
    w = w_ref[0]
    w_hi = w.astype(BF16)
    w_lo = (w - w_hi.astype(F32)).astype(BF16)
    acc = jnp.dot(c_hi, w_hi, preferred_element_type=F32)
    acc += jnp.dot(c_hi, w_lo, preferred_element_type=F32)
    acc += jnp.dot(c_lo, w_hi, preferred_element_type=F32)
    o_ref[0] = acc + b_ref[0]


def ada_vectors(c, ada_w, ada_b):
    depth2 = ada_w.shape[0] * ada_w.shape[1]
    b, d = c.shape
    d3 = ada_w.shape[-1]
    tn = d
    w = ada_w.reshape(depth2, d, d3)
    bias = ada_b.reshape(depth2, 1, d3)
    return pl.pallas_call(
        _ada_kernel,
        grid=(depth2, d3 // tn),
        in_specs=[
            pl.BlockSpec((b, d), lambda l, j: (0, 0)),
            pl.BlockSpec((1, d, tn), lambda l, j: (l, 0, j)),
            pl.BlockSpec((1, 1, tn), lambda l, j: (l, 0, j)),
        ],
        out_specs=pl.BlockSpec((1, b, tn), lambda l, j: (l, 0, j)),
        out_shape=jax.ShapeDtypeStruct((depth2, b, d3), F32),
        compiler_params=_params("parallel", "parallel"),
        name="ada_vectors",
    )(c, w, bias)


def _rope_kernel(pos_ref, freq_ref, sign_ref, cos_ref, sin_ref):
    ang = pos_ref[0].astype(F32) * freq_ref[...]
    cos_ref[0] = jnp.cos(ang)
    sin_ref[0] = jnp.sin(ang) * sign_ref[...]


def rope_tables(positions):
    b, s = positions.shape
    half = A_HEAD_DIM // 2
    inv_freq = ROPE_THETA ** (-jnp.arange(half, dtype=F32) / half)
    freq = jnp.tile(inv_freq, LANES // half).reshape(1, LANES)
    sign = jnp.tile(jnp.concatenate([-jnp.ones((half,), F32), jnp.ones((half,), F32)]),
                    LANES // A_HEAD_DIM).reshape(1, LANES)
    ts = min(s, 1024)
    return pl.pallas_call(
        _rope_kernel,
        grid=(b, s // ts),
        in_specs=[
            pl.BlockSpec((1, ts, 1), lambda i, j: (i, j, 0)),
            pl.BlockSpec((1, LANES), lambda i, j: (0, 0)),
            pl.BlockSpec((1, LANES), lambda i, j: (0, 0)),
        ],
        out_specs=[pl.BlockSpec((1, ts, LANES), lambda i, j: (i, j, 0))] * 2,
        out_shape=[jax.ShapeDtypeStruct((b, s, LANES), F32)] * 2,
        compiler_params=_params("parallel", "parallel"),
        name="rope_tables",
    )(positions.reshape(b, s, 1), freq, sign)


def _residue_major(ref, lead, dil, tl):
    if dil == 1:
        return ref[lead] if lead is not None else ref[...]
    parts = []
    for r in range(dil):
        if lead is None:
            parts.append(ref[pl.ds(r, tl, stride=dil), :])
        else:
            parts.append(ref[lead, pl.ds(r, tl, stride=dil), :])
    return jnp.concatenate(parts, axis=0)


def _attn_proj_kernel(x_ref, shift_ref, scale_ref, cos_ref, sin_ref, w_ref,
                      q_ref, k_ref, v_ref, h_scr, *, dil, tm):
    tl = tm // dil
    d = x_ref.shape[-1]
    h = x_ref[0] * (1.0 + scale_ref[0]) + shift_ref[0]
    if dil == 1:
        hb = h.astype(BF16)
    else:
        for cidx in range(d // LANES):
            h_scr[cidx] = h[:, cidx * LANES:(cidx + 1) * LANES]
        hb = jnp.concatenate(
            [_residue_major(h_scr, cidx, dil, tl).astype(BF16) for cidx in range(d // LANES)],
            axis=1)
    cosp = _residue_major(cos_ref, 0, dil, tl)
    sinp = _residue_major(sin_ref, 0, dil, tl)
    lane = lax.broadcasted_iota(jnp.int32, (tm, LANES), 1)
    first_half = (lane % A_HEAD_DIM) < (A_HEAD_DIM // 2)
    qscale = A_HEAD_DIM ** -0.5
    for j, out_ref in enumerate((q_ref, k_ref, v_ref)):
        res = jnp.dot(hb, w_ref[:, j * d:(j + 1) * d], preferred_element_type=F32)
        for hp in range(d // LANES):
            blk = res[:, hp * LANES:(hp + 1) * LANES]
            if j < 2:
                swapped = jnp.where(first_half,
                                    pltpu.roll(blk, LANES - A_HEAD_DIM // 2, 1),
                                    pltpu.roll(blk, A_HEAD_DIM // 2, 1))
                blk = blk * cosp + swapped * sinp
                if j == 0:
                    blk = blk * qscale
            blk = blk.astype(BF16)
            for r in range(dil):
                out_ref[0, hp, r] = blk[r * tl:(r + 1) * tl]


def attn_project(x, shift, scale, cos_t, sin_t, w_g, dil, tm=512):
    b, s, d = x.shape
    tl = tm // dil
    n_hp = d // LANES
    out_sds = jax.ShapeDtypeStruct((b, n_hp, dil, s // dil, LANES), BF16)
    out_spec = pl.BlockSpec((1, n_hp, dil, tl, LANES), lambda i, j: (i, 0, 0, j, 0))
    return pl.pallas_call(
        functools.partial(_attn_proj_kernel, dil=dil, tm=tm),
        grid=(b, s // tm),
        in_specs=[
            pl.BlockSpec((1, tm, d), lambda i, j: (i, j, 0)),
            pl.BlockSpec((1, 1, d), lambda i, j: (i, 0, 0)),
            pl.BlockSpec((1, 1, d), lambda i, j: (i, 0, 0)),
            pl.BlockSpec((1, tm, LANES), lambda i, j: (i, j, 0)),
            pl.BlockSpec((1, tm, LANES), lambda i, j: (i, j, 0)),
            pl.BlockSpec((d, 3 * d), lambda i, j: (0, 0)),
        ],
        out_specs=[out_spec] * 3,
        out_shape=[out_sds] * 3,
        scratch_shapes=[pltpu.VMEM((d // LANES, tm, LANES), F32)],
        compiler_params=_params("parallel", "parallel"),
        name=f"attn_project_d{dil}",
    )(x, shift, scale, cos_t, sin_t, w_g)


def _attn_kernel(*refs, s_len, dils):
    n_g = len(dils)
    qkv = refs[:3 * n_g]
    o_ref = refs[3 * n_g]
    oacc, lacc = refs[3 * n_g + 1:]
    blk = ATTN_BLOCK

    lane = lax.broadcasted_iota(jnp.int32, (blk, LANES), 1)
    head0 = lane < A_HEAD_DIM
    row = lax.broadcasted_iota(jnp.int32, (2 * blk, 2 * blk), 0) % blk
    col = lax.broadcasted_iota(jnp.int32, (2 * blk, 2 * blk), 1)
    bias_rest = jnp.where((col >= row) & (col <= row + blk), 0.0, NEG_BIG).astype(F32)
    bias_first = bias_rest[:, blk:]

    def run_blocks(g, q_ref, k_ref, v_ref, dil, specs):
        ids = range(len(specs))
        qq, kc, vc = [], [], []
        for r, n, first in specs:
            q = q_ref[0, 0, r, pl.ds(pl.multiple_of(n * blk, blk), blk), :]
            zero = jnp.zeros_like(q)
            qq.append(jnp.concatenate([jnp.where(head0, q, zero), jnp.where(head0, zero, q)],
                                      axis=0))
            if first:
                keys = pl.ds(0, blk)
            else:
                keys = pl.ds(pl.multiple_of((n - 1) * blk, blk), 2 * blk)
            kc.append(k_ref[0, 0, r, keys, :])
            vc.append(v_ref[0, 0, r, keys, :])
        sc = [lax.dot_general(qq[i], kc[i], (((1,), (1,)), ((), ())), preferred_element_type=F32)
              + (bias_first if specs[i][2] else bias_rest) for i in ids]
        m = [jnp.max(sc[i], axis=1, keepdims=True) for i in ids]
        p = [jnp.exp(sc[i] - m[i]) for i in ids]
        l = [jnp.sum(p[i], axis=1, keepdims=True) for i in ids]
        pv = [jnp.dot(p[i].astype(BF16), vc[i], preferred_element_type=F32) * (1.0 / l[i])
              for i in ids]
        for i in ids:
            r, n, _ = specs[i]
            lse = m[i] + jnp.log(l[i])
            o_blk = jnp.where(head0, pv[i][:blk], pv[i][blk:])
            lse_blk = jnp.where(head0, jnp.broadcast_to(lse[:blk], (blk, LANES)),
                                jnp.broadcast_to(lse[blk:], (blk, LANES)))
            start = n * (blk * dil) + r
            if dil == 1:
                idx = pl.ds(pl.multiple_of(start, blk), blk)
            else:
                idx = pl.ds(start, blk, stride=dil)
            oacc[g, idx, :] = o_blk
            lacc[g, idx, :] = lse_blk

    for g, dil in enumerate(dils):
        q_ref, k_ref, v_ref = qkv[3 * g:3 * g + 3]
        nb = s_len // dil // blk

        def per_step(rr, carry, g=g, dil=dil, nb=nb, q_ref=q_ref, k_ref=k_ref, v_ref=v_ref):
            if dil == 1:
                res, width = [0], 4
            else:
                res, width = [2 * rr, 2 * rr + 1], 2
            if nb < width:
                run_blocks(g, q_ref, k_ref, v_ref, dil, [(r, 0, True) for r in res])
                return carry
            run_blocks(g, q_ref, k_ref, v_ref, dil,
                       [(r, n, n == 0) for r in res for n in range(width)])

            def per_group(i, c2):
                run_blocks(g, q_ref, k_ref, v_ref, dil,
                           [(r, width * i + n, False) for r in res for n in range(width)])
                return c2

            lax.fori_loop(1, nb // width, per_group, 0)
            return carry

        lax.fori_loop(0, max(dil // 2, 1), per_step, 0)

    tc = 256

    def mix(i, carry):
        rows = pl.ds(pl.multiple_of(i * tc, tc), tc)
        ls = [lacc[g, rows, :] for g in range(n_g)]
        mx = functools.reduce(jnp.maximum, ls)
        es = [jnp.exp(l - mx) for l in ls]
        den = functools.reduce(lambda a, b2: a + b2, es)
        num = es[0] * oacc[0, rows, :]
        for g in range(1, n_g):
            num = num + es[g] * oacc[g, rows, :]
        o_ref[0, 0, rows, :] = (num * (1.0 / den)).astype(o_ref.dtype)
        return carry

    lax.fori_loop(0, s_len // tc, mix, 0)


def attn_core(qkvs, dils, s_len):
    b, n_hp = qkvs[0].shape[:2]
    in_specs = []
    for g, dil in enumerate(dils):
        spec = pl.BlockSpec((1, 1, dil, s_len // dil, LANES), lambda i, j: (i, j, 0, 0, 0))
        in_specs += [spec] * 3
    n_g = len(dils)
    return pl.pallas_call(
        functools.partial(_attn_kernel, s_len=s_len, dils=tuple(dils)),
        grid=(b, n_hp),
        in_specs=in_specs,
        out_specs=pl.BlockSpec((1, 1, s_len, LANES), lambda i, j: (i, j, 0, 0)),
        out_shape=jax.ShapeDtypeStruct((b, n_hp, s_len, LANES), BF16),
        scratch_shapes=[pltpu.VMEM((n_g, s_len, LANES), F32),
                        pltpu.VMEM((n_g, s_len, LANES), F32)],
        compiler_params=_params("parallel", "parallel"),
        name="attn_core",
    )(*qkvs)


def _layer_norm(z, g, b):
    mu = jnp.mean(z, axis=-1, keepdims=True)
    zc = z - mu
    var = jnp.mean(zc * zc, axis=-1, keepdims=True)
    return zc * lax.rsqrt(var + LN_EPS) * g + b


def _out_ln_kernel(o_ref, w_ref, x_ref, gate_ref, g_ref, b_ref, y_ref, *, alpha):
    n_hp = o_ref.shape[1]
    o = jnp.concatenate([o_ref[0, hp] for hp in range(n_hp)], axis=-1)
    y = jnp.dot(o, w_ref[...], preferred_element_type=F32)
    z = alpha * x_ref[0] + gate_ref[0] * y
    y_ref[0] = _layer_norm(z, g_ref[...], b_ref[...])


def out_proj_ln(o_heads, w_out, x, gate1p, ln_g, ln_b, alpha, tm=512):
    b, s, d = x.shape
    n_hp = o_heads.shape[1]
    return pl.pallas_call(
        functools.partial(_out_ln_kernel, alpha=alpha),
        grid=(b, s // tm),
        in_specs=[
            pl.BlockSpec((1, n_hp, tm, LANES), lambda i, j: (i, 0, j, 0)),
            pl.BlockSpec((d, d), lambda i, j: (0, 0)),
            pl.BlockSpec((1, tm, d), lambda i, j: (i, j, 0)),
            pl.BlockSpec((1, 1, d), lambda i, j: (i, 0, 0)),
            pl.BlockSpec((1, d), lambda i, j: (0, 0)),
            pl.BlockSpec((1, d), lambda i, j: (0, 0)),
        ],
        out_specs=pl.BlockSpec((1, tm, d), lambda i, j: (i, j, 0)),
        out_shape=jax.ShapeDtypeStruct((b, s, d), F32),
        compiler_params=_params("parallel", "parallel"),
        name="out_proj_ln",
    )(o_heads, w_out, x, gate1p, ln_g.reshape(1, d), ln_b.reshape(1, d))


def attention_layer(x, mods, cos_t, sin_t, w_in, w_out, ln_g, ln_b, alpha):
    b, s, d = x.shape
    shift = mods[:, None, 0:d]
    scale = mods[:, None, d:2 * d]
    gate1p = 1.0 + mods[:, None, 2 * d:3 * d]
    w_in = w_in.astype(BF16)
    dils = [dil for _, dil in DILATED_GROUPS]
    qkvs = []
    for g, dil in enumerate(dils):
        qkvs += attn_project(x, shift, scale, cos_t, sin_t,
                             w_in[:, g * 3 * d:(g + 1) * 3 * d], dil)
    o = attn_core(qkvs, dils, s)
    return out_proj_ln(o, w_out.astype(BF16), x, gate1p, ln_g, ln_b, alpha)


def _split_bf16(a):
    hi = a.astype(BF16)
    lo = (a - hi.astype(F32)).astype(BF16)
    return hi, lo


def _gdn_proj_kernel(x_ref, shift_ref, scale_ref, w_ref, wab_hi_ref, wab_lo_ref,
                     wabt_hi_ref, wabt_lo_ref, q_ref, k_ref, v_ref, z_ref, ab_ref, abt_ref, *, tm):
    d = x_ref.shape[-1]
    h = x_ref[0] * (1.0 + scale_ref[0]) + shift_ref[0]
    h_hi, h_lo = _split_bf16(h)
    for j, out_ref in enumerate((q_ref, k_ref, v_ref, z_ref)):
        res = jnp.dot(h_hi, w_ref[:, j * d:(j + 1) * d], preferred_element_type=F32)
        for hd in range(d // LANES):
            out_ref[0, hd] = res[:, hd * LANES:(hd + 1) * LANES].astype(BF16)
    ab = jnp.dot(h_hi, wab_hi_ref[...], preferred_element_type=F32)
    ab += jnp.dot(h_lo, wab_hi_ref[...], preferred_element_type=F32)
    ab += jnp.dot(h_hi, wab_lo_ref[...], preferred_element_type=F32)
    ab_ref[0] = ab
    nt = (((1,), (1,)), ((), ()))
    abt = lax.dot_general(wabt_hi_ref[...], h_hi, nt, preferred_element_type=F32)
    abt += lax.dot_general(wabt_hi_ref[...], h_lo, nt, preferred_element_type=F32)
    abt += lax.dot_general(wabt_lo_ref[...], h_hi, nt, preferred_element_type=F32)
    for n in range(tm // GDN_CHUNK):
        abt_ref[0, n] = abt[:, n * GDN_CHUNK:(n + 1) * GDN_CHUNK]


def gdn_project(x, shift, scale, w_in, tm=512):
    b, s, d = x.shape
    n_h = d // LANES
    w_main = w_in[:, :4 * d].astype(BF16)
    w_ab = jnp.pad(w_in[:, 4 * d:], ((0, 0), (0, LANES - 2 * GDN_HEADS)))
    w_abt = w_in[:, 4 * d:].T
    wab_hi, wab_lo = _split_bf16(w_ab)
    wabt_hi, wabt_lo = _split_bf16(w_abt)
    head_sds = jax.ShapeDtypeStruct((b, n_h, s, LANES), BF16)
    head_spec = pl.BlockSpec((1, n_h, tm, LANES), lambda i, j: (i, 0, j, 0))
    n_ab = 2 * GDN_HEADS
    return pl.pallas_call(
        functools.partial(_gdn_proj_kernel, tm=tm),
        grid=(b, s // tm),
        in_specs=[
            pl.BlockSpec((1, tm, d), lambda i, j: (i, j, 0)),
            pl.BlockSpec((1, 1, d), lambda i, j: (i, 0, 0)),
            pl.BlockSpec((1, 1, d), lambda i, j: (i, 0, 0)),
            pl.BlockSpec((d, 4 * d), lambda i, j: (0, 0)),
            pl.BlockSpec((d, LANES), lambda i, j: (0, 0)),
            pl.BlockSpec((d, LANES), lambda i, j: (0, 0)),
            pl.BlockSpec((n_ab, d), lambda i, j: (0, 0)),
            pl.BlockSpec((n_ab, d), lambda i, j: (0, 0)),
        ],
        out_specs=[head_spec] * 4 + [
            pl.BlockSpec((1, tm, LANES), lambda i, j: (i, j, 0)),
            pl.BlockSpec((1, tm // GDN_CHUNK, n_ab, GDN_CHUNK), lambda i, j: (i, j, 0, 0)),
        ],
        out_shape=[head_sds] * 4 + [
            jax.ShapeDtypeStruct((b, s, LANES), F32),
            jax.ShapeDtypeStruct((b, s // GDN_CHUNK, n_ab, GDN_CHUNK), F32),
        ],
        compiler_params=_params("parallel", "parallel"),
        name="gdn_project",
    )(x, shift, scale, w_main, wab_hi, wab_lo, wabt_hi, wabt_lo)


def _softplus(t):
    return jnp.maximum(t, 0.0) + jnp.log(1.0 + jnp.exp(-jnp.abs(t)))


def _sigmoid(t):
    return 1.0 / (1.0 + jnp.exp(-t))


def _dot3_exact_lhs(lhs_bf16, rhs_f32):
    acc = None
    rem = rhs_f32
    for _ in range(3):
        part = rem.astype(BF16)
        rem = rem - part.astype(F32)
        t = jnp.dot(lhs_bf16, part, preferred_element_type=F32)
        acc = t if acc is None else acc + t
    return acc


def _dot3_exact_rhs(lhs_f32, rhs_bf16):
    acc = None
    rem = lhs_f32
    for _ in range(3):
        part = rem.astype(BF16)
        rem = rem - part.astype(F32)
        t = jnp.dot(part, rhs_bf16, preferred_element_type=F32)
        acc = t if acc is None else acc + t
    return acc


def _gdn_kernel(q_ref, k_ref, v_ref, z_ref, ab_ref, abt_ref, convw_ref, alog_l_ref, dtb_l_ref,
                alog_c_ref, dtb_c_ref, normw_ref, o_ref,
                state, carry, qs, ks, vs, gcol_s, beta_s, *, tc):
    n_h = GDN_HEADS
    ch = GDN_CHUNK
    dh = GDN_HEAD_DIM
    halo = 8

    @pl.when(pl.program_id(1) == 0)
    def _():
        state[...] = jnp.zeros_like(state)
        carry[:, :, 0:halo, :] = jnp.zeros((3, n_h, halo, dh), F32)

    for a, (src, dst) in enumerate(((q_ref, qs), (k_ref, ks), (v_ref, vs))):
        for hd in range(n_h):
            xb = carry.at[a, hd]
            xb[halo:halo + tc, :] = src[0, hd].astype(F32)
            w = convw_ref[a, hd]
            y = xb[halo:halo + tc, :] * w[CONV_WIDTH - 1:CONV_WIDTH]
            for j in range(CONV_WIDTH - 1):
                off = halo - (CONV_WIDTH - 1) + j
                y = y + xb[off:off + tc, :] * w[j:j + 1]
            xb[0:halo, :] = xb[tc:tc + halo, :]
            y = y * _sigmoid(y)
            if a < 2:
                y = y * lax.rsqrt(jnp.sum(y * y, axis=-1, keepdims=True) + RMS_EPS)
                if a == 0:
                    y = y * (dh ** -0.5)
            dst[hd] = y

    ab = ab_ref[0]
    gcol_s[...] = -jnp.exp(alog_l_ref[...]) * _softplus(ab + dtb_l_ref[...])
    beta_s[...] = _sigmoid(ab)

    ri = lax.broadcasted_iota(jnp.int32, (ch, ch), 0)
    ci = lax.broadcasted_iota(jnp.int32, (ch, ch), 1)
    causal = ri >= ci
    strict = ri > ci
    tril = jnp.where(causal, 1.0, 0.0).astype(BF16)
    triu = jnp.where(ri <= ci, 1.0, 0.0).astype(BF16)
    eye = jnp.where(ri == ci, 1.0, 0.0).astype(F32)
    nt = (((1,), (1,)), ((), ()))
    tn = (((0,), (0,)), ((), ()))

    def chunk(cidx, carry_unused):
        rows = pl.ds(pl.multiple_of(cidx * ch, ch), ch)
        gc = _dot3_exact_lhs(tril, gcol_s[rows, :])
        a_t = abt_ref[0, cidx][0:n_h, :]
        g_t = -jnp.exp(alog_c_ref[...]) * _softplus(a_t + dtb_c_ref[...])
        gc_t = _dot3_exact_rhs(g_t, triu)
        beta = beta_s[rows, :]
        heads = range(n_h)
        gcol = [gc[:, hd:hd + 1] for hd in heads]
        bcol = [beta[:, n_h + hd:n_h + hd + 1] for hd in heads]
        decay = [jnp.exp(jnp.where(causal, gcol[hd] - gc_t[hd:hd + 1, :], NEG_BIG))
                 for hd in heads]
        qc = [qs[hd, rows, :] for hd in heads]
        kc = [ks[hd, rows, :] for hd in heads]
        kb = [kc[hd] * bcol[hd] for hd in heads]
        both = [lax.dot_general(jnp.concatenate([kb[hd], qc[hd]], axis=0).astype(BF16),
                                kc[hd].astype(BF16), nt, preferred_element_type=F32)
                for hd in heads]
        lmat = [jnp.where(strict, both[hd][:ch] * decay[hd], 0.0) for hd in heads]
        qk = [(both[hd][ch:] * decay[hd]).astype(BF16) for hd in heads]
        pm = [-lmat[hd] for hd in heads]
        mm = lmat
        for _ in range(5):
            mb = [mm[hd].astype(BF16) for hd in heads]
            mm = [jnp.dot(mb[hd], mb[hd], preferred_element_type=F32) for hd in heads]
            pm = [pm[hd] + mm[hd] + jnp.dot(pm[hd].astype(BF16), mm[hd].astype(BF16),
                                            preferred_element_type=F32) for hd in heads]
        eg = [jnp.exp(gcol[hd]) for hd in heads]
        rhs = [jnp.concatenate([vs[hd, rows, :] * bcol[hd], kb[hd] * eg[hd]], axis=1)
               for hd in heads]
        sol = [rhs[hd] + jnp.dot(pm[hd].astype(BF16), rhs[hd].astype(BF16),
                                 preferred_element_type=F32) for hd in heads]
        glast = [gcol[hd][ch - 1:ch, :] for hd in heads]
        kdec = [(kc[hd] * jnp.exp(glast[hd] - gcol[hd])).astype(BF16) for hd in heads]
        s_prev = [state[hd] for hd in heads]
        wq = [jnp.dot(jnp.concatenate([sol[hd][:, dh:], qc[hd] * eg[hd]], axis=0).astype(BF16),
                      s_prev[hd].astype(BF16), preferred_element_type=F32)
              for hd in heads]
        vb = [(sol[hd][:, :dh] - wq[hd][:ch]).astype(BF16) for hd in heads]
        o = [wq[hd][ch:] + jnp.dot(qk[hd], vb[hd], preferred_element_type=F32) for hd in heads]
        for hd in heads:
            state[hd] = s_prev[hd] * jnp.exp(glast[hd]) + lax.dot_general(
                kdec[hd], vb[hd], tn, preferred_element_type=F32)
        for hd in heads:
            on = o[hd] * lax.rsqrt(jnp.mean(o[hd] * o[hd], axis=-1, keepdims=True)
                                   + RMS_EPS) * normw_ref[...]
            zc = z_ref[0, hd, rows, :].astype(F32)
            o_ref[0, hd, rows, :] = (on * (zc * _sigmoid(zc))).astype(o_ref.dtype)
        return carry_unused

    lax.fori_loop(0, tc // ch, chunk, 0)


def gdn_core(q, k, v, z, ab, abt, conv_w, a_log, dt_bias, norm_w, tc=512):
    b, n_h, s, dh = q.shape
    n_ab = 2 * n_h
    convw = conv_w.reshape(CONV_WIDTH, 3, n_h, dh).transpose(1, 2, 0, 3)
    pad = (0, LANES - n_h)
    alog_l = jnp.pad(a_log.astype(F32), pad).reshape(1, LANES)
    dtb_l = jnp.pad(dt_bias.astype(F32), pad).reshape(1, LANES)
    alog_c = a_log.astype(F32).reshape(n_h, 1)
    dtb_c = dt_bias.astype(F32).reshape(n_h, 1)
    head_spec = pl.BlockSpec((1, n_h, tc, dh), lambda i, j: (i, 0, j, 0))

    def full(shape):
        return pl.BlockSpec(shape, lambda i, j: (0,) * len(shape))

    return pl.pallas_call(
        functools.partial(_gdn_kernel, tc=tc),
        grid=(b, s // tc),
        in_specs=[head_spec] * 4 + [
            pl.BlockSpec((1, tc, LANES), lambda i, j: (i, j, 0)),
            pl.BlockSpec((1, tc // GDN_CHUNK, n_ab, GDN_CHUNK), lambda i, j: (i, j, 0, 0)),
            full((3, n_h, CONV_WIDTH, dh)),
            full((1, LANES)), full((1, LANES)), full((n_h, 1)), full((n_h, 1)), full((1, dh)),
        ],
        out_specs=head_spec,
        out_shape=jax.ShapeDtypeStruct((b, n_h, s, dh), BF16),
        scratch_shapes=[
            pltpu.VMEM((n_h, dh, dh), F32),
            pltpu.VMEM((3, n_h, 8 + tc, dh), F32),
            pltpu.VMEM((n_h, tc, dh), F32),
            pltpu.VMEM((n_h, tc, dh), F32),
            pltpu.VMEM((n_h, tc, dh), F32),
            pltpu.VMEM((tc, LANES), F32),
            pltpu.VMEM((tc, LANES), F32),
        ],
        compiler_params=_params("parallel", "arbitrary"),
        name="gdn_core",
    )(q, k, v, z, ab, abt, convw, alog_l, dtb_l, alog_c, dtb_c, norm_w.reshape(1, dh).astype(F32))


def gdn_layer(x, mods, w_in, conv_w, a_log, dt_bias, norm_w, w_out, ln_g, ln_b, alpha):
    b, s, d = x.shape
    shift = mods[:, None, 0:d]
    scale = mods[:, None, d:2 * d]
    gate1p = 1.0 + mods[:, None, 2 * d:3 * d]
    q, k, v, z, ab, abt = gdn_project(x, shift, scale, w_in)
    o = gdn_core(q, k, v, z, ab, abt, conv_w, a_log, dt_bias, norm_w)
    return out_proj_ln(o, w_out.astype(BF16), x, gate1p, ln_g, ln_b, alpha)


def _first_argmax(vals, idx, n):
    mx = jnp.max(vals, axis=0, keepdims=True)
    first = jnp.min(jnp.where(vals == mx, idx, n), axis=0, keepdims=True)
    return mx, first


def _router_kernel(x_ref, shift_ref, scale_ref, rwt_hi_ref, rwt_lo_ref, bias_ref,
                   eidx_ref, rank_ref, gcol_ref, cnt_ref, cnt_scr, *, tm):
    n_e = N_EXPERTS
    per_g = n_e // N_EXPERT_GROUPS

    @pl.when(pl.program_id(0) == 0)
    def _():
        cnt_scr[...] = jnp.zeros_like(cnt_scr)

    h = x_ref[...] * (1.0 + scale_ref[0]) + shift_ref[0]
    h_hi, h_lo = _split_bf16(h)
    nt = (((1,), (1,)), ((), ()))
    logits = lax.dot_general(rwt_hi_ref[...], h_hi, nt, preferred_element_type=F32)
    logits += lax.dot_general(rwt_hi_ref[...], h_lo, nt, preferred_element_type=F32)
    logits += lax.dot_general(rwt_lo_ref[...], h_hi, nt, preferred_element_type=F32)
    scores = _sigmoid(logits)
    biased = scores + bias_ref[...]

    r8 = lax.broadcasted_iota(jnp.int32, (per_g, tm), 0)
    gscores = []
    for g in range(N_EXPERT_GROUPS):
        blk = biased[g * per_g:(g + 1) * per_g]
        m1, i1 = _first_argmax(blk, r8, per_g)
        m2 = jnp.max(jnp.where(r8 == i1, NEG_BIG, blk), axis=0, keepdims=True)
        gscores.append(m1 + m2)
    vals = jnp.concatenate(gscores, axis=0)
    rg = lax.broadcasted_iota(jnp.int32, (N_EXPERT_GROUPS, tm), 0)
    gsel = jnp.zeros((N_EXPERT_GROUPS, tm), F32)
    for _ in range(TOPK_GROUPS):
        _, idx = _first_argmax(vals, rg, N_EXPERT_GROUPS)
        hit = rg == idx
        gsel = jnp.where(hit, 1.0, gsel)
        vals = jnp.where(hit, NEG_BIG, vals)
    emask = jnp.concatenate(
        [jnp.broadcast_to(gsel[g:g + 1], (per_g, tm)) for g in range(N_EXPERT_GROUPS)], axis=0)
    masked = jnp.where(emask > 0.5, biased, NEG_BIG)

    re = lax.broadcasted_iota(jnp.int32, (n_e, tm), 0)
    hits, eidx, gates = [], [], []
    for _ in range(TOP_K):
        _, idx = _first_argmax(masked, re, n_e)
        hit = re == idx
        gates.append(jnp.sum(jnp.where(hit, scores, 0.0), axis=0, keepdims=True))
        masked = jnp.where(hit, NEG_BIG, masked)
        hits.append(hit)
        eidx.append(idx)
    gates = jnp.concatenate(gates, axis=0)
    gates = gates / jnp.sum(gates, axis=0, keepdims=True) * ROUTED_SCALE
    onehot = jnp.zeros((n_e, tm), F32)
    for hit in hits:
        onehot = jnp.where(hit, 1.0, onehot)

    ti = lax.broadcasted_iota(jnp.int32, (tm, tm), 0)
    tj = lax.broadcasted_iota(jnp.int32, (tm, tm), 1)
    before = jnp.where(ti < tj, 1.0, 0.0).astype(BF16)
    prefix = jnp.dot(onehot.astype(BF16), before, preferred_element_type=F32) + cnt_scr[...]
    ranks = [jnp.sum(jnp.where(hit, prefix, 0.0), axis=0, keepdims=True) for hit in hits]
    cnt_scr[...] = cnt_scr[...] + jnp.sum(onehot, axis=1, keepdims=True)

    eidx_ref[...] = jnp.concatenate(eidx, axis=0)
    rank_ref[...] = jnp.concatenate(ranks, axis=0).astype(jnp.int32)
    cnt_ref[...] = jnp.broadcast_to(cnt_scr[...], cnt_ref.shape)
    gpad = jnp.concatenate([gates, jnp.zeros((LANES - TOP_K, tm), F32)], axis=0)
    gcol_ref[...] = gpad.T


def moe_route(xt, shift, scale, router_w, router_bias, s_len, tm=512):
    t, d = xt.shape
    n_e = router_w.shape[1]
    rwt_hi, rwt_lo = _split_bf16(router_w.T)
    per_b = s_len // tm
    return pl.pallas_call(
        functools.partial(_router_kernel, tm=tm),
        grid=(t // tm,),
        in_specs=[
            pl.BlockSpec((tm, d), lambda i: (i, 0)),
            pl.BlockSpec((1, 1, d), lambda i: (i // per_b, 0, 0)),
            pl.BlockSpec((1, 1, d), lambda i: (i // per_b, 0, 0)),
            pl.BlockSpec((n_e, d), lambda i: (0, 0)),
            pl.BlockSpec((n_e, d), lambda i: (0, 0)),
            pl.BlockSpec((n_e, 1), lambda i: (0, 0)),
        ],
        out_specs=[
            pl.BlockSpec((TOP_K, tm), lambda i: (0, i)),
            pl.BlockSpec((TOP_K, tm), lambda i: (0, i)),
            pl.BlockSpec((tm, LANES), lambda i: (i, 0)),
            pl.BlockSpec((n_e, LANES), lambda i: (0, 0)),
        ],
        out_shape=[
            jax.ShapeDtypeStruct((TOP_K, t), jnp.int32),
            jax.ShapeDtypeStruct((TOP_K, t), jnp.int32),
            jax.ShapeDtypeStruct((t, LANES), F32),
            jax.ShapeDtypeStruct((n_e, LANES), F32),
        ],
        scratch_shapes=[pltpu.VMEM((n_e, 1), F32)],
        compiler_params=_params("arbitrary"),
        name="moe_route",
    )(xt, shift, scale, rwt_hi, rwt_lo, router_bias.reshape(n_e, 1).astype(F32))


def _slots_kernel(cnt_ref, eidx_ref, rank_ref, dest_ref, blk_e_ref, nused_ref, *, block_rows):
    n_e = N_EXPERTS
    cnt = cnt_ref[...].astype(jnp.int32)
    padded = ((cnt + (block_rows - 1)) // block_rows) * block_rows
    ri = lax.broadcasted_iota(jnp.int32, (n_e, n_e), 0)
    ci = lax.broadcasted_iota(jnp.int32, (n_e, n_e), 1)
    tril = jnp.where(ri >= ci, 1.0, 0.0).astype(BF16)
    pend = _dot3_exact_lhs(tril, padded.astype(F32))
    pstart = pend.astype(jnp.int32) - padded
    eidx = eidx_ref[...]
    dest = rank_ref[...]
    for e in range(n_e):
        dest = dest + jnp.where(eidx == e, pstart[e:e + 1, 0:1], 0)
    dest_ref[...] = dest
    nb = blk_e_ref.shape[1]
    first_row = lax.broadcasted_iota(jnp.int32, (n_e, nb), 1) * block_rows
    pend_i = pend.astype(jnp.int32)[:, 0:1]
    owner = jnp.sum(jnp.where(pend_i <= first_row, 1, 0), axis=0, keepdims=True)
    blk_e_ref[...] = jnp.minimum(owner, n_e - 1)
    nused_ref[...] = pend_i[n_e - 1:n_e, :] // block_rows + jnp.zeros(nused_ref.shape, jnp.int32)


def moe_slots(counts, eidx, rank, n_blocks, block_rows, tt=2048):
    t = eidx.shape[1]
    tt = min(tt, t)
    nb_pad = -(-n_blocks // LANES) * LANES
    n_e = counts.shape[0]
    return pl.pallas_call(
        functools.partial(_slots_kernel, block_rows=block_rows),
        grid=(t // tt,),
        in_specs=[
            pl.BlockSpec((n_e, LANES), lambda i: (0, 0)),
            pl.BlockSpec((TOP_K, tt), lambda i: (0, i)),
            pl.BlockSpec((TOP_K, tt), lambda i: (0, i)),
        ],
        out_specs=[
            pl.BlockSpec((TOP_K, tt), lambda i: (0, i)),
            pl.BlockSpec((1, nb_pad), lambda i: (0, 0)),
            pl.BlockSpec((1, LANES), lambda i: (0, 0)),
        ],
        out_shape=[
            jax.ShapeDtypeStruct((TOP_K, t), jnp.int32),
            jax.ShapeDtypeStruct((1, nb_pad), jnp.int32),
            jax.ShapeDtypeStruct((1, LANES), jnp.int32),
        ],
        compiler_params=_params("arbitrary"),
        name="moe_slots",
    )(counts, eidx, rank)


TILE_ROWS = 8


def _tile_copy(src, src_tok, dst, dst_tok, sem):
    return pltpu.make_async_copy(
        src.at[pl.ds(pl.multiple_of(src_tok * TILE_ROWS, TILE_ROWS), TILE_ROWS)],
        dst.at[pl.ds(pl.multiple_of(dst_tok * TILE_ROWS, TILE_ROWS), TILE_ROWS)], sem)


def _to_token_tiles(ref, val, n_tok):
    for cidx in range(TILE_ROWS):
        ref[pl.ds(cidx, n_tok, stride=TILE_ROWS), :] = val[:, cidx * LANES:(cidx + 1) * LANES]


def _from_token_tiles(ref, n_tok):
    return jnp.concatenate(
        [ref[pl.ds(cidx, n_tok, stride=TILE_ROWS), :] for cidx in range(TILE_ROWS)], axis=1)


def _dispatch_kernel(dest_ref, x_ref, shift_ref, scale_ref, xs_hbm, hs, sem, *, tm, t_total):
    base = pl.program_id(0) * tm
    _to_token_tiles(hs, x_ref[...] * (1.0 + scale_ref[0]) + shift_ref[0], tm)

    def issue(t, carry):
        for k in range(TOP_K):
            _tile_copy(hs, t, xs_hbm, dest_ref[k * t_total + base + t], sem).start()
        return carry

    lax.fori_loop(0, tm, issue, 0, unroll=2)
    for k in range(TOP_K):
        pltpu.make_async_copy(hs, xs_hbm.at[pl.ds(0, tm * TILE_ROWS)], sem).wait()


def moe_dispatch(dest_flat, xt, shift, scale, n_rows, s_len, tm=256):
    t, d = xt.shape
    per_b = s_len // tm
    grid_spec = pltpu.PrefetchScalarGridSpec(
        num_scalar_prefetch=1,
        grid=(t // tm,),
        in_specs=[
            pl.BlockSpec((tm, d), lambda i, dest: (i, 0)),
            pl.BlockSpec((1, 1, d), lambda i, dest: (i // per_b, 0, 0)),
            pl.BlockSpec((1, 1, d), lambda i, dest: (i // per_b, 0, 0)),
        ],
        out_specs=pl.BlockSpec(memory_space=pl.ANY),
        scratch_shapes=[pltpu.VMEM((tm * TILE_ROWS, LANES), F32), pltpu.SemaphoreType.DMA],
    )
    assert d == TILE_ROWS * LANES
    return pl.pallas_call(
        functools.partial(_dispatch_kernel, tm=tm, t_total=t),
        grid_spec=grid_spec,
        out_shape=jax.ShapeDtypeStruct((n_rows * TILE_ROWS, LANES), F32),
        compiler_params=_params("arbitrary"),
        name="moe_dispatch",
    )(dest_flat, xt, shift, scale)


def _expert_kernel(blk_e_ref, nused_ref, xs_ref, w1_ref, w3_ref, w2_ref, ys_ref,
                   w1b, w3b, w2b, *, block_rows):
    i = pl.program_id(0)
    prev = blk_e_ref[jnp.maximum(i - 1, 0)]

    @pl.when((i == 0) | (blk_e_ref[i] != prev))
    def _():
        w1b[...] = w1_ref[0].astype(BF16)
        w3b[...] = w3_ref[0].astype(BF16)
        w2b[...] = w2_ref[0].astype(BF16)

    @pl.when(i < nused_ref[0])
    def _():
        xb = _from_token_tiles(xs_ref, block_rows).astype(BF16)
        h1 = jnp.dot(xb, w1b[...], preferred_element_type=F32)
        h3 = jnp.dot(xb, w3b[...], preferred_element_type=F32)
        act = (h1 * _sigmoid(h1)) * h3
        y = jnp.dot(act.astype(BF16), w2b[...], preferred_element_type=F32)
        _to_token_tiles(ys_ref, y, block_rows)


def moe_experts(blk_e, nused, xs, w1, w3, w2, block_rows, first_expert):
    d, de = w1.shape[-2:]
    n_rows = xs.shape[0] // TILE_ROWS
    n_blocks = n_rows // block_rows

    def row_map(i, be, nu):
        return (jnp.minimum(i, nu[0] - 1), 0)

    def w_map(i, be, nu):
        return (be[i] + first_expert, 0, 0)

    grid_spec = pltpu.PrefetchScalarGridSpec(
        num_scalar_prefetch=2,
        grid=(n_blocks,),
        in_specs=[
            pl.BlockSpec((block_rows * TILE_ROWS, LANES), row_map),
            pl.BlockSpec((1, d, de), w_map),
            pl.BlockSpec((1, d, de), w_map),
            pl.BlockSpec((1, de, d), w_map),
        ],
        out_specs=pl.BlockSpec((block_rows * TILE_ROWS, LANES), row_map),
        scratch_shapes=[pltpu.VMEM((d, de), BF16), pltpu.VMEM((d, de), BF16),
                        pltpu.VMEM((de, d), BF16)],
    )
    return pl.pallas_call(
        functools.partial(_expert_kernel, block_rows=block_rows),
        grid_spec=grid_spec,
        out_shape=jax.ShapeDtypeStruct((n_rows * TILE_ROWS, LANES), F32),
        compiler_params=_params("arbitrary"),
        name="moe_experts",
    )(blk_e, nused, xs, w1, w3, w2)


def _combine_kernel(dest_ref, ys_hbm, x_ref, shift_ref, scale_ref, gate_ref, gcol_ref,
                    ws1_ref, ws3_ref, ws2_ref, g_ref, b_ref, o_ref, buf, sem,
                    *, tm, t_total, alpha):
    base = pl.program_id(0) * tm

    def issue(t, carry):
        for k in range(TOP_K):
            _tile_copy(ys_hbm, dest_ref[k * t_total + base + t], buf.at[k], t, sem).start()
        return carry

    lax.fori_loop(0, tm, issue, 0, unroll=2)

    x = x_ref[...]
    hb = (x * (1.0 + scale_ref[0]) + shift_ref[0]).astype(BF16)
    h1 = jnp.dot(hb, ws1_ref[...], preferred_element_type=F32)
    h3 = jnp.dot(hb, ws3_ref[...], preferred_element_type=F32)
    act = (h1 * _sigmoid(h1)) * h3
    y = jnp.dot(act.astype(BF16), ws2_ref[...], preferred_element_type=F32)

    for k in range(TOP_K):
        pltpu.make_async_copy(ys_hbm.at[pl.ds(0, tm * TILE_ROWS)], buf.at[k], sem).wait()

    gcol = gcol_ref[...]
    for k in range(TOP_K):
        y = y + _from_token_tiles(buf.at[k], tm) * gcol[:, k:k + 1]
    z = alpha * x + gate_ref[0] * y
    o_ref[...] = _layer_norm(z, g_ref[...], b_ref[...])


def moe_combine(dest_flat, ys, xt, shift, scale, gate1p, gcol, ws1, ws3, ws2, ln_g, ln_b,
                s_len, alpha, tm=128):
    t, d = xt.shape
    ds_ = ws1.shape[1]
    per_b = s_len // tm

    def const(shape):
        return pl.BlockSpec(shape, lambda i, dest: (0,) * len(shape))

    mod_spec = pl.BlockSpec((1, 1, d), lambda i, dest: (i // per_b, 0, 0))
    grid_spec = pltpu.PrefetchScalarGridSpec(
        num_scalar_prefetch=1,
        grid=(t // tm,),
        in_specs=[
            pl.BlockSpec(memory_space=pl.ANY),
            pl.BlockSpec((tm, d), lambda i, dest: (i, 0)),
            mod_spec, mod_spec, mod_spec,
            pl.BlockSpec((tm, LANES), lambda i, dest: (i, 0)),
            const((d, ds_)), const((d, ds_)), const((ds_, d)),
            const((1, d)), const((1, d)),
        ],
        out_specs=pl.BlockSpec((tm, d), lambda i, dest: (i, 0)),
        scratch_shapes=[pltpu.VMEM((TOP_K, tm * TILE_ROWS, LANES), F32),
                        pltpu.SemaphoreType.DMA],
    )
    return pl.pallas_call(
        functools.partial(_combine_kernel, tm=tm, t_total=t, alpha=alpha),
        grid_spec=grid_spec,
        out_shape=jax.ShapeDtypeStruct((t, d), F32),
        compiler_params=_params("arbitrary"),
        name="moe_combine",
    )(dest_flat, ys, xt, shift, scale, gate1p, gcol, ws1, ws3, ws2,
      ln_g.reshape(1, d), ln_b.reshape(1, d))


def moe_layer(x, mods, router_w, router_bias, w1, w3, w2, ws1, ws3, ws2, ln_g, ln_b, alpha,
              layer):
    b, s, d = x.shape
    t = b * s
    shift = mods[:, None, 0:d]
    scale = mods[:, None, d:2 * d]
    gate1p = 1.0 + mods[:, None, 2 * d:3 * d]
    xt = x.reshape(t, d)
    n_blocks = t * TOP_K // MOE_BLOCK + N_EXPERTS
    eidx, rank, gcol, counts = moe_route(xt, shift, scale, router_w, router_bias, s)
    dest, blk_e, nused = moe_slots(counts, eidx, rank, n_blocks, MOE_BLOCK)
    dest_flat = dest.reshape(-1)
    xs = moe_dispatch(dest_flat, xt, shift, scale, n_blocks * MOE_BLOCK, s)
    n_e = w1.shape[1]
    ys = moe_experts(blk_e.reshape(-1), nused.reshape(-1), xs,
                     w1.reshape((-1,) + w1.shape[2:]), w3.reshape((-1,) + w3.shape[2:]),
                     w2.reshape((-1,) + w2.shape[2:]), MOE_BLOCK, layer * n_e)
    out = moe_combine(dest_flat, ys, xt, shift, scale, gate1p, gcol,
                      ws1.astype(BF16), ws3.astype(BF16), ws2.astype(BF16), ln_g, ln_b, s, alpha)
    return out.reshape(b, s, d)


def kernel(x, c, positions, ada_w, ada_b, ln_g, ln_b, attn_w_in, attn_w_out, gdn_w_in, gdn_conv_w,
           gdn_a_log, gdn_dt_bias, gdn_norm_w, gdn_w_out, router_w, router_bias, expert_w1,
           expert_w3, expert_w2, shared_w1, shared_w3, shared_w2):
    depth = ada_w.shape[0]
    alpha = (2 * depth) ** 0.25
    mods = ada_vectors(c, ada_w, ada_b)
    cos_t, sin_t = rope_tables(positions)
    for i in range(depth):
        j = i // 2
        if i % 2 == 0:
            x = attention_layer(x, mods[2 * i], cos_t, sin_t, attn_w_in[j], attn_w_out[j],
                                ln_g[i, 0], ln_b[i, 0], alpha)
        else:
            x = gdn_layer(x, mods[2 * i], gdn_w_in[j], gdn_conv_w[j], gdn_a_log[j],
                          gdn_dt_bias[j], gdn_norm_w[j], gdn_w_out[j], ln_g[i, 0], ln_b[i, 0],
                          alpha)
        x = moe_layer(x, mods[2 * i + 1], router_w[i], router_bias[i], expert_w1,
                      expert_w3, expert_w2, shared_w1[i], shared_w3[i], shared_w2[i],
                      ln_g[i, 1], ln_b[i, 1], alpha, i)
    return x
```

```python
import functools
import math

import jax
import jax.numpy as jnp
from jax import lax
from jax.experimental import pallas as pl
from jax.experimental.pallas import tpu as pltpu

F32 = jnp.float32
BF16 = jnp.bfloat16

LANES = 128

DILATED_GROUPS = ((128, 1), (512, 4), (2048, 16))
A_HEADS = 16
A_HEAD_DIM = 64
ATTN_BLOCK = 128
ROPE_THETA = 10000.0
GDN_HEADS = 8
GDN_HEAD_DIM = 128
CONV_WIDTH = 4
GDN_CHUNK = 64
N_EXPERTS = 64
TOP_K = 8
N_EXPERT_GROUPS = 8
TOPK_GROUPS = 4
ROUTED_SCALE = 2.5
MOE_BLOCK = 512
LN_EPS = 1e-5
RMS_EPS = 1e-6
NEG_BIG = -1e30

VMEM_LIMIT = 56 * 1024 * 1024


def _params(*sem):
    return pltpu.CompilerParams(dimension_semantics=sem, vmem_limit_bytes=VMEM_LIMIT)


def _ada_kernel(c_ref, w_ref, b_ref, o_ref):
    c = c_ref[...]
    c_hi = c.astype(BF16)
    c_lo = (c - c_hi.astype(F32)).astype(BF16)
    w = w_ref[0]
    w_hi = w.astype(BF16)
    w_lo = (w - w_hi.astype(F32)).astype(BF16)
    acc = jnp.dot(c_hi, w_hi, preferred_element_type=F32)
    acc += jnp.dot(c_hi, w_lo, preferred_element_type=F32)
    acc += jnp.dot(c_lo, w_hi, preferred_element_type=F32)
    o_ref[0] = acc + b_ref[0]


def ada_vectors(c, ada_w, ada_b):
    depth2 = ada_w.shape[0] * ada_w.shape[1]
    b, d = c.shape
    d3 = ada_w.shape[-1]
    tn = d
    w = ada_w.reshape(depth2, d, d3)
    bias = ada_b.reshape(depth2, 1, d3)
    return pl.pallas_call(
        _ada_kernel,
        grid=(depth2, d3 // tn),
        in_specs=[
            pl.BlockSpec((b, d), lambda l, j: (0, 0)),
            pl.BlockSpec((1, d, tn), lambda l, j: (l, 0, j)),
            pl.BlockSpec((1, 1, tn), lambda l, j: (l, 0, j)),
        ],
        out_specs=pl.BlockSpec((1, b, tn), lambda l, j: (l, 0, j)),
        out_shape=jax.ShapeDtypeStruct((depth2, b, d3), F32),
        compiler_params=_params("parallel", "parallel"),
        name="ada_vectors",
    )(c, w, bias)


def _rope_kernel(pos_ref, freq_ref, sign_ref, cos_ref, sin_ref):
    ang = pos_ref[0].astype(F32) * freq_ref[...]
    cos_ref[0] = jnp.cos(ang)
    sin_ref[0] = jnp.sin(ang) * sign_ref[...]


def rope_tables(positions):
    b, s = positions.shape
    half = A_HEAD_DIM // 2
    inv_freq = ROPE_THETA ** (-jnp.arange(half, dtype=F32) / half)
    freq = jnp.tile(inv_freq, LANES // half).reshape(1, LANES)
    sign = jnp.tile(jnp.concatenate([-jnp.ones((half,), F32), jnp.ones((half,), F32)]),
                    LANES // A_HEAD_DIM).reshape(1, LANES)
    ts = min(s, 1024)
    return pl.pallas_call(
        _rope_kernel,
        grid=(b, s // ts),
        in_specs=[
            pl.BlockSpec((1, ts, 1), lambda i, j: (i, j, 0)),
            pl.BlockSpec((1, LANES), lambda i, j: (0, 0)),
            pl.BlockSpec((1, LANES), lambda i, j: (0, 0)),
        ],
        out_specs=[pl.BlockSpec((1, ts, LANES), lambda i, j: (i, j, 0))] * 2,
        out_shape=[jax.ShapeDtypeStruct((b, s, LANES), F32)] * 2,
        compiler_params=_params("parallel", "parallel"),
        name="rope_tables",
    )(positions.reshape(b, s, 1), freq, sign)


def _residue_major(ref, lead, dil, tl):
    if dil == 1:
        return ref[lead] if lead is not None else ref[...]
    parts = []
    for r in range(dil):
        if lead is None:
            parts.append(ref[pl.ds(r, tl, stride=dil), :])
        else:
            parts.append(ref[lead, pl.ds(r, tl, stride=dil), :])
    return jnp.concatenate(parts, axis=0)


def _attn_proj_kernel(x_ref, shift_ref, scale_ref, cos_ref, sin_ref, w_ref,
                      q_ref, k_ref, v_ref, h_scr, *, dil, tm):
    tl = tm // dil
    d = x_ref.shape[-1]
    h = x_ref[0] * (1.0 + scale_ref[0]) + shift_ref[0]
    if dil == 1:
        hb = h.astype(BF16)
    else:
        for cidx in range(d // LANES):
            h_scr[cidx] = h[:, cidx * LANES:(cidx + 1) * LANES]
        hb = jnp.concatenate(
            [_residue_major(h_scr, cidx, dil, tl).astype(BF16) for cidx in range(d // LANES)],
            axis=1)
    cosp = _residue_major(cos_ref, 0, dil, tl)
    sinp = _residue_major(sin_ref, 0, dil, tl)
    lane = lax.broadcasted_iota(jnp.int32, (tm, LANES), 1)
    first_half = (lane % A_HEAD_DIM) < (A_HEAD_DIM // 2)
    qscale = A_HEAD_DIM ** -0.5
    for j, out_ref in enumerate((q_ref, k_ref, v_ref)):
        res = jnp.dot(hb, w_ref[:, j * d:(j + 1) * d], preferred_element_type=F32)
        for hp in range(d // LANES):
            blk = res[:, hp * LANES:(hp + 1) * LANES]
            if j < 2:
                swapped = jnp.where(first_half,
                                    pltpu.roll(blk, LANES - A_HEAD_DIM // 2, 1),
                                    pltpu.roll(blk, A_HEAD_DIM // 2, 1))
                blk = blk * cosp + swapped * sinp
                if j == 0:
                    blk = blk * qscale
            blk = blk.astype(BF16)
            for r in range(dil):
                out_ref[0, hp, r] = blk[r * tl:(r + 1) * tl]


def attn_project(x, shift, scale, cos_t, sin_t, w_g, dil, tm=512):
    b, s, d = x.shape
    tl = tm // dil
    n_hp = d // LANES
    out_sds = jax.ShapeDtypeStruct((b, n_hp, dil, s // dil, LANES), BF16)
    out_spec = pl.BlockSpec((1, n_hp, dil, tl, LANES), lambda i, j: (i, 0, 0, j, 0))
    return pl.pallas_call(
        functools.partial(_attn_proj_kernel, dil=dil, tm=tm),
        grid=(b, s // tm),
        in_specs=[
            pl.BlockSpec((1, tm, d), lambda i, j: (i, j, 0)),
            pl.BlockSpec((1, 1, d), lambda i, j: (i, 0, 0)),
            pl.BlockSpec((1, 1, d), lambda i, j: (i, 0, 0)),
            pl.BlockSpec((1, tm, LANES), lambda i, j: (i, j, 0)),
            pl.BlockSpec((1, tm, LANES), lambda i, j: (i, j, 0)),
            pl.BlockSpec((d, 3 * d), lambda i, j: (0, 0)),
        ],
        out_specs=[out_spec] * 3,
        out_shape=[out_sds] * 3,
        scratch_shapes=[pltpu.VMEM((d // LANES, tm, LANES), F32)],
        compiler_params=_params("parallel", "parallel"),
        name=f"attn_project_d{dil}",
    )(x, shift, scale, cos_t, sin_t, w_g)


def _attn_kernel(*refs, s_len, dils):
    n_g = len(dils)
    qkv = refs[:3 * n_g]
    o_ref = refs[3 * n_g]
    oacc, lacc = refs[3 * n_g + 1:]
    blk = ATTN_BLOCK

    lane = lax.broadcasted_iota(jnp.int32, (blk, LANES), 1)
    head0 = lane < A_HEAD_DIM
    row = lax.broadcasted_iota(jnp.int32, (2 * blk, 2 * blk), 0) % blk
    col = lax.broadcasted_iota(jnp.int32, (2 * blk, 2 * blk), 1)
    bias_rest = jnp.where((col >= row) & (col <= row + blk), 0.0, NEG_BIG).astype(F32)
    bias_first = bias_rest[:, blk:]

    def run_blocks(g, q_ref, k_ref, v_ref, dil, specs):
        ids = range(len(specs))
        qq, kc, vc = [], [], []
        for r, n, first in specs:
            q = q_ref[0, 0, r, pl.ds(pl.multiple_of(n * blk, blk), blk), :]
            zero = jnp.zeros_like(q)
            qq.append(jnp.concatenate([jnp.where(head0, q, zero), jnp.where(head0, zero, q)],
                                      axis=0))
            if first:
                keys = pl.ds(0, blk)
            else:
                keys = pl.ds(pl.multiple_of((n - 1) * blk, blk), 2 * blk)
            kc.append(k_ref[0, 0, r, keys, :])
            vc.append(v_ref[0, 0, r, keys, :])
        sc = [lax.dot_general(qq[i], kc[i], (((1,), (1,)), ((), ())), preferred_element_type=F32)
              + (bias_first if specs[i][2] else bias_rest) for i in ids]
        m = [jnp.max(sc[i], axis=1, keepdims=True) for i in ids]
        p = [jnp.exp(sc[i] - m[i]) for i in ids]
        l = [jnp.sum(p[i], axis=1, keepdims=True) for i in ids]
        pv = [jnp.dot(p[i].astype(BF16), vc[i], preferred_element_type=F32) * (1.0 / l[i])
              for i in ids]
        for i in ids:
            r, n, _ = specs[i]
            lse = m[i] + jnp.log(l[i])
            o_blk = jnp.where(head0, pv[i][:blk], pv[i][blk:])
            lse_blk = jnp.where(head0, jnp.broadcast_to(lse[:blk], (blk, LANES)),
                                jnp.broadcast_to(lse[blk:], (blk, LANES)))
            start = n * (blk * dil) + r
            if dil == 1:
                idx = pl.ds(pl.multiple_of(start, blk), blk)
            else:
                idx = pl.ds(start, blk, stride=dil)
            oacc[g, idx, :] = o_blk
            lacc[g, idx, :] = lse_blk

    for g, dil in enumerate(dils):
        q_ref, k_ref, v_ref = qkv[3 * g:3 * g + 3]
        nb = s_len // dil // blk

        def per_step(rr, carry, g=g, dil=dil, nb=nb, q_ref=q_ref, k_ref=k_ref, v_ref=v_ref):
            if dil == 1:
                res, width = [0], 4
            else:
                res, width = [2 * rr, 2 * rr + 1], 2
            if nb < width:
                run_blocks(g, q_ref, k_ref, v_ref, dil, [(r, 0, True) for r in res])
                return carry
            run_blocks(g, q_ref, k_ref, v_ref, dil,
                       [(r, n, n == 0) for r in res for n in range(width)])

            def per_group(i, c2):
                run_blocks(g, q_ref, k_ref, v_ref, dil,
                           [(r, width * i + n, False) for r in res for n in range(width)])
                return c2

            lax.fori_loop(1, nb // width, per_group, 0)
            return carry

        lax.fori_loop(0, max(dil // 2, 1), per_step, 0)

    tc = 256

    def mix(i, carry):
        rows = pl.ds(pl.multiple_of(i * tc, tc), tc)
        ls = [lacc[g, rows, :] for g in range(n_g)]
        mx = functools.reduce(jnp.maximum, ls)
        es = [jnp.exp(l - mx) for l in ls]
        den = functools.reduce(lambda a, b2: a + b2, es)
        num = es[0] * oacc[0, rows, :]
        for g in range(1, n_g):
            num = num + es[g] * oacc[g, rows, :]
        o_ref[0, 0, rows, :] = (num * (1.0 / den)).astype(o_ref.dtype)
        return carry

    lax.fori_loop(0, s_len // tc, mix, 0)


def attn_core(qkvs, dils, s_len):
    b, n_hp = qkvs[0].shape[:2]
    in_specs = []
    for g, dil in enumerate(dils):
        spec = pl.BlockSpec((1, 1, dil, s_len // dil, LANES), lambda i, j: (i, j, 0, 0, 0))
        in_specs += [spec] * 3
    n_g = len(dils)
    return pl.pallas_call(
        functools.partial(_attn_kernel, s_len=s_len, dils=tuple(dils)),
        grid=(b, n_hp),
        in_specs=in_specs,
        out_specs=pl.BlockSpec((1, 1, s_len, LANES), lambda i, j: (i, j, 0, 0)),
        out_shape=jax.ShapeDtypeStruct((b, n_hp, s_len, LANES), BF16),
        scratch_shapes=[pltpu.VMEM((n_g, s_len, LANES), F32),
                        pltpu.VMEM((n_g, s_len, LANES), F32)],
        compiler_params=_params("parallel", "parallel"),
        name="attn_core",
    )(*qkvs)


def _layer_norm(z, g, b):
    mu = jnp.mean(z, axis=-1, keepdims=True)
    zc = z - mu
    var = jnp.mean(zc * zc, axis=-1, keepdims=True)
    return zc * lax.rsqrt(var + LN_EPS) * g + b


def _out_ln_kernel(o_ref, w_ref, x_ref, gate_ref, g_ref, b_ref, y_ref, *, alpha):
    n_hp = o_ref.shape[1]
    o = jnp.concatenate([o_ref[0, hp] for hp in range(n_hp)], axis=-1)
    y = jnp.dot(o, w_ref[...], preferred_element_type=F32)
    z = alpha * x_ref[0] + gate_ref[0] * y
    y_ref[0] = _layer_norm(z, g_ref[...], b_ref[...])


def out_proj_ln(o_heads, w_out, x, gate1p, ln_g, ln_b, alpha, tm=512):
    b, s, d = x.shape
    n_hp = o_heads.shape[1]
    return pl.pallas_call(
        functools.partial(_out_ln_kernel, alpha=alpha),
        grid=(b, s // tm),
        in_specs=[
            pl.BlockSpec((1, n_hp, tm, LANES), lambda i, j: (i, 0, j, 0)),
            pl.BlockSpec((d, d), lambda i, j: (0, 0)),
            pl.BlockSpec((1, tm, d), lambda i, j: (i, j, 0)),
            pl.BlockSpec((1, 1, d), lambda i, j: (i, 0, 0)),
            pl.BlockSpec((1, d), lambda i, j: (0, 0)),
            pl.BlockSpec((1, d), lambda i, j: (0, 0)),
        ],
        out_specs=pl.BlockSpec((1, tm, d), lambda i, j: (i, j, 0)),
        out_shape=jax.ShapeDtypeStruct((b, s, d), F32),
        compiler_params=_params("parallel", "parallel"),
        name="out_proj_ln",
    )(o_heads, w_out, x, gate1p, ln_g.reshape(1, d), ln_b.reshape(1, d))


def attention_layer(x, mods, cos_t, sin_t, w_in, w_out, ln_g, ln_b, alpha):
    b, s, d = x.shape
    shift = mods[:, None, 0:d]
    scale = mods[:, None, d:2 * d]
    gate1p = 1.0 + mods[:, None, 2 * d:3 * d]
    w_in = w_in.astype(BF16)
    dils = [dil for _, dil in DILATED_GROUPS]
    qkvs = []
    for g, dil in enumerate(dils):
        qkvs += attn_project(x, shift, scale, cos_t, sin_t,
                             w_in[:, g * 3 * d:(g + 1) * 3 * d], dil)
    o = attn_core(qkvs, dils, s)
    return out_proj_ln(o, w_out.astype(BF16), x, gate1p, ln_g, ln_b, alpha)


def _split_bf16(a):
    hi = a.astype(BF16)
    lo = (a - hi.astype(F32)).astype(BF16)
    return hi, lo


def _gdn_proj_kernel(x_ref, shift_ref, scale_ref, w_ref, wab_hi_ref, wab_lo_ref,
                     wabt_hi_ref, wabt_lo_ref, q_ref, k_ref, v_ref, z_ref, ab_ref, abt_ref, *, tm):
    d = x_ref.shape[-1]
    h = x_ref[0] * (1.0 + scale_ref[0]) + shift_ref[0]
    h_hi, h_lo = _split_bf16(h)
    for j, out_ref in enumerate((q_ref, k_ref, v_ref, z_ref)):
        res = jnp.dot(h_hi, w_ref[:, j * d:(j + 1) * d], preferred_element_type=F32)
        for hd in range(d // LANES):
            out_ref[0, hd] = res[:, hd * LANES:(hd + 1) * LANES].astype(BF16)
    ab = jnp.dot(h_hi, wab_hi_ref[...], preferred_element_type=F32)
    ab += jnp.dot(h_lo, wab_hi_ref[...], preferred_element_type=F32)
    ab += jnp.dot(h_hi, wab_lo_ref[...], preferred_element_type=F32)
    ab_ref[0] = ab
    nt = (((1,), (1,)), ((), ()))
    abt = lax.dot_general(wabt_hi_ref[...], h_hi, nt, preferred_element_type=F32)
    abt += lax.dot_general(wabt_hi_ref[...], h_lo, nt, preferred_element_type=F32)
    abt += lax.dot_general(wabt_lo_ref[...], h_hi, nt, preferred_element_type=F32)
    for n in range(tm // GDN_CHUNK):
        abt_ref[0, n] = abt[:, n * GDN_CHUNK:(n + 1) * GDN_CHUNK]


def gdn_project(x, shift, scale, w_in, tm=512):
    b, s, d = x.shape
    n_h = d // LANES
    w_main = w_in[:, :4 * d].astype(BF16)
    w_ab = jnp.pad(w_in[:, 4 * d:], ((0, 0), (0, LANES - 2 * GDN_HEADS)))
    w_abt = w_in[:, 4 * d:].T
    wab_hi, wab_lo = _split_bf16(w_ab)
    wabt_hi, wabt_lo = _split_bf16(w_abt)
    head_sds = jax.ShapeDtypeStruct((b, n_h, s, LANES), BF16)
    head_spec = pl.BlockSpec((1, n_h, tm, LANES), lambda i, j: (i, 0, j, 0))
    n_ab = 2 * GDN_HEADS
    return pl.pallas_call(
        functools.partial(_gdn_proj_kernel, tm=tm),
        grid=(b, s // tm),
        in_specs=[
            pl.BlockSpec((1, tm, d), lambda i, j: (i, j, 0)),
            pl.BlockSpec((1, 1, d), lambda i, j: (i, 0, 0)),
            pl.BlockSpec((1, 1, d), lambda i, j: (i, 0, 0)),
            pl.BlockSpec((d, 4 * d), lambda i, j: (0, 0)),
            pl.BlockSpec((d, LANES), lambda i, j: (0, 0)),
            pl.BlockSpec((d, LANES), lambda i, j: (0, 0)),
            pl.BlockSpec((n_ab, d), lambda i, j: (0, 0)),
            pl.BlockSpec((n_ab, d), lambda i, j: (0, 0)),
        ],
        out_specs=[head_spec] * 4 + [
            pl.BlockSpec((1, tm, LANES), lambda i, j: (i, j, 0)),
            pl.BlockSpec((1, tm // GDN_CHUNK, n_ab, GDN_CHUNK), lambda i, j: (i, j, 0, 0)),
        ],
        out_shape=[head_sds] * 4 + [
            jax.ShapeDtypeStruct((b, s, LANES), F32),
            jax.ShapeDtypeStruct((b, s // GDN_CHUNK, n_ab, GDN_CHUNK), F32),
        ],
        compiler_params=_params("parallel", "parallel"),
        name="gdn_project",
    )(x, shift, scale, w_main, wab_hi, wab_lo, wabt_hi, wabt_lo)


def _softplus(t):
    return jnp.maximum(t, 0.0) + jnp.log(1.0 + jnp.exp(-jnp.abs(t)))


def _sigmoid(t):
    return 1.0 / (1.0 + jnp.exp(-t))


def _dot3_exact_lhs(lhs_bf16, rhs_f32):
    acc = None
    rem = rhs_f32
    for _ in range(3):
        part = rem.astype(BF16)
        rem = rem - part.astype(F32)
        t = jnp.dot(lhs_bf16, part, preferred_element_type=F32)
        acc = t if acc is None else acc + t
    return acc


def _dot3_exact_rhs(lhs_f32, rhs_bf16):
    acc = None
    rem = lhs_f32
    for _ in range(3):
        part = rem.astype(BF16)
        rem = rem - part.astype(F32)
        t = jnp.dot(part, rhs_bf16, preferred_element_type=F32)
        acc = t if acc is None else acc + t
    return acc


def _gdn_kernel(q_ref, k_ref, v_ref, z_ref, ab_ref, abt_ref, convw_ref, alog_l_ref, dtb_l_ref,
                alog_c_ref, dtb_c_ref, normw_ref, o_ref,
                state, carry, qs, ks, vs, gcol_s, beta_s, *, tc):
    n_h = GDN_HEADS
    ch = GDN_CHUNK
    dh = GDN_HEAD_DIM
    halo = 8

    @pl.when(pl.program_id(1) == 0)
    def _():
        state[...] = jnp.zeros_like(state)
        carry[:, :, 0:halo, :] = jnp.zeros((3, n_h, halo, dh), F32)

    for a, (src, dst) in enumerate(((q_ref, qs), (k_ref, ks), (v_ref, vs))):
        for hd in range(n_h):
            xb = carry.at[a, hd]
            xb[halo:halo + tc, :] = src[0, hd].astype(F32)
            w = convw_ref[a, hd]
            y = xb[halo:halo + tc, :] * w[CONV_WIDTH - 1:CONV_WIDTH]
            for j in range(CONV_WIDTH - 1):
                off = halo - (CONV_WIDTH - 1) + j
                y = y + xb[off:off + tc, :] * w[j:j + 1]
            xb[0:halo, :] = xb[tc:tc + halo, :]
            y = y * _sigmoid(y)
            if a < 2:
                y = y * lax.rsqrt(jnp.sum(y * y, axis=-1, keepdims=True) + RMS_EPS)
                if a == 0:
                    y = y * (dh ** -0.5)
            dst[hd] = y

    ab = ab_ref[0]
    gcol_s[...] = -jnp.exp(alog_l_ref[...]) * _softplus(ab + dtb_l_ref[...])
    beta_s[...] = _sigmoid(ab)

    ri = lax.broadcasted_iota(jnp.int32, (ch, ch), 0)
    ci = lax.broadcasted_iota(jnp.int32, (ch, ch), 1)
    causal = ri >= ci
    strict = ri > ci
    tril = jnp.where(causal, 1.0, 0.0).astype(BF16)
    triu = jnp.where(ri <= ci, 1.0, 0.0).astype(BF16)
    eye = jnp.where(ri == ci, 1.0, 0.0).astype(F32)
    nt = (((1,), (1,)), ((), ()))
    tn = (((0,), (0,)), ((), ()))

    def chunk(cidx, carry_unused):
        rows = pl.ds(pl.multiple_of(cidx * ch, ch), ch)
        gc = _dot3_exact_lhs(tril, gcol_s[rows, :])
        a_t = abt_ref[0, cidx][0:n_h, :]
        g_t = -jnp.exp(alog_c_ref[...]) * _softplus(a_t + dtb_c_ref[...])
        gc_t = _dot3_exact_rhs(g_t, triu)
        beta = beta_s[rows, :]
        heads = range(n_h)
        gcol = [gc[:, hd:hd + 1] for hd in heads]
        bcol = [beta[:, n_h + hd:n_h + hd + 1] for hd in heads]
        decay = [jnp.exp(jnp.where(causal, gcol[hd] - gc_t[hd:hd + 1, :], NEG_BIG))
                 for hd in heads]
        qc = [qs[hd, rows, :] for hd in heads]
        kc = [ks[hd, rows, :] for hd in heads]
        kb = [kc[hd] * bcol[hd] for hd in heads]
        both = [lax.dot_general(jnp.concatenate([kb[hd], qc[hd]], axis=0).astype(BF16),
                                kc[hd].astype(BF16), nt, preferred_element_type=F32)
                for hd in heads]
        lmat = [jnp.where(strict, both[hd][:ch] * decay[hd], 0.0) for hd in heads]
        qk = [(both[hd][ch:] * decay[hd]).astype(BF16) for hd in heads]
        pm = [-lmat[hd] for hd in heads]
        mm = lmat
        for _ in range(5):
            mb = [mm[hd].astype(BF16) for hd in heads]
            mm = [jnp.dot(mb[hd], mb[hd], preferred_element_type=F32) for hd in heads]
            pm = [pm[hd] + mm[hd] + jnp.dot(pm[hd].astype(BF16), mm[hd].astype(BF16),
                                            preferred_element_type=F32) for hd in heads]
        eg = [jnp.exp(gcol[hd]) for hd in heads]
        rhs = [jnp.concatenate([vs[hd, rows, :] * bcol[hd], kb[hd] * eg[hd]], axis=1)
               for hd in heads]
        sol = [rhs[hd] + jnp.dot(pm[hd].astype(BF16), rhs[hd].astype(BF16),
                                 preferred_element_type=F32) for hd in heads]
        glast = [gcol[hd][ch - 1:ch, :] for hd in heads]
        kdec = [(kc[hd] * jnp.exp(glast[hd] - gcol[hd])).astype(BF16) for hd in heads]
        s_prev = [state[hd] for hd in heads]
        wq = [jnp.dot(jnp.concatenate([sol[hd][:, dh:], qc[hd] * eg[hd]], axis=0).astype(BF16),
                      s_prev[hd].astype(BF16), preferred_element_type=F32)
              for hd in heads]
        vb = [(sol[hd][:, :dh] - wq[hd][:ch]).astype(BF16) for hd in heads]
        o = [wq[hd][ch:] + jnp.dot(qk[hd], vb[hd], preferred_element_type=F32) for hd in heads]
        for hd in heads:
            state[hd] = s_prev[hd] * jnp.exp(glast[hd]) + lax.dot_general(
                kdec[hd], vb[hd], tn, preferred_element_type=F32)
        for hd in heads:
            on = o[hd] * lax.rsqrt(jnp.mean(o[hd] * o[hd], axis=-1, keepdims=True)
                                   + RMS_EPS) * normw_ref[...]
            zc = z_ref[0, hd, rows, :].astype(F32)
            o_ref[0, hd, rows, :] = (on * (zc * _sigmoid(zc))).astype(o_ref.dtype)
        return carry_unused

    lax.fori_loop(0, tc // ch, chunk, 0)


def gdn_core(q, k, v, z, ab, abt, conv_w, a_log, dt_bias, norm_w, tc=512):
    b, n_h, s, dh = q.shape
    n_ab = 2 * n_h
    convw = conv_w.reshape(CONV_WIDTH, 3, n_h, dh).transpose(1, 2, 0, 3)
    pad = (0, LANES - n_h)
    alog_l = jnp.pad(a_log.astype(F32), pad).reshape(1, LANES)
    dtb_l = jnp.pad(dt_bias.astype(F32), pad).reshape(1, LANES)
    alog_c = a_log.astype(F32).reshape(n_h, 1)
    dtb_c = dt_bias.astype(F32).reshape(n_h, 1)
    head_spec = pl.BlockSpec((1, n_h, tc, dh), lambda i, j: (i, 0, j, 0))

    def full(shape):
        return pl.BlockSpec(shape, lambda i, j: (0,) * len(shape))

    return pl.pallas_call(
        functools.partial(_gdn_kernel, tc=tc),
        grid=(b, s // tc),
        in_specs=[head_spec] * 4 + [
            pl.BlockSpec((1, tc, LANES), lambda i, j: (i, j, 0)),
            pl.BlockSpec((1, tc // GDN_CHUNK, n_ab, GDN_CHUNK), lambda i, j: (i, j, 0, 0)),
            full((3, n_h, CONV_WIDTH, dh)),
            full((1, LANES)), full((1, LANES)), full((n_h, 1)), full((n_h, 1)), full((1, dh)),
        ],
        out_specs=head_spec,
        out_shape=jax.ShapeDtypeStruct((b, n_h, s, dh), BF16),
        scratch_shapes=[
            pltpu.VMEM((n_h, dh, dh), F32),
            pltpu.VMEM((3, n_h, 8 + tc, dh), F32),
            pltpu.VMEM((n_h, tc, dh), F32),
            pltpu.VMEM((n_h, tc, dh), F32),
            pltpu.VMEM((n_h, tc, dh), F32),
            pltpu.VMEM((tc, LANES), F32),
            pltpu.VMEM((tc, LANES), F32),
        ],
        compiler_params=_params("parallel", "arbitrary"),
        name="gdn_core",
    )(q, k, v, z, ab, abt, convw, alog_l, dtb_l, alog_c, dtb_c, norm_w.reshape(1, dh).astype(F32))


def gdn_layer(x, mods, w_in, conv_w, a_log, dt_bias, norm_w, w_out, ln_g, ln_b, alpha):
    b, s, d = x.shape
    shift = mods[:, None, 0:d]
    scale = mods[:, None, d:2 * d]
    gate1p = 1.0 + mods[:, None, 2 * d:3 * d]
    q, k, v, z, ab, abt = gdn_project(x, shift, scale, w_in)
    o = gdn_core(q, k, v, z, ab, abt, conv_w, a_log, dt_bias, norm_w)
    return out_proj_ln(o, w_out.astype(BF16), x, gate1p, ln_g, ln_b, alpha)


def _first_argmax(vals, idx, n):
    mx = jnp.max(vals, axis=0, keepdims=True)
    first = jnp.min(jnp.where(vals == mx, idx, n), axis=0, keepdims=True)
    return mx, first


def _router_kernel(x_ref, shift_ref, scale_ref, rwt_hi_ref, rwt_lo_ref, bias_ref,
                   eidx_ref, rank_ref, gcol_ref, cnt_ref, cnt_scr, *, tm):
    n_e = N_EXPERTS
    per_g = n_e // N_EXPERT_GROUPS

    @pl.when(pl.program_id(0) == 0)
    def _():
        cnt_scr[...] = jnp.zeros_like(cnt_scr)

    h = x_ref[...] * (1.0 + scale_ref[0]) + shift_ref[0]
    h_hi, h_lo = _split_bf16(h)
    nt = (((1,), (1,)), ((), ()))
    logits = lax.dot_general(rwt_hi_ref[...], h_hi, nt, preferred_element_type=F32)
    logits += lax.dot_general(rwt_hi_ref[...], h_lo, nt, preferred_element_type=F32)
    logits += lax.dot_general(rwt_lo_ref[...], h_hi, nt, preferred_element_type=F32)
    scores = _sigmoid(logits)
    biased = scores + bias_ref[...]

    r8 = lax.broadcasted_iota(jnp.int32, (per_g, tm), 0)
    gscores = []
    for g in range(N_EXPERT_GROUPS):
        blk = biased[g * per_g:(g + 1) * per_g]
        m1, i1 = _first_argmax(blk, r8, per_g)
        m2 = jnp.max(jnp.where(r8 == i1, NEG_BIG, blk), axis=0, keepdims=True)
        gscores.append(m1 + m2)
    vals = jnp.concatenate(gscores, axis=0)
    rg = lax.broadcasted_iota(jnp.int32, (N_EXPERT_GROUPS, tm), 0)
    gsel = jnp.zeros((N_EXPERT_GROUPS, tm), F32)
    for _ in range(TOPK_GROUPS):
        _, idx = _first_argmax(vals, rg, N_EXPERT_GROUPS)
        hit = rg == idx
        gsel = jnp.where(hit, 1.0, gsel)
        vals = jnp.where(hit, NEG_BIG, vals)
    emask = jnp.concatenate(
        [jnp.broadcast_to(gsel[g:g + 1], (per_g, tm)) for g in range(N_EXPERT_GROUPS)], axis=0)
    masked = jnp.where(emask > 0.5, biased, NEG_BIG)

    re = lax.broadcasted_iota(jnp.int32, (n_e, tm), 0)
    hits, eidx, gates = [], [], []
    for _ in range(TOP_K):
        _, idx = _first_argmax(masked, re, n_e)
        hit = re == idx
        gates.append(jnp.sum(jnp.where(hit, scores, 0.0), axis=0, keepdims=True))
        masked = jnp.where(hit, NEG_BIG, masked)
        hits.append(hit)
        eidx.append(idx)
    gates = jnp.concatenate(gates, axis=0)
    gates = gates / jnp.sum(gates, axis=0, keepdims=True) * ROUTED_SCALE
    onehot = jnp.zeros((n_e, tm), F32)
    for hit in hits:
        onehot = jnp.where(hit, 1.0, onehot)

    ti = lax.broadcasted_iota(jnp.int32, (tm, tm), 0)
    tj = lax.broadcasted_iota(jnp.int32, (tm, tm), 1)
    before = jnp.where(ti < tj, 1.0, 0.0).astype(BF16)
    prefix = jnp.dot(onehot.astype(BF16), before, preferred_element_type=F32) + cnt_scr[...]
    ranks = [jnp.sum(jnp.where(hit, prefix, 0.0), axis=0, keepdims=True) for hit in hits]
    cnt_scr[...] = cnt_scr[...] + jnp.sum(onehot, axis=1, keepdims=True)

    eidx_ref[...] = jnp.concatenate(eidx, axis=0)
    rank_ref[...] = jnp.concatenate(ranks, axis=0).astype(jnp.int32)
    cnt_ref[...] = jnp.broadcast_to(cnt_scr[...], cnt_ref.shape)
    gpad = jnp.concatenate([gates, jnp.zeros((LANES - TOP_K, tm), F32)], axis=0)
    gcol_ref[...] = gpad.T


def moe_route(xt, shift, scale, router_w, router_bias, s_len, tm=512):
    t, d = xt.shape
    n_e = router_w.shape[1]
    rwt_hi, rwt_lo = _split_bf16(router_w.T)
    per_b = s_len // tm
    return pl.pallas_call(
        functools.partial(_router_kernel, tm=tm),
        grid=(t // tm,),
        in_specs=[
            pl.BlockSpec((tm, d), lambda i: (i, 0)),
            pl.BlockSpec((1, 1, d), lambda i: (i // per_b, 0, 0)),
            pl.BlockSpec((1, 1, d), lambda i: (i // per_b, 0, 0)),
            pl.BlockSpec((n_e, d), lambda i: (0, 0)),
            pl.BlockSpec((n_e, d), lambda i: (0, 0)),
            pl.BlockSpec((n_e, 1), lambda i: (0, 0)),
        ],
        out_specs=[
            pl.BlockSpec((TOP_K, tm), lambda i: (0, i)),
            pl.BlockSpec((TOP_K, tm), lambda i: (0, i)),
            pl.BlockSpec((tm, LANES), lambda i: (i, 0)),
            pl.BlockSpec((n_e, LANES), lambda i: (0, 0)),
        ],
        out_shape=[
            jax.ShapeDtypeStruct((TOP_K, t), jnp.int32),
            jax.ShapeDtypeStruct((TOP_K, t), jnp.int32),
            jax.ShapeDtypeStruct((t, LANES), F32),
            jax.ShapeDtypeStruct((n_e, LANES), F32),
        ],
        scratch_shapes=[pltpu.VMEM((n_e, 1), F32)],
        compiler_params=_params("arbitrary"),
        name="moe_route",
    )(xt, shift, scale, rwt_hi, rwt_lo, router_bias.reshape(n_e, 1).astype(F32))


def _slots_kernel(cnt_ref, eidx_ref, rank_ref, dest_ref, blk_e_ref, nused_ref, *, block_rows):
    n_e = N_EXPERTS
    cnt = cnt_ref[...].astype(jnp.int32)
    padded = ((cnt + (block_rows - 1)) // block_rows) * block_rows
    ri = lax.broadcasted_iota(jnp.int32, (n_e, n_e), 0)
    ci = lax.broadcasted_iota(jnp.int32, (n_e, n_e), 1)
    tril = jnp.where(ri >= ci, 1.0, 0.0).astype(BF16)
    pend = _dot3_exact_lhs(tril, padded.astype(F32))
    pstart = pend.astype(jnp.int32) - padded
    eidx = eidx_ref[...]
    dest = rank_ref[...]
    for e in range(n_e):
        dest = dest + jnp.where(eidx == e, pstart[e:e + 1, 0:1], 0)
    dest_ref[...] = dest
    nb = blk_e_ref.shape[1]
    first_row = lax.broadcasted_iota(jnp.int32, (n_e, nb), 1) * block_rows
    pend_i = pend.astype(jnp.int32)[:, 0:1]
    owner = jnp.sum(jnp.where(pend_i <= first_row, 1, 0), axis=0, keepdims=True)
    blk_e_ref[...] = jnp.minimum(owner, n_e - 1)
    nused_ref[...] = pend_i[n_e - 1:n_e, :] // block_rows + jnp.zeros(nused_ref.shape, jnp.int32)


def moe_slots(counts, eidx, rank, n_blocks, block_rows, tt=2048):
    t = eidx.shape[1]
    tt = min(tt, t)
    nb_pad = -(-n_blocks // LANES) * LANES
    n_e = counts.shape[0]
    return pl.pallas_call(
        functools.partial(_slots_kernel, block_rows=block_rows),
        grid=(t // tt,),
        in_specs=[
            pl.BlockSpec((n_e, LANES), lambda i: (0, 0)),
            pl.BlockSpec((TOP_K, tt), lambda i: (0, i)),
            pl.BlockSpec((TOP_K, tt), lambda i: (0, i)),
        ],
        out_specs=[
            pl.BlockSpec((TOP_K, tt), lambda i: (0, i)),
            pl.BlockSpec((1, nb_pad), lambda i: (0, 0)),
            pl.BlockSpec((1, LANES), lambda i: (0, 0)),
        ],
        out_shape=[
            jax.ShapeDtypeStruct((TOP_K, t), jnp.int32),
            jax.ShapeDtypeStruct((1, nb_pad), jnp.int32),
            jax.ShapeDtypeStruct((1, LANES), jnp.int32),
        ],
        compiler_params=_params("arbitrary"),
        name="moe_slots",
    )(counts, eidx, rank)


TILE_ROWS = 4
U32 = jnp.uint32


def _pack_bf16_pairs(v):
    half = v.shape[1] // 2
    lo = lax.bitcast_convert_type(v[:, :half].astype(BF16).astype(F32), U32)
    hi = lax.bitcast_convert_type(v[:, half:].astype(BF16).astype(F32), U32)
    return hi | (lo >> 16)


def _unpack_bf16_pairs(u):
    lo = lax.bitcast_convert_type(u << 16, F32)
    hi = lax.bitcast_convert_type(u & U32(0xFFFF0000), F32)
    return jnp.concatenate([lo, hi], axis=1)


def _tile_copy(src, src_tok, dst, dst_tok, sem):
    return pltpu.make_async_copy(
        src.at[pl.ds(pl.multiple_of(src_tok * TILE_ROWS, TILE_ROWS), TILE_ROWS)],
        dst.at[pl.ds(pl.multiple_of(dst_tok * TILE_ROWS, TILE_ROWS), TILE_ROWS)], sem)


def _to_token_tiles(ref, val, n_tok):
    for cidx in range(TILE_ROWS):
        ref[pl.ds(cidx, n_tok, stride=TILE_ROWS), :] = val[:, cidx * LANES:(cidx + 1) * LANES]


def _from_token_tiles(ref, n_tok):
    return jnp.concatenate(
        [ref[pl.ds(cidx, n_tok, stride=TILE_ROWS), :] for cidx in range(TILE_ROWS)], axis=1)


def _dispatch_kernel(dest_ref, x_ref, shift_ref, scale_ref, xs_hbm, hs, sem, *, tm, t_total):
    base = pl.program_id(0) * tm
    _to_token_tiles(hs, _pack_bf16_pairs(x_ref[...] * (1.0 + scale_ref[0]) + shift_ref[0]), tm)

    def issue(t, carry):
        for k in range(TOP_K):
            _tile_copy(hs, t, xs_hbm, dest_ref[k * t_total + base + t], sem).start()
        return carry

    lax.fori_loop(0, tm, issue, 0, unroll=2)
    for k in range(TOP_K):
        pltpu.make_async_copy(hs, xs_hbm.at[pl.ds(0, tm * TILE_ROWS)], sem).wait()


def moe_dispatch(dest_flat, xt, shift, scale, n_rows, s_len, tm=256):
    t, d = xt.shape
    per_b = s_len // tm
    grid_spec = pltpu.PrefetchScalarGridSpec(
        num_scalar_prefetch=1,
        grid=(t // tm,),
        in_specs=[
            pl.BlockSpec((tm, d), lambda i, dest: (i, 0)),
            pl.BlockSpec((1, 1, d), lambda i, dest: (i // per_b, 0, 0)),
            pl.BlockSpec((1, 1, d), lambda i, dest: (i // per_b, 0, 0)),
        ],
        out_specs=pl.BlockSpec(memory_space=pl.ANY),
        scratch_shapes=[pltpu.VMEM((tm * TILE_ROWS, LANES), U32), pltpu.SemaphoreType.DMA],
    )
    assert d == 2 * TILE_ROWS * LANES
    return pl.pallas_call(
        functools.partial(_dispatch_kernel, tm=tm, t_total=t),
        grid_spec=grid_spec,
        out_shape=jax.ShapeDtypeStruct((n_rows * TILE_ROWS, LANES), U32),
        compiler_params=_params("arbitrary"),
        name="moe_dispatch",
    )(dest_flat, xt, shift, scale)


def _expert_kernel(blk_e_ref, nused_ref, xs_ref, w1_ref, w3_ref, w2_ref, ys_ref,
                   w1b, w3b, w2b, *, block_rows):
    i = pl.program_id(0)
    prev = blk_e_ref[jnp.maximum(i - 1, 0)]

    @pl.when((i == 0) | (blk_e_ref[i] != prev))
    def _():
        w1b[...] = w1_ref[0].astype(BF16)
        w3b[...] = w3_ref[0].astype(BF16)
        w2b[...] = w2_ref[0].astype(BF16)

    @pl.when(i < nused_ref[0])
    def _():
        xb = _unpack_bf16_pairs(_from_token_tiles(xs_ref, block_rows)).astype(BF16)
        h1 = jnp.dot(xb, w1b[...], preferred_element_type=F32)
        h3 = jnp.dot(xb, w3b[...], preferred_element_type=F32)
        act = (h1 * _sigmoid(h1)) * h3
        y = jnp.dot(act.astype(BF16), w2b[...], preferred_element_type=F32)
        _to_token_tiles(ys_ref, _pack_bf16_pairs(y), block_rows)


def moe_experts(blk_e, nused, xs, w1, w3, w2, block_rows, first_expert):
    d, de = w1.shape[-2:]
    n_rows = xs.shape[0] // TILE_ROWS
    n_blocks = n_rows // block_rows

    def row_map(i, be, nu):
        return (jnp.minimum(i, nu[0] - 1), 0)

    def w_map(i, be, nu):
        return (be[i] + first_expert, 0, 0)

    grid_spec = pltpu.PrefetchScalarGridSpec(
        num_scalar_prefetch=2,
        grid=(n_blocks,),
        in_specs=[
            pl.BlockSpec((block_rows * TILE_ROWS, LANES), row_map),
            pl.BlockSpec((1, d, de), w_map),
            pl.BlockSpec((1, d, de), w_map),
            pl.BlockSpec((1, de, d), w_map),
        ],
        out_specs=pl.BlockSpec((block_rows * TILE_ROWS, LANES), row_map),
        scratch_shapes=[pltpu.VMEM((d, de), BF16), pltpu.VMEM((d, de), BF16),
                        pltpu.VMEM((de, d), BF16)],
    )
    return pl.pallas_call(
        functools.partial(_expert_kernel, block_rows=block_rows),
        grid_spec=grid_spec,
        out_shape=jax.ShapeDtypeStruct((n_rows * TILE_ROWS, LANES), U32),
        compiler_params=_params("arbitrary"),
        name="moe_experts",
    )(blk_e, nused, xs, w1, w3, w2)


def _combine_kernel(dest_ref, ys_hbm, x_ref, shift_ref, scale_ref, gate_ref, gcol_ref,
                    ws1_ref, ws3_ref, ws2_ref, g_ref, b_ref, o_ref, buf, sem,
                    *, tm, t_total, alpha):
    step = pl.program_id(0)
    slot = step % 2

    def start_gather(for_step, into):
        base = for_step * tm

        def issue(t, carry):
            for k in range(TOP_K):
                _tile_copy(ys_hbm, dest_ref[k * t_total + base + t], buf.at[into, k], t,
                           sem.at[into]).start()
            return carry

        lax.fori_loop(0, tm, issue, 0, unroll=2)

    @pl.when(step == 0)
    def _():
        start_gather(0, 0)

    @pl.when(step + 1 < pl.num_programs(0))
    def _():
        start_gather(step + 1, 1 - slot)

    x = x_ref[...]
    hb = (x * (1.0 + scale_ref[0]) + shift_ref[0]).astype(BF16)
    h1 = jnp.dot(hb, ws1_ref[...], preferred_element_type=F32)
    h3 = jnp.dot(hb, ws3_ref[...], preferred_element_type=F32)
    act = (h1 * _sigmoid(h1)) * h3
    y = jnp.dot(act.astype(BF16), ws2_ref[...], preferred_element_type=F32)

    for k in range(TOP_K):
        pltpu.make_async_copy(ys_hbm.at[pl.ds(0, tm * TILE_ROWS)], buf.at[slot, k],
                              sem.at[slot]).wait()

    gcol = gcol_ref[...]
    for k in range(TOP_K):
        y = y + _unpack_bf16_pairs(_from_token_tiles(buf.at[slot, k], tm)) * gcol[:, k:k + 1]
    z = alpha * x + gate_ref[0] * y
    o_ref[...] = _layer_norm(z, g_ref[...], b_ref[...])


def moe_combine(dest_flat, ys, xt, shift, scale, gate1p, gcol, ws1, ws3, ws2, ln_g, ln_b,
                s_len, alpha, tm=128):
    t, d = xt.shape
    ds_ = ws1.shape[1]
    per_b = s_len // tm

    def const(shape):
        return pl.BlockSpec(shape, lambda i, dest: (0,) * len(shape))

    mod_spec = pl.BlockSpec((1, 1, d), lambda i, dest: (i // per_b, 0, 0))
    grid_spec = pltpu.PrefetchScalarGridSpec(
        num_scalar_prefetch=1,
        grid=(t // tm,),
        in_specs=[
            pl.BlockSpec(memory_space=pl.ANY),
            pl.BlockSpec((tm, d), lambda i, dest: (i, 0)),
            mod_spec, mod_spec, mod_spec,
            pl.BlockSpec((tm, LANES), lambda i, dest: (i, 0)),
            const((d, ds_)), const((d, ds_)), const((ds_, d)),
            const((1, d)), const((1, d)),
        ],
        out_specs=pl.BlockSpec((tm, d), lambda i, dest: (i, 0)),
        scratch_shapes=[pltpu.VMEM((2, TOP_K, tm * TILE_ROWS, LANES), U32),
                        pltpu.SemaphoreType.DMA((2,))],
    )
    return pl.pallas_call(
        functools.partial(_combine_kernel, tm=tm, t_total=t, alpha=alpha),
        grid_spec=grid_spec,
        out_shape=jax.ShapeDtypeStruct((t, d), F32),
        compiler_params=_params("arbitrary"),
        name="moe_combine",
    )(dest_flat, ys, xt, shift, scale, gate1p, gcol, ws1, ws3, ws2,
      ln_g.reshape(1, d), ln_b.reshape(1, d))


def moe_layer(x, mods, router_w, router_bias, w1, w3, w2, ws1, ws3, ws2, ln_g, ln_b, alpha,
              layer):
    b, s, d = x.shape
    t = b * s
    shift = mods[:, None, 0:d]
    scale = mods[:, None, d:2 * d]
    gate1p = 1.0 + mods[:, None, 2 * d:3 * d]
    xt = x.reshape(t, d)
    n_blocks = t * TOP_K // MOE_BLOCK + N_EXPERTS
    eidx, rank, gcol, counts = moe_route(xt, shift, scale, router_w, router_bias, s)
    dest, blk_e, nused = moe_slots(counts, eidx, rank, n_blocks, MOE_BLOCK)
    dest_flat = dest.reshape(-1)
    xs = moe_dispatch(dest_flat, xt, shift, scale, n_blocks * MOE_BLOCK, s)
    n_e = w1.shape[1]
    ys = moe_experts(blk_e.reshape(-1), nused.reshape(-1), xs,
                     w1.reshape((-1,) + w1.shape[2:]), w3.reshape((-1,) + w3.shape[2:]),
                     w2.reshape((-1,) + w2.shape[2:]), MOE_BLOCK, layer * n_e)
    out = moe_combine(dest_flat, ys, xt, shift, scale, gate1p, gcol,
                      ws1.astype(BF16), ws3.astype(BF16), ws2.astype(BF16), ln_g, ln_b, s, alpha)
    return out.reshape(b, s, d)


def kernel(x, c, positions, ada_w, ada_b, ln_g, ln_b, attn_w_in, attn_w_out, gdn_w_in, gdn_conv_w,
           gdn_a_log, gdn_dt_bias, gdn_norm_w, gdn_w_out, router_w, router_bias, expert_w1,
           expert_w3, expert_w2, shared_w1, shared_w3, shared_w2):
    depth = ada_w.shape[0]
    alpha = (2 * depth) ** 0.25
    mods = ada_vectors(c, ada_w, ada_b)
    cos_t, sin_t = rope_tables(positions)
    for i in range(depth):
        j = i // 2
        if i % 2 == 0:
            x = attention_layer(x, mods[2 * i], cos_t, sin_t, attn_w_in[j], attn_w_out[j],
                                ln_g[i, 0], ln_b[i, 0], alpha)
        else:
            x = gdn_layer(x, mods[2 * i], gdn_w_in[j], gdn_conv_w[j], gdn_a_log[j],
                          gdn_dt_bias[j], gdn_norm_w[j], gdn_w_out[j], ln_g[i, 0], ln_b[i, 0],
                          alpha)
        x = moe_layer(x, mods[2 * i + 1], router_w[i], router_bias[i], expert_w1,
                      expert_w3, expert_w2, shared_w1[i], shared_w3[i], shared_w2[i],
                      ln_g[i, 1], ln_b[i, 1], alpha, i)
    return x
```

```python
import functools
import math

import jax
import jax.numpy as jnp
from jax import lax
from jax.experimental import pallas as pl
from jax.experimental.pallas import tpu as pltpu

F32 = jnp.float32
BF16 = jnp.bfloat16

LANES = 128

DILATED_GROUPS = ((128, 1), (512, 4), (2048, 16))
A_HEADS = 16
A_HEAD_DIM = 64
ATTN_BLOCK = 128
ROPE_THETA = 10000.0
GDN_HEADS = 8
GDN_HEAD_DIM = 128
CONV_WIDTH = 4
GDN_CHUNK = 64
N_EXPERTS = 64
TOP_K = 8
N_EXPERT_GROUPS = 8
TOPK_GROUPS = 4
ROUTED_SCALE = 2.5
MOE_BLOCK = 512
LN_EPS = 1e-5
RMS_EPS = 1e-6
NEG_BIG = -1e30

VMEM_LIMIT = 56 * 1024 * 1024


def _params(*sem):
    return pltpu.CompilerParams(dimension_semantics=sem, vmem_limit_bytes=VMEM_LIMIT)


def _ada_kernel(c_ref, w_ref, b_ref, o_ref):
    c = c_ref[...]
    c_hi = c.astype(BF16)
    c_lo = (c - c_hi.astype(F32)).astype(BF16)
    w = w_ref[0]
    w_hi = w.astype(BF16)
    w_lo = (w - w_hi.astype(F32)).astype(BF16)
    acc = jnp.dot(c_hi, w_hi, preferred_element_type=F32)
    acc += jnp.dot(c_hi, w_lo, preferred_element_type=F32)
    acc += jnp.dot(c_lo, w_hi, preferred_element_type=F32)
    o_ref[0] = acc + b_ref[0]


def ada_vectors(c, ada_w, ada_b):
    depth2 = ada_w.shape[0] * ada_w.shape[1]
    b, d = c.shape
    d3 = ada_w.shape[-1]
    tn = d
    w = ada_w.reshape(depth2, d, d3)
    bias = ada_b.reshape(depth2, 1, d3)
    return pl.pallas_call(
        _ada_kernel,
        grid=(depth2, d3 // tn),
        in_specs=[
            pl.BlockSpec((b, d), lambda l, j: (0, 0)),
            pl.BlockSpec((1, d, tn), lambda l, j: (l, 0, j)),
            pl.BlockSpec((1, 1, tn), lambda l, j: (l, 0, j)),
        ],
        out_specs=pl.BlockSpec((1, b, tn), lambda l, j: (l, 0, j)),
        out_shape=jax.ShapeDtypeStruct((depth2, b, d3), F32),
        compiler_params=_params("parallel", "parallel"),
        name="ada_vectors",
    )(c, w, bias)


def _rope_kernel(pos_ref, freq_ref, sign_ref, cos_ref, sin_ref):
    ang = pos_ref[0].astype(F32) * freq_ref[...]
    cos_ref[0] = jnp.cos(ang)
    sin_ref[0] = jnp.sin(ang) * sign_ref[...]


def rope_tables(positions):
    b, s = positions.shape
    half = A_HEAD_DIM // 2
    inv_freq = ROPE_THETA ** (-jnp.arange(half, dtype=F32) / half)
    freq = jnp.tile(inv_freq, LANES // half).reshape(1, LANES)
    sign = jnp.tile(jnp.concatenate([-jnp.ones((half,), F32), jnp.ones((half,), F32)]),
                    LANES // A_HEAD_DIM).reshape(1, LANES)
    ts = min(s, 1024)
    return pl.pallas_call(
        _rope_kernel,
        grid=(b, s // ts),
        in_specs=[
            pl.BlockSpec((1, ts, 1), lambda i, j: (i, j, 0)),
            pl.BlockSpec((1, LANES), lambda i, j: (0, 0)),
            pl.BlockSpec((1, LANES), lambda i, j: (0, 0)),
        ],
        out_specs=[pl.BlockSpec((1, ts, LANES), lambda i, j: (i, j, 0))] * 2,
        out_shape=[jax.ShapeDtypeStruct((b, s, LANES), F32)] * 2,
        compiler_params=_params("parallel", "parallel"),
        name="rope_tables",
    )(positions.reshape(b, s, 1), freq, sign)


def _residue_major(ref, lead, dil, tl):
    if dil == 1:
        return ref[lead] if lead is not None else ref[...]
    parts = []
    for r in range(dil):
        if lead is None:
            parts.append(ref[pl.ds(r, tl, stride=dil), :])
        else:
            parts.append(ref[lead, pl.ds(r, tl, stride=dil), :])
    return jnp.concatenate(parts, axis=0)


def _attn_proj_kernel(x_ref, shift_ref, scale_ref, cos_ref, sin_ref, w_ref,
                      q_ref, k_ref, v_ref, h_scr, *, dil, tm):
    tl = tm // dil
    d = x_ref.shape[-1]
    h = x_ref[0] * (1.0 + scale_ref[0]) + shift_ref[0]
    if dil == 1:
        hb = h.astype(BF16)
    else:
        for cidx in range(d // LANES):
            h_scr[cidx] = h[:, cidx * LANES:(cidx + 1) * LANES]
        hb = jnp.concatenate(
            [_residue_major(h_scr, cidx, dil, tl).astype(BF16) for cidx in range(d // LANES)],
            axis=1)
    cosp = _residue_major(cos_ref, 0, dil, tl)
    sinp = _residue_major(sin_ref, 0, dil, tl)
    lane = lax.broadcasted_iota(jnp.int32, (tm, LANES), 1)
    first_half = (lane % A_HEAD_DIM) < (A_HEAD_DIM // 2)
    qscale = A_HEAD_DIM ** -0.5
    for j, out_ref in enumerate((q_ref, k_ref, v_ref)):
        res = jnp.dot(hb, w_ref[:, j * d:(j + 1) * d], preferred_element_type=F32)
        for hp in range(d // LANES):
            blk = res[:, hp * LANES:(hp + 1) * LANES]
            if j < 2:
                swapped = jnp.where(first_half,
                                    pltpu.roll(blk, LANES - A_HEAD_DIM // 2, 1),
                                    pltpu.roll(blk, A_HEAD_DIM // 2, 1))
                blk = blk * cosp + swapped * sinp
                if j == 0:
                    blk = blk * qscale
            blk = blk.astype(BF16)
            for r in range(dil):
                out_ref[0, hp, r] = blk[r * tl:(r + 1) * tl]


def attn_project(x, shift, scale, cos_t, sin_t, w_g, dil, tm=512):
    b, s, d = x.shape
    tl = tm // dil
    n_hp = d // LANES
    out_sds = jax.ShapeDtypeStruct((b, n_hp, dil, s // dil, LANES), BF16)
    out_spec = pl.BlockSpec((1, n_hp, dil, tl, LANES), lambda i, j: (i, 0, 0, j, 0))
    return pl.pallas_call(
        functools.partial(_attn_proj_kernel, dil=dil, tm=tm),
        grid=(b, s // tm),
        in_specs=[
            pl.BlockSpec((1, tm, d), lambda i, j: (i, j, 0)),
            pl.BlockSpec((1, 1, d), lambda i, j: (i, 0, 0)),
            pl.BlockSpec((1, 1, d), lambda i, j: (i, 0, 0)),
            pl.BlockSpec((1, tm, LANES), lambda i, j: (i, j, 0)),
            pl.BlockSpec((1, tm, LANES), lambda i, j: (i, j, 0)),
            pl.BlockSpec((d, 3 * d), lambda i, j: (0, 0)),
        ],
        out_specs=[out_spec] * 3,
        out_shape=[out_sds] * 3,
        scratch_shapes=[pltpu.VMEM((d // LANES, tm, LANES), F32)],
        compiler_params=_params("parallel", "parallel"),
        name=f"attn_project_d{dil}",
    )(x, shift, scale, cos_t, sin_t, w_g)


def _attn_kernel(*refs, s_len, dils):
    n_g = len(dils)
    qkv = refs[:3 * n_g]
    o_ref = refs[3 * n_g]
    oacc, lacc = refs[3 * n_g + 1:]
    blk = ATTN_BLOCK

    lane = lax.broadcasted_iota(jnp.int32, (blk, LANES), 1)
    head0 = lane < A_HEAD_DIM
    row = lax.broadcasted_iota(jnp.int32, (2 * blk, 2 * blk), 0) % blk
    col = lax.broadcasted_iota(jnp.int32, (2 * blk, 2 * blk), 1)
    bias_rest = jnp.where((col >= row) & (col <= row + blk), 0.0, NEG_BIG).astype(F32)
    bias_first = bias_rest[:, blk:]

    def run_blocks(g, q_ref, k_ref, v_ref, dil, specs):
        ids = range(len(specs))
        qq, kc, vc = [], [], []
        for r, n, first in specs:
            q = q_ref[0, 0, r, pl.ds(pl.multiple_of(n * blk, blk), blk), :]
            zero = jnp.zeros_like(q)
            qq.append(jnp.concatenate([jnp.where(head0, q, zero), jnp.where(head0, zero, q)],
                                      axis=0))
            if first:
                keys = pl.ds(0, blk)
            else:
                keys = pl.ds(pl.multiple_of((n - 1) * blk, blk), 2 * blk)
            kc.append(k_ref[0, 0, r, keys, :])
            vc.append(v_ref[0, 0, r, keys, :])
        sc = [lax.dot_general(qq[i], kc[i], (((1,), (1,)), ((), ())), preferred_element_type=F32)
              + (bias_first if specs[i][2] else bias_rest) for i in ids]
        m = [jnp.max(sc[i], axis=1, keepdims=True) for i in ids]
        p = [jnp.exp(sc[i] - m[i]) for i in ids]
        l = [jnp.sum(p[i], axis=1, keepdims=True) for i in ids]
        pv = [jnp.dot(p[i].astype(BF16), vc[i], preferred_element_type=F32) * (1.0 / l[i])
              for i in ids]
        for i in ids:
            r, n, _ = specs[i]
            lse = m[i] + jnp.log(l[i])
            o_blk = jnp.where(head0, pv[i][:blk], pv[i][blk:])
            lse_blk = jnp.where(head0, jnp.broadcast_to(lse[:blk], (blk, LANES)),
                                jnp.broadcast_to(lse[blk:], (blk, LANES)))
            start = n * (blk * dil) + r
            if dil == 1:
                idx = pl.ds(pl.multiple_of(start, blk), blk)
            else:
                idx = pl.ds(start, blk, stride=dil)
            oacc[g, idx, :] = o_blk
            lacc[g, idx, :] = lse_blk

    for g, dil in enumerate(dils):
        q_ref, k_ref, v_ref = qkv[3 * g:3 * g + 3]
        nb = s_len // dil // blk

        def per_step(rr, carry, g=g, dil=dil, nb=nb, q_ref=q_ref, k_ref=k_ref, v_ref=v_ref):
            if dil == 1:
                res, width = [0], 4
            else:
                res, width = [2 * rr, 2 * rr + 1], 2
            if nb < width:
                run_blocks(g, q_ref, k_ref, v_ref, dil, [(r, 0, True) for r in res])
                return carry
            run_blocks(g, q_ref, k_ref, v_ref, dil,
                       [(r, n, n == 0) for r in res for n in range(width)])

            def per_group(i, c2):
                run_blocks(g, q_ref, k_ref, v_ref, dil,
                           [(r, width * i + n, False) for r in res for n in range(width)])
                return c2

            lax.fori_loop(1, nb // width, per_group, 0)
            return carry

        lax.fori_loop(0, max(dil // 2, 1), per_step, 0)

    tc = 256

    def mix(i, carry):
        rows = pl.ds(pl.multiple_of(i * tc, tc), tc)
        ls = [lacc[g, rows, :] for g in range(n_g)]
        mx = functools.reduce(jnp.maximum, ls)
        es = [jnp.exp(l - mx) for l in ls]
        den = functools.reduce(lambda a, b2: a + b2, es)
        num = es[0] * oacc[0, rows, :]
        for g in range(1, n_g):
            num = num + es[g] * oacc[g, rows, :]
        o_ref[0, 0, rows, :] = (num * (1.0 / den)).astype(o_ref.dtype)
        return carry

    lax.fori_loop(0, s_len // tc, mix, 0)


def attn_core(qkvs, dils, s_len):
    b, n_hp = qkvs[0].shape[:2]
    in_specs = []
    for g, dil in enumerate(dils):
        spec = pl.BlockSpec((1, 1, dil, s_len // dil, LANES), lambda i, j: (i, j, 0, 0, 0))
        in_specs += [spec] * 3
    n_g = len(dils)
    return pl.pallas_call(
        functools.partial(_attn_kernel, s_len=s_len, dils=tuple(dils)),
        grid=(b, n_hp),
        in_specs=in_specs,
        out_specs=pl.BlockSpec((1, 1, s_len, LANES), lambda i, j: (i, j, 0, 0)),
        out_shape=jax.ShapeDtypeStruct((b, n_hp, s_len, LANES), BF16),
        scratch_shapes=[pltpu.VMEM((n_g, s_len, LANES), F32),
                        pltpu.VMEM((n_g, s_len, LANES), F32)],
        compiler_params=_params("parallel", "parallel"),
        name="attn_core",
    )(*qkvs)


def _layer_norm(z, g, b):
    mu = jnp.mean(z, axis=-1, keepdims=True)
    zc = z - mu
    var = jnp.mean(zc * zc, axis=-1, keepdims=True)
    return zc * lax.rsqrt(var + LN_EPS) * g + b


def _out_ln_kernel(o_ref, w_ref, x_ref, gate_ref, g_ref, b_ref, y_ref, *, alpha):
    n_hp = o_ref.shape[1]
    o = jnp.concatenate([o_ref[0, hp] for hp in range(n_hp)], axis=-1)
    y = jnp.dot(o, w_ref[...], preferred_element_type=F32)
    z = alpha * x_ref[0] + gate_ref[0] * y
    y_ref[0] = _layer_norm(z, g_ref[...], b_ref[...])


def out_proj_ln(o_heads, w_out, x, gate1p, ln_g, ln_b, alpha, tm=512):
    b, s, d = x.shape
    n_hp = o_heads.shape[1]
    return pl.pallas_call(
        functools.partial(_out_ln_kernel, alpha=alpha),
        grid=(b, s // tm),
        in_specs=[
            pl.BlockSpec((1, n_hp, tm, LANES), lambda i, j: (i, 0, j, 0)),
            pl.BlockSpec((d, d), lambda i, j: (0, 0)),
            pl.BlockSpec((1, tm, d), lambda i, j: (i, j, 0)),
            pl.BlockSpec((1, 1, d), lambda i, j: (i, 0, 0)),
            pl.BlockSpec((1, d), lambda i, j: (0, 0)),
            pl.BlockSpec((1, d), lambda i, j: (0, 0)),
        ],
        out_specs=pl.BlockSpec((1, tm, d), lambda i, j: (i, j, 0)),
        out_shape=jax.ShapeDtypeStruct((b, s, d), F32),
        compiler_params=_params("parallel", "parallel"),
        name="out_proj_ln",
    )(o_heads, w_out, x, gate1p, ln_g.reshape(1, d), ln_b.reshape(1, d))


def attention_layer(x, mods, cos_t, sin_t, w_in, w_out, ln_g, ln_b, alpha):
    b, s, d = x.shape
    shift = mods[:, None, 0:d]
    scale = mods[:, None, d:2 * d]
    gate1p = 1.0 + mods[:, None, 2 * d:3 * d]
    w_in = w_in.astype(BF16)
    dils = [dil for _, dil in DILATED_GROUPS]
    qkvs = []
    for g, dil in enumerate(dils):
        qkvs += attn_project(x, shift, scale, cos_t, sin_t,
                             w_in[:, g * 3 * d:(g + 1) * 3 * d], dil)
    o = attn_core(qkvs, dils, s)
    return out_proj_ln(o, w_out.astype(BF16), x, gate1p, ln_g, ln_b, alpha)


def _split_bf16(a):
    hi = a.astype(BF16)
    lo = (a - hi.astype(F32)).astype(BF16)
    return hi, lo


def _gdn_proj_kernel(x_ref, shift_ref, scale_ref, w_ref, wab_hi_ref, wab_lo_ref,
                     wabt_hi_ref, wabt_lo_ref, q_ref, k_ref, v_ref, z_ref, ab_ref, abt_ref, *, tm):
    d = x_ref.shape[-1]
    h = x_ref[0] * (1.0 + scale_ref[0]) + shift_ref[0]
    h_hi, h_lo = _split_bf16(h)
    for j, out_ref in enumerate((q_ref, k_ref, v_ref, z_ref)):
        res = jnp.dot(h_hi, w_ref[:, j * d:(j + 1) * d], preferred_element_type=F32)
        for hd in range(d // LANES):
            out_ref[0, hd] = res[:, hd * LANES:(hd + 1) * LANES].astype(BF16)
    ab = jnp.dot(h_hi, wab_hi_ref[...], preferred_element_type=F32)
    ab += jnp.dot(h_lo, wab_hi_ref[...], preferred_element_type=F32)
    ab += jnp.dot(h_hi, wab_lo_ref[...], preferred_element_type=F32)
    ab_ref[0] = ab
    nt = (((1,), (1,)), ((), ()))
    abt = lax.dot_general(wabt_hi_ref[...], h_hi, nt, preferred_element_type=F32)
    abt += lax.dot_general(wabt_hi_ref[...], h_lo, nt, preferred_element_type=F32)
    abt += lax.dot_general(wabt_lo_ref[...], h_hi, nt, preferred_element_type=F32)
    for n in range(tm // GDN_CHUNK):
        abt_ref[0, n] = abt[:, n * GDN_CHUNK:(n + 1) * GDN_CHUNK]


def gdn_project(x, shift, scale, w_in, tm=512):
    b, s, d = x.shape
    n_h = d // LANES
    w_main = w_in[:, :4 * d].astype(BF16)
    w_ab = jnp.pad(w_in[:, 4 * d:], ((0, 0), (0, LANES - 2 * GDN_HEADS)))
    w_abt = w_in[:, 4 * d:].T
    wab_hi, wab_lo = _split_bf16(w_ab)
    wabt_hi, wabt_lo = _split_bf16(w_abt)
    head_sds = jax.ShapeDtypeStruct((b, n_h, s, LANES), BF16)
    head_spec = pl.BlockSpec((1, n_h, tm, LANES), lambda i, j: (i, 0, j, 0))
    n_ab = 2 * GDN_HEADS
    return pl.pallas_call(
        functools.partial(_gdn_proj_kernel, tm=tm),
        grid=(b, s // tm),
        in_specs=[
            pl.BlockSpec((1, tm, d), lambda i, j: (i, j, 0)),
            pl.BlockSpec((1, 1, d), lambda i, j: (i, 0, 0)),
            pl.BlockSpec((1, 1, d), lambda i, j: (i, 0, 0)),
            pl.BlockSpec((d, 4 * d), lambda i, j: (0, 0)),
            pl.BlockSpec((d, LANES), lambda i, j: (0, 0)),
            pl.BlockSpec((d, LANES), lambda i, j: (0, 0)),
            pl.BlockSpec((n_ab, d), lambda i, j: (0, 0)),
            pl.BlockSpec((n_ab, d), lambda i, j: (0, 0)),
        ],
        out_specs=[head_spec] * 4 + [
            pl.BlockSpec((1, tm, LANES), lambda i, j: (i, j, 0)),
            pl.BlockSpec((1, tm // GDN_CHUNK, n_ab, GDN_CHUNK), lambda i, j: (i, j, 0, 0)),
        ],
        out_shape=[head_sds] * 4 + [
            jax.ShapeDtypeStruct((b, s, LANES), F32),
            jax.ShapeDtypeStruct((b, s // GDN_CHUNK, n_ab, GDN_CHUNK), F32),
        ],
        compiler_params=_params("parallel", "parallel"),
        name="gdn_project",
    )(x, shift, scale, w_main, wab_hi, wab_lo, wabt_hi, wabt_lo)


def _softplus(t):
    return jnp.maximum(t, 0.0) + jnp.log(1.0 + jnp.exp(-jnp.abs(t)))


def _sigmoid(t):
    return 1.0 / (1.0 + jnp.exp(-t))


def _dot3_exact_lhs(lhs_bf16, rhs_f32):
    acc = None
    rem = rhs_f32
    for _ in range(3):
        part = rem.astype(BF16)
        rem = rem - part.astype(F32)
        t = jnp.dot(lhs_bf16, part, preferred_element_type=F32)
        acc = t if acc is None else acc + t
    return acc


def _dot3_exact_rhs(lhs_f32, rhs_bf16):
    acc = None
    rem = lhs_f32
    for _ in range(3):
        part = rem.astype(BF16)
        rem = rem - part.astype(F32)
        t = jnp.dot(part, rhs_bf16, preferred_element_type=F32)
        acc = t if acc is None else acc + t
    return acc


def _gdn_kernel(q_ref, k_ref, v_ref, z_ref, ab_ref, abt_ref, convw_ref, alog_l_ref, dtb_l_ref,
                alog_c_ref, dtb_c_ref, normw_ref, o_ref,
                state, carry, qs, ks, vs, gcol_s, beta_s, *, tc):
    n_h = GDN_HEADS
    ch = GDN_CHUNK
    dh = GDN_HEAD_DIM
    halo = 8

    @pl.when(pl.program_id(1) == 0)
    def _():
        state[...] = jnp.zeros_like(state)
        carry[:, :, 0:halo, :] = jnp.zeros((3, n_h, halo, dh), F32)

    for a, (src, dst) in enumerate(((q_ref, qs), (k_ref, ks), (v_ref, vs))):
        for hd in range(n_h):
            xb = carry.at[a, hd]
            xb[halo:halo + tc, :] = src[0, hd].astype(F32)
            w = convw_ref[a, hd]
            y = xb[halo:halo + tc, :] * w[CONV_WIDTH - 1:CONV_WIDTH]
            for j in range(CONV_WIDTH - 1):
                off = halo - (CONV_WIDTH - 1) + j
                y = y + xb[off:off + tc, :] * w[j:j + 1]
            xb[0:halo, :] = xb[tc:tc + halo, :]
            y = y * _sigmoid(y)
            if a < 2:
                y = y * lax.rsqrt(jnp.sum(y * y, axis=-1, keepdims=True) + RMS_EPS)
                if a == 0:
                    y = y * (dh ** -0.5)
            dst[hd] = y

    ab = ab_ref[0]
    gcol_s[...] = -jnp.exp(alog_l_ref[...]) * _softplus(ab + dtb_l_ref[...])
    beta_s[...] = _sigmoid(ab)

    ri = lax.broadcasted_iota(jnp.int32, (ch, ch), 0)
    ci = lax.broadcasted_iota(jnp.int32, (ch, ch), 1)
    causal = ri >= ci
    strict = ri > ci
    tril = jnp.where(causal, 1.0, 0.0).astype(BF16)
    triu = jnp.where(ri <= ci, 1.0, 0.0).astype(BF16)
    eye = jnp.where(ri == ci, 1.0, 0.0).astype(F32)
    nt = (((1,), (1,)), ((), ()))
    tn = (((0,), (0,)), ((), ()))

    n_par = 4 if (tc // ch) % 4 == 0 else 1
    heads = range(n_h)

    def chunk_group(gidx, carry_unused):
        cids = [gidx * n_par + j for j in range(n_par)]
        rows = [pl.ds(pl.multiple_of(c * ch, ch), ch) for c in cids]
        gc = [_dot3_exact_lhs(tril, gcol_s[r, :]) for r in rows]
        gc_t = []
        for c in cids:
            a_t = abt_ref[0, c][0:n_h, :]
            g_t = -jnp.exp(alog_c_ref[...]) * _softplus(a_t + dtb_c_ref[...])
            gc_t.append(_dot3_exact_rhs(g_t, triu))
        beta = [beta_s[r, :] for r in rows]
        items = [(j, hd) for j in range(n_par) for hd in heads]
        ids = range(len(items))
        gcol = [gc[j][:, hd:hd + 1] for j, hd in items]
        bcol = [beta[j][:, n_h + hd:n_h + hd + 1] for j, hd in items]
        decay = [jnp.exp(jnp.where(causal, gcol[i] - gc_t[j][hd:hd + 1, :], NEG_BIG))
                 for i, (j, hd) in enumerate(items)]
        qc = [qs[hd, rows[j], :] for j, hd in items]
        kc = [ks[hd, rows[j], :] for j, hd in items]
        kb = [kc[i] * bcol[i] for i in ids]
        both = [lax.dot_general(jnp.concatenate([kb[i], qc[i]], axis=0).astype(BF16),
                                kc[i].astype(BF16), nt, preferred_element_type=F32)
                for i in ids]
        lmat = [jnp.where(strict, both[i][:ch] * decay[i], 0.0) for i in ids]
        qk = [(both[i][ch:] * decay[i]).astype(BF16) for i in ids]
        pm = [-lmat[i] for i in ids]
        mm = lmat
        for _ in range(5):
            mb = [mm[i].astype(BF16) for i in ids]
            mm = [jnp.dot(mb[i], mb[i], preferred_element_type=F32) for i in ids]
            pm = [pm[i] + mm[i] + jnp.dot(pm[i].astype(BF16), mm[i].astype(BF16),
                                          preferred_element_type=F32) for i in ids]
        eg = [jnp.exp(gcol[i]) for i in ids]
        rhs = [jnp.concatenate([vs[hd, rows[j], :] * bcol[i], kb[i] * eg[i]], axis=1)
               for i, (j, hd) in enumerate(items)]
        sol = [rhs[i] + jnp.dot(pm[i].astype(BF16), rhs[i].astype(BF16),
                                preferred_element_type=F32) for i in ids]
        glast = [gcol[i][ch - 1:ch, :] for i in ids]
        kdec = [(kc[i] * jnp.exp(glast[i] - gcol[i])).astype(BF16) for i in ids]
        wqg = [jnp.concatenate([sol[i][:, dh:], qc[i] * eg[i]], axis=0).astype(BF16)
               for i in ids]
        for j in range(n_par):
            sel = [i for i in ids if items[i][0] == j]
            s_prev = [state[hd] for hd in heads]
            wq = [jnp.dot(wqg[i], s_prev[hd].astype(BF16), preferred_element_type=F32)
                  for hd, i in enumerate(sel)]
            vb = [(sol[i][:, :dh] - wq[hd][:ch]).astype(BF16) for hd, i in enumerate(sel)]
            o = [wq[hd][ch:] + jnp.dot(qk[i], vb[hd], preferred_element_type=F32)
                 for hd, i in enumerate(sel)]
            for hd, i in enumerate(sel):
                state[hd] = s_prev[hd] * jnp.exp(glast[i]) + lax.dot_general(
                    kdec[i], vb[hd], tn, preferred_element_type=F32)
            for hd in heads:
                on = o[hd] * lax.rsqrt(jnp.mean(o[hd] * o[hd], axis=-1, keepdims=True)
                                       + RMS_EPS) * normw_ref[...]
                zc = z_ref[0, hd, rows[j], :].astype(F32)
                o_ref[0, hd, rows[j], :] = (on * (zc * _sigmoid(zc))).astype(o_ref.dtype)
        return carry_unused

    lax.fori_loop(0, tc // ch // n_par, chunk_group, 0)


def gdn_core(q, k, v, z, ab, abt, conv_w, a_log, dt_bias, norm_w, tc=512):
    b, n_h, s, dh = q.shape
    n_ab = 2 * n_h
    convw = conv_w.reshape(CONV_WIDTH, 3, n_h, dh).transpose(1, 2, 0, 3)
    pad = (0, LANES - n_h)
    alog_l = jnp.pad(a_log.astype(F32), pad).reshape(1, LANES)
    dtb_l = jnp.pad(dt_bias.astype(F32), pad).reshape(1, LANES)
    alog_c = a_log.astype(F32).reshape(n_h, 1)
    dtb_c = dt_bias.astype(F32).reshape(n_h, 1)
    head_spec = pl.BlockSpec((1, n_h, tc, dh), lambda i, j: (i, 0, j, 0))

    def full(shape):
        return pl.BlockSpec(shape, lambda i, j: (0,) * len(shape))

    return pl.pallas_call(
        functools.partial(_gdn_kernel, tc=tc),
        grid=(b, s // tc),
        in_specs=[head_spec] * 4 + [
            pl.BlockSpec((1, tc, LANES), lambda i, j: (i, j, 0)),
            pl.BlockSpec((1, tc // GDN_CHUNK, n_ab, GDN_CHUNK), lambda i, j: (i, j, 0, 0)),
            full((3, n_h, CONV_WIDTH, dh)),
            full((1, LANES)), full((1, LANES)), full((n_h, 1)), full((n_h, 1)), full((1, dh)),
        ],
        out_specs=head_spec,
        out_shape=jax.ShapeDtypeStruct((b, n_h, s, dh), BF16),
        scratch_shapes=[
            pltpu.VMEM((n_h, dh, dh), F32),
            pltpu.VMEM((3, n_h, 8 + tc, dh), F32),
            pltpu.VMEM((n_h, tc, dh), F32),
            pltpu.VMEM((n_h, tc, dh), F32),
            pltpu.VMEM((n_h, tc, dh), F32),
            pltpu.VMEM((tc, LANES), F32),
            pltpu.VMEM((tc, LANES), F32),
        ],
        compiler_params=_params("parallel", "arbitrary"),
        name="gdn_core",
    )(q, k, v, z, ab, abt, convw, alog_l, dtb_l, alog_c, dtb_c, norm_w.reshape(1, dh).astype(F32))


def gdn_layer(x, mods, w_in, conv_w, a_log, dt_bias, norm_w, w_out, ln_g, ln_b, alpha):
    b, s, d = x.shape
    shift = mods[:, None, 0:d]
    scale = mods[:, None, d:2 * d]
    gate1p = 1.0 + mods[:, None, 2 * d:3 * d]
    q, k, v, z, ab, abt = gdn_project(x, shift, scale, w_in)
    o = gdn_core(q, k, v, z, ab, abt, conv_w, a_log, dt_bias, norm_w)
    return out_proj_ln(o, w_out.astype(BF16), x, gate1p, ln_g, ln_b, alpha)


def _first_argmax(vals, idx, n):
    mx = jnp.max(vals, axis=0, keepdims=True)
    first = jnp.min(jnp.where(vals == mx, idx, n), axis=0, keepdims=True)
    return mx, first


def _router_kernel(x_ref, shift_ref, scale_ref, rwt_hi_ref, rwt_lo_ref, bias_ref,
                   eidx_ref, rank_ref, gcol_ref, cnt_ref, cnt_scr, *, tm):
    n_e = N_EXPERTS
    per_g = n_e // N_EXPERT_GROUPS

    @pl.when(pl.program_id(0) == 0)
    def _():
        cnt_scr[...] = jnp.zeros_like(cnt_scr)

    h = x_ref[...] * (1.0 + scale_ref[0]) + shift_ref[0]
    h_hi, h_lo = _split_bf16(h)
    nt = (((1,), (1,)), ((), ()))
    logits = lax.dot_general(rwt_hi_ref[...], h_hi, nt, preferred_element_type=F32)
    logits += lax.dot_general(rwt_hi_ref[...], h_lo, nt, preferred_element_type=F32)
    logits += lax.dot_general(rwt_lo_ref[...], h_hi, nt, preferred_element_type=F32)
    scores = _sigmoid(logits)
    biased = scores + bias_ref[...]

    r8 = lax.broadcasted_iota(jnp.int32, (per_g, tm), 0)
    gscores = []
    for g in range(N_EXPERT_GROUPS):
        blk = biased[g * per_g:(g + 1) * per_g]
        m1, i1 = _first_argmax(blk, r8, per_g)
        m2 = jnp.max(jnp.where(r8 == i1, NEG_BIG, blk), axis=0, keepdims=True)
        gscores.append(m1 + m2)
    vals = jnp.concatenate(gscores, axis=0)
    rg = lax.broadcasted_iota(jnp.int32, (N_EXPERT_GROUPS, tm), 0)
    gsel = jnp.zeros((N_EXPERT_GROUPS, tm), F32)
    for _ in range(TOPK_GROUPS):
        _, idx = _first_argmax(vals, rg, N_EXPERT_GROUPS)
        hit = rg == idx
        gsel = jnp.where(hit, 1.0, gsel)
        vals = jnp.where(hit, NEG_BIG, vals)
    emask = jnp.concatenate(
        [jnp.broadcast_to(gsel[g:g + 1], (per_g, tm)) for g in range(N_EXPERT_GROUPS)], axis=0)
    masked = jnp.where(emask > 0.5, biased, NEG_BIG)

    re = lax.broadcasted_iota(jnp.int32, (n_e, tm), 0)
    hits, eidx, gates = [], [], []
    for _ in range(TOP_K):
        _, idx = _first_argmax(masked, re, n_e)
        hit = re == idx
        gates.append(jnp.sum(jnp.where(hit, scores, 0.0), axis=0, keepdims=True))
        masked = jnp.where(hit, NEG_BIG, masked)
        hits.append(hit)
        eidx.append(idx)
    gates = jnp.concatenate(gates, axis=0)
    gates = gates / jnp.sum(gates, axis=0, keepdims=True) * ROUTED_SCALE
    onehot = jnp.zeros((n_e, tm), F32)
    for hit in hits:
        onehot = jnp.where(hit, 1.0, onehot)

    ti = lax.broadcasted_iota(jnp.int32, (tm, tm), 0)
    tj = lax.broadcasted_iota(jnp.int32, (tm, tm), 1)
    before = jnp.where(ti < tj, 1.0, 0.0).astype(BF16)
    prefix = jnp.dot(onehot.astype(BF16), before, preferred_element_type=F32) + cnt_scr[...]
    ranks = [jnp.sum(jnp.where(hit, prefix, 0.0), axis=0, keepdims=True) for hit in hits]
    cnt_scr[...] = cnt_scr[...] + jnp.sum(onehot, axis=1, keepdims=True)

    eidx_ref[...] = jnp.concatenate(eidx, axis=0)
    rank_ref[...] = jnp.concatenate(ranks, axis=0).astype(jnp.int32)
    cnt_ref[...] = jnp.broadcast_to(cnt_scr[...], cnt_ref.shape)
    gpad = jnp.concatenate([gates, jnp.zeros((LANES - TOP_K, tm), F32)], axis=0)
    gcol_ref[...] = gpad.T


def moe_route(xt, shift, scale, router_w, router_bias, s_len, tm=512):
    t, d = xt.shape
    n_e = router_w.shape[1]
    rwt_hi, rwt_lo = _split_bf16(router_w.T)
    per_b = s_len // tm
    return pl.pallas_call(
        functools.partial(_router_kernel, tm=tm),
        grid=(t // tm,),
        in_specs=[
            pl.BlockSpec((tm, d), lambda i: (i, 0)),
            pl.BlockSpec((1, 1, d), lambda i: (i // per_b, 0, 0)),
            pl.BlockSpec((1, 1, d), lambda i: (i // per_b, 0, 0)),
            pl.BlockSpec((n_e, d), lambda i: (0, 0)),
            pl.BlockSpec((n_e, d), lambda i: (0, 0)),
            pl.BlockSpec((n_e, 1), lambda i: (0, 0)),
        ],
        out_specs=[
            pl.BlockSpec((TOP_K, tm), lambda i: (0, i)),
            pl.BlockSpec((TOP_K, tm), lambda i: (0, i)),
            pl.BlockSpec((tm, LANES), lambda i: (i, 0)),
            pl.BlockSpec((n_e, LANES), lambda i: (0, 0)),
        ],
        out_shape=[
            jax.ShapeDtypeStruct((TOP_K, t), jnp.int32),
            jax.ShapeDtypeStruct((TOP_K, t), jnp.int32),
            jax.ShapeDtypeStruct((t, LANES), F32),
            jax.ShapeDtypeStruct((n_e, LANES), F32),
        ],
        scratch_shapes=[pltpu.VMEM((n_e, 1), F32)],
        compiler_params=_params("arbitrary"),
        name="moe_route",
    )(xt, shift, scale, rwt_hi, rwt_lo, router_bias.reshape(n_e, 1).astype(F32))


def _slots_kernel(cnt_ref, eidx_ref, rank_ref, dest_ref, blk_e_ref, nused_ref, *, block_rows):
    n_e = N_EXPERTS
    cnt = cnt_ref[...].astype(jnp.int32)
    padded = ((cnt + (block_rows - 1)) // block_rows) * block_rows
    ri = lax.broadcasted_iota(jnp.int32, (n_e, n_e), 0)
    ci = lax.broadcasted_iota(jnp.int32, (n_e, n_e), 1)
    tril = jnp.where(ri >= ci, 1.0, 0.0).astype(BF16)
    pend = _dot3_exact_lhs(tril, padded.astype(F32))
    pstart = pend.astype(jnp.int32) - padded
    eidx = eidx_ref[...]
    dest = rank_ref[...]
    for e in range(n_e):
        dest = dest + jnp.where(eidx == e, pstart[e:e + 1, 0:1], 0)
    dest_ref[...] = dest
    nb = blk_e_ref.shape[1]
    first_row = lax.broadcasted_iota(jnp.int32, (n_e, nb), 1) * block_rows
    pend_i = pend.astype(jnp.int32)[:, 0:1]
    owner = jnp.sum(jnp.where(pend_i <= first_row, 1, 0), axis=0, keepdims=True)
    blk_e_ref[...] = jnp.minimum(owner, n_e - 1)
    nused_ref[...] = pend_i[n_e - 1:n_e, :] // block_rows + jnp.zeros(nused_ref.shape, jnp.int32)


def moe_slots(counts, eidx, rank, n_blocks, block_rows, tt=2048):
    t = eidx.shape[1]
    tt = min(tt, t)
    nb_pad = -(-n_blocks // LANES) * LANES
    n_e = counts.shape[0]
    return pl.pallas_call(
        functools.partial(_slots_kernel, block_rows=block_rows),
        grid=(t // tt,),
        in_specs=[
            pl.BlockSpec((n_e, LANES), lambda i: (0, 0)),
            pl.BlockSpec((TOP_K, tt), lambda i: (0, i)),
            pl.BlockSpec((TOP_K, tt), lambda i: (0, i)),
        ],
        out_specs=[
            pl.BlockSpec((TOP_K, tt), lambda i: (0, i)),
            pl.BlockSpec((1, nb_pad), lambda i: (0, 0)),
            pl.BlockSpec((1, LANES), lambda i: (0, 0)),
        ],
        out_shape=[
            jax.ShapeDtypeStruct((TOP_K, t), jnp.int32),
            jax.ShapeDtypeStruct((1, nb_pad), jnp.int32),
            jax.ShapeDtypeStruct((1, LANES), jnp.int32),
        ],
        compiler_params=_params("arbitrary"),
        name="moe_slots",
    )(counts, eidx, rank)


TILE_ROWS = 4
U32 = jnp.uint32


def _pack_bf16_pairs(v):
    half = v.shape[1] // 2
    lo = lax.bitcast_convert_type(v[:, :half].astype(BF16).astype(F32), U32)
    hi = lax.bitcast_convert_type(v[:, half:].astype(BF16).astype(F32), U32)
    return hi | (lo >> 16)


def _unpack_bf16_pairs(u):
    lo = lax.bitcast_convert_type(u << 16, F32)
    hi = lax.bitcast_convert_type(u & U32(0xFFFF0000), F32)
    return jnp.concatenate([lo, hi], axis=1)


def _tile_copy(src, src_tok, dst, dst_tok, sem):
    return pltpu.make_async_copy(
        src.at[pl.ds(pl.multiple_of(src_tok * TILE_ROWS, TILE_ROWS), TILE_ROWS)],
        dst.at[pl.ds(pl.multiple_of(dst_tok * TILE_ROWS, TILE_ROWS), TILE_ROWS)], sem)


def _to_token_tiles(ref, val, n_tok):
    for cidx in range(TILE_ROWS):
        ref[pl.ds(cidx, n_tok, stride=TILE_ROWS), :] = val[:, cidx * LANES:(cidx + 1) * LANES]


def _from_token_tiles(ref, n_tok):
    return jnp.concatenate(
        [ref[pl.ds(cidx, n_tok, stride=TILE_ROWS), :] for cidx in range(TILE_ROWS)], axis=1)


def _dispatch_kernel(dest_ref, x_ref, shift_ref, scale_ref, xs_hbm, hs, sem, *, tm, t_total):
    base = pl.program_id(0) * tm
    _to_token_tiles(hs, _pack_bf16_pairs(x_ref[...] * (1.0 + scale_ref[0]) + shift_ref[0]), tm)

    def issue(t, carry):
        for k in range(TOP_K):
            _tile_copy(hs, t, xs_hbm, dest_ref[k * t_total + base + t], sem).start()
        return carry

    lax.fori_loop(0, tm, issue, 0, unroll=2)
    for k in range(TOP_K):
        pltpu.make_async_copy(hs, xs_hbm.at[pl.ds(0, tm * TILE_ROWS)], sem).wait()


def moe_dispatch(dest_flat, xt, shift, scale, n_rows, s_len, tm=256):
    t, d = xt.shape
    per_b = s_len // tm
    grid_spec = pltpu.PrefetchScalarGridSpec(
        num_scalar_prefetch=1,
        grid=(t // tm,),
        in_specs=[
            pl.BlockSpec((tm, d), lambda i, dest: (i, 0)),
            pl.BlockSpec((1, 1, d), lambda i, dest: (i // per_b, 0, 0)),
            pl.BlockSpec((1, 1, d), lambda i, dest: (i // per_b, 0, 0)),
        ],
        out_specs=pl.BlockSpec(memory_space=pl.ANY),
        scratch_shapes=[pltpu.VMEM((tm * TILE_ROWS, LANES), U32), pltpu.SemaphoreType.DMA],
    )
    assert d == 2 * TILE_ROWS * LANES
    return pl.pallas_call(
        functools.partial(_dispatch_kernel, tm=tm, t_total=t),
        grid_spec=grid_spec,
        out_shape=jax.ShapeDtypeStruct((n_rows * TILE_ROWS, LANES), U32),
        compiler_params=_params("arbitrary"),
        name="moe_dispatch",
    )(dest_flat, xt, shift, scale)


def _expert_kernel(blk_e_ref, nused_ref, xs_ref, w1_ref, w3_ref, w2_ref, ys_ref,
                   w1b, w3b, w2b, *, block_rows):
    i = pl.program_id(0)
    prev = blk_e_ref[jnp.maximum(i - 1, 0)]

    @pl.when((i == 0) | (blk_e_ref[i] != prev))
    def _():
        w1b[...] = w1_ref[0].astype(BF16)
        w3b[...] = w3_ref[0].astype(BF16)
        w2b[...] = w2_ref[0].astype(BF16)

    @pl.when(i < nused_ref[0])
    def _():
        xb = _unpack_bf16_pairs(_from_token_tiles(xs_ref, block_rows)).astype(BF16)
        h1 = jnp.dot(xb, w1b[...], preferred_element_type=F32)
        h3 = jnp.dot(xb, w3b[...], preferred_element_type=F32)
        act = (h1 * _sigmoid(h1)) * h3
        y = jnp.dot(act.astype(BF16), w2b[...], preferred_element_type=F32)
        _to_token_tiles(ys_ref, _pack_bf16_pairs(y), block_rows)


def moe_experts(blk_e, nused, xs, w1, w3, w2, block_rows, first_expert):
    d, de = w1.shape[-2:]
    n_rows = xs.shape[0] // TILE_ROWS
    n_blocks = n_rows // block_rows

    def row_map(i, be, nu):
        return (jnp.minimum(i, nu[0] - 1), 0)

    def w_map(i, be, nu):
        return (be[i] + first_expert, 0, 0)

    grid_spec = pltpu.PrefetchScalarGridSpec(
        num_scalar_prefetch=2,
        grid=(n_blocks,),
        in_specs=[
            pl.BlockSpec((block_rows * TILE_ROWS, LANES), row_map),
            pl.BlockSpec((1, d, de), w_map),
            pl.BlockSpec((1, d, de), w_map),
            pl.BlockSpec((1, de, d), w_map),
        ],
        out_specs=pl.BlockSpec((block_rows * TILE_ROWS, LANES), row_map),
        scratch_shapes=[pltpu.VMEM((d, de), BF16), pltpu.VMEM((d, de), BF16),
                        pltpu.VMEM((de, d), BF16)],
    )
    return pl.pallas_call(
        functools.partial(_expert_kernel, block_rows=block_rows),
        grid_spec=grid_spec,
        out_shape=jax.ShapeDtypeStruct((n_rows * TILE_ROWS, LANES), U32),
        compiler_params=_params("arbitrary"),
        name="moe_experts",
    )(blk_e, nused, xs, w1, w3, w2)


def _combine_kernel(dest_ref, ys_hbm, x_ref, shift_ref, scale_ref, gate_ref, gcol_ref,
                    ws1_ref, ws3_ref, ws2_ref, g_ref, b_ref, o_ref, buf, sem,
                    *, tm, t_total, alpha):
    step = pl.program_id(0)
    slot = step % 2

    def start_gather(for_step, into):
        base = for_step * tm

        def issue(t, carry):
            for k in range(TOP_K):
                _tile_copy(ys_hbm, dest_ref[k * t_total + base + t], buf.at[into, k], t,
                           sem.at[into]).start()
            return carry

        lax.fori_loop(0, tm, issue, 0, unroll=2)

    @pl.when(step == 0)
    def _():
        start_gather(0, 0)

    @pl.when(step + 1 < pl.num_programs(0))
    def _():
        start_gather(step + 1, 1 - slot)

    x = x_ref[...]
    hb = (x * (1.0 + scale_ref[0]) + shift_ref[0]).astype(BF16)
    h1 = jnp.dot(hb, ws1_ref[...], preferred_element_type=F32)
    h3 = jnp.dot(hb, ws3_ref[...], preferred_element_type=F32)
    act = (h1 * _sigmoid(h1)) * h3
    y = jnp.dot(act.astype(BF16), ws2_ref[...], preferred_element_type=F32)

    for k in range(TOP_K):
        pltpu.make_async_copy(ys_hbm.at[pl.ds(0, tm * TILE_ROWS)], buf.at[slot, k],
                              sem.at[slot]).wait()

    gcol = gcol_ref[...]
    for k in range(TOP_K):
        y = y + _unpack_bf16_pairs(_from_token_tiles(buf.at[slot, k], tm)) * gcol[:, k:k + 1]
    z = alpha * x + gate_ref[0] * y
    o_ref[...] = _layer_norm(z, g_ref[...], b_ref[...])


def moe_combine(dest_flat, ys, xt, shift, scale, gate1p, gcol, ws1, ws3, ws2, ln_g, ln_b,
                s_len, alpha, tm=128):
    t, d = xt.shape
    ds_ = ws1.shape[1]
    per_b = s_len // tm

    def const(shape):
        return pl.BlockSpec(shape, lambda i, dest: (0,) * len(shape))

    mod_spec = pl.BlockSpec((1, 1, d), lambda i, dest: (i // per_b, 0, 0))
    grid_spec = pltpu.PrefetchScalarGridSpec(
        num_scalar_prefetch=1,
        grid=(t // tm,),
        in_specs=[
            pl.BlockSpec(memory_space=pl.ANY),
            pl.BlockSpec((tm, d), lambda i, dest: (i, 0)),
            mod_spec, mod_spec, mod_spec,
            pl.BlockSpec((tm, LANES), lambda i, dest: (i, 0)),
            const((d, ds_)), const((d, ds_)), const((ds_, d)),
            const((1, d)), const((1, d)),
        ],
        out_specs=pl.BlockSpec((tm, d), lambda i, dest: (i, 0)),
        scratch_shapes=[pltpu.VMEM((2, TOP_K, tm * TILE_ROWS, LANES), U32),
                        pltpu.SemaphoreType.DMA((2,))],
    )
    return pl.pallas_call(
        functools.partial(_combine_kernel, tm=tm, t_total=t, alpha=alpha),
        grid_spec=grid_spec,
        out_shape=jax.ShapeDtypeStruct((t, d), F32),
        compiler_params=_params("arbitrary"),
        name="moe_combine",
    )(dest_flat, ys, xt, shift, scale, gate1p, gcol, ws1, ws3, ws2,
      ln_g.reshape(1, d), ln_b.reshape(1, d))


def moe_layer(x, mods, router_w, router_bias, w1, w3, w2, ws1, ws3, ws2, ln_g, ln_b, alpha,
              layer):
    b, s, d = x.shape
    t = b * s
    shift = mods[:, None, 0:d]
    scale = mods[:, None, d:2 * d]
    gate1p = 1.0 + mods[:, None, 2 * d:3 * d]
    xt = x.reshape(t, d)
    n_blocks = t * TOP_K // MOE_BLOCK + N_EXPERTS
    eidx, rank, gcol, counts = moe_route(xt, shift, scale, router_w, router_bias, s)
    dest, blk_e, nused = moe_slots(counts, eidx, rank, n_blocks, MOE_BLOCK)
    dest_flat = dest.reshape(-1)
    xs = moe_dispatch(dest_flat, xt, shift, scale, n_blocks * MOE_BLOCK, s)
    n_e = w1.shape[1]
    ys = moe_experts(blk_e.reshape(-1), nused.reshape(-1), xs,
                     w1.reshape((-1,) + w1.shape[2:]), w3.reshape((-1,) + w3.shape[2:]),
                     w2.reshape((-1,) + w2.shape[2:]), MOE_BLOCK, layer * n_e)
    out = moe_combine(dest_flat, ys, xt, shift, scale, gate1p, gcol,
                      ws1.astype(BF16), ws3.astype(BF16), ws2.astype(BF16), ln_g, ln_b, s, alpha)
    return out.reshape(b, s, d)


def kernel(x, c, positions, ada_w, ada_b, ln_g, ln_b, attn_w_in, attn_w_out, gdn_w_in, gdn_conv_w,
           gdn_a_log, gdn_dt_bias, gdn_norm_w, gdn_w_out, router_w, router_bias, expert_w1,
           expert_w3, expert_w2, shared_w1, shared_w3, shared_w2):
    depth = ada_w.shape[0]
    alpha = (2 * depth) ** 0.25
    mods = ada_vectors(c, ada_w, ada_b)
    cos_t, sin_t = rope_tables(positions)
    for i in range(depth):
        j = i // 2
        if i % 2 == 0:
            x = attention_layer(x, mods[2 * i], cos_t, sin_t, attn_w_in[j], attn_w_out[j],
                                ln_g[i, 0], ln_b[i, 0], alpha)
        else:
            x = gdn_layer(x, mods[2 * i], gdn_w_in[j], gdn_conv_w[j], gdn_a_log[j],
                          gdn_dt_bias[j], gdn_norm_w[j], gdn_w_out[j], ln_g[i, 0], ln_b[i, 0],
                          alpha)
        x = moe_layer(x, mods[2 * i + 1], router_w[i], router_bias[i], expert_w1,
                      expert_w3, expert_w2, shared_w1[i], shared_w3[i], shared_w2[i],
                      ln_g[i, 1], ln_b[i, 1], alpha, i)
    return x
```

```python
import functools
import math

import jax
import jax.numpy as jnp
from jax import lax
from jax.experimental import pallas as pl
from jax.experimental.pallas import tpu as pltpu

F32 = jnp.float32
BF16 = jnp.bfloat16

LANES = 128

DILATED_GROUPS = ((128, 1), (512, 4), (2048, 16))
A_HEADS = 16
A_HEAD_DIM = 64
ATTN_BLOCK = 128
ROPE_THETA = 10000.0
GDN_HEADS = 8
GDN_HEAD_DIM = 128
CONV_WIDTH = 4
GDN_CHUNK = 64
N_EXPERTS = 64
TOP_K = 8
N_EXPERT_GROUPS = 8
TOPK_GROUPS = 4
ROUTED_SCALE = 2.5
MOE_BLOCK = 512
LN_EPS = 1e-5
RMS_EPS = 1e-6
NEG_BIG = -1e30

VMEM_LIMIT = 56 * 1024 * 1024


def _params(*sem):
    return pltpu.CompilerParams(dimension_semantics=sem, vmem_limit_bytes=VMEM_LIMIT)


def _ada_kernel(c_ref, w_ref, b_ref, o_ref):
    c = c_ref[...]
    c_hi = c.astype(BF16)
    c_lo = (c - c_hi.astype(F32)).astype(BF16)
    w = w_ref[0]
    w_hi = w.astype(BF16)
    w_lo = (w - w_hi.astype(F32)).astype(BF16)
    acc = jnp.dot(c_hi, w_hi, preferred_element_type=F32)
    acc += jnp.dot(c_hi, w_lo, preferred_element_type=F32)
    acc += jnp.dot(c_lo, w_hi, preferred_element_type=F32)
    o_ref[0] = acc + b_ref[0]


def ada_vectors(c, ada_w, ada_b):
    depth2 = ada_w.shape[0] * ada_w.shape[1]
    b, d = c.shape
    d3 = ada_w.shape[-1]
    tn = d
    w = ada_w.reshape(depth2, d, d3)
    bias = ada_b.reshape(depth2, 1, d3)
    return pl.pallas_call(
        _ada_kernel,
        grid=(depth2, d3 // tn),
        in_specs=[
            pl.BlockSpec((b, d), lambda l, j: (0, 0)),
            pl.BlockSpec((1, d, tn), lambda l, j: (l, 0, j)),
            pl.BlockSpec((1, 1, tn), lambda l, j: (l, 0, j)),
        ],
        out_specs=pl.BlockSpec((1, b, tn), lambda l, j: (l, 0, j)),
        out_shape=jax.ShapeDtypeStruct((depth2, b, d3), F32),
        compiler_params=_params("parallel", "parallel"),
        name="ada_vectors",
    )(c, w, bias)


def _rope_kernel(pos_ref, freq_ref, sign_ref, cos_ref, sin_ref):
    ang = pos_ref[0].astype(F32) * freq_ref[...]
    cos_ref[0] = jnp.cos(ang)
    sin_ref[0] = jnp.sin(ang) * sign_ref[...]


def rope_tables(positions):
    b, s = positions.shape
    half = A_HEAD_DIM // 2
    inv_freq = ROPE_THETA ** (-jnp.arange(half, dtype=F32) / half)
    freq = jnp.tile(inv_freq, LANES // half).reshape(1, LANES)
    sign = jnp.tile(jnp.concatenate([-jnp.ones((half,), F32), jnp.ones((half,), F32)]),
                    LANES // A_HEAD_DIM).reshape(1, LANES)
    ts = min(s, 1024)
    return pl.pallas_call(
        _rope_kernel,
        grid=(b, s // ts),
        in_specs=[
            pl.BlockSpec((1, ts, 1), lambda i, j: (i, j, 0)),
            pl.BlockSpec((1, LANES), lambda i, j: (0, 0)),
            pl.BlockSpec((1, LANES), lambda i, j: (0, 0)),
        ],
        out_specs=[pl.BlockSpec((1, ts, LANES), lambda i, j: (i, j, 0))] * 2,
        out_shape=[jax.ShapeDtypeStruct((b, s, LANES), F32)] * 2,
        compiler_params=_params("parallel", "parallel"),
        name="rope_tables",
    )(positions.reshape(b, s, 1), freq, sign)


def _residue_major(ref, lead, dil, tl):
    if dil == 1:
        return ref[lead] if lead is not None else ref[...]
    parts = []
    for r in range(dil):
        if lead is None:
            parts.append(ref[pl.ds(r, tl, stride=dil), :])
        else:
            parts.append(ref[lead, pl.ds(r, tl, stride=dil), :])
    return jnp.concatenate(parts, axis=0)


def _attn_proj_kernel(x_ref, shift_ref, scale_ref, cos_ref, sin_ref, w_ref,
                      q_ref, k_ref, v_ref, h_scr, *, dil, tm):
    tl = tm // dil
    d = x_ref.shape[-1]
    h = x_ref[0] * (1.0 + scale_ref[0]) + shift_ref[0]
    if dil == 1:
        hb = h.astype(BF16)
    else:
        for cidx in range(d // LANES):
            h_scr[cidx] = h[:, cidx * LANES:(cidx + 1) * LANES]
        hb = jnp.concatenate(
            [_residue_major(h_scr, cidx, dil, tl).astype(BF16) for cidx in range(d // LANES)],
            axis=1)
    cosp = _residue_major(cos_ref, 0, dil, tl)
    sinp = _residue_major(sin_ref, 0, dil, tl)
    lane = lax.broadcasted_iota(jnp.int32, (tm, LANES), 1)
    first_half = (lane % A_HEAD_DIM) < (A_HEAD_DIM // 2)
    qscale = A_HEAD_DIM ** -0.5
    for j, out_ref in enumerate((q_ref, k_ref, v_ref)):
        res = jnp.dot(hb, w_ref[:, j * d:(j + 1) * d], preferred_element_type=F32)
        for hp in range(d // LANES):
            blk = res[:, hp * LANES:(hp + 1) * LANES]
            if j < 2:
                swapped = jnp.where(first_half,
                                    pltpu.roll(blk, LANES - A_HEAD_DIM // 2, 1),
                                    pltpu.roll(blk, A_HEAD_DIM // 2, 1))
                blk = blk * cosp + swapped * sinp
                if j == 0:
                    blk = blk * qscale
            blk = blk.astype(BF16)
            for r in range(dil):
                out_ref[0, hp, r] = blk[r * tl:(r + 1) * tl]


def attn_project(x, shift, scale, cos_t, sin_t, w_g, dil, tm=512):
    b, s, d = x.shape
    tl = tm // dil
    n_hp = d // LANES
    out_sds = jax.ShapeDtypeStruct((b, n_hp, dil, s // dil, LANES), BF16)
    out_spec = pl.BlockSpec((1, n_hp, dil, tl, LANES), lambda i, j: (i, 0, 0, j, 0))
    return pl.pallas_call(
        functools.partial(_attn_proj_kernel, dil=dil, tm=tm),
        grid=(b, s // tm),
        in_specs=[
            pl.BlockSpec((1, tm, d), lambda i, j: (i, j, 0)),
            pl.BlockSpec((1, 1, d), lambda i, j: (i, 0, 0)),
            pl.BlockSpec((1, 1, d), lambda i, j: (i, 0, 0)),
            pl.BlockSpec((1, tm, LANES), lambda i, j: (i, j, 0)),
            pl.BlockSpec((1, tm, LANES), lambda i, j: (i, j, 0)),
            pl.BlockSpec((d, 3 * d), lambda i, j: (0, 0)),
        ],
        out_specs=[out_spec] * 3,
        out_shape=[out_sds] * 3,
        scratch_shapes=[pltpu.VMEM((d // LANES, tm, LANES), F32)],
        compiler_params=_params("parallel", "parallel"),
        name=f"attn_project_d{dil}",
    )(x, shift, scale, cos_t, sin_t, w_g)


def _attn_kernel(*refs, s_len, dils):
    n_g = len(dils)
    qkv = refs[:3 * n_g]
    o_ref = refs[3 * n_g]
    oacc, lacc = refs[3 * n_g + 1:]
    blk = ATTN_BLOCK

    lane = lax.broadcasted_iota(jnp.int32, (blk, LANES), 1)
    head0 = lane < A_HEAD_DIM
    row = lax.broadcasted_iota(jnp.int32, (2 * blk, 2 * blk), 0) % blk
    col = lax.broadcasted_iota(jnp.int32, (2 * blk, 2 * blk), 1)
    bias_rest = jnp.where((col >= row) & (col <= row + blk), 0.0, NEG_BIG).astype(F32)
    bias_first = bias_rest[:, blk:]

    def run_blocks(g, q_ref, k_ref, v_ref, dil, specs):
        ids = range(len(specs))
        qq, kc, vc = [], [], []
        for r, n, first in specs:
            q = q_ref[0, 0, r, pl.ds(pl.multiple_of(n * blk, blk), blk), :]
            zero = jnp.zeros_like(q)
            qq.append(jnp.concatenate([jnp.where(head0, q, zero), jnp.where(head0, zero, q)],
                                      axis=0))
            if first:
                keys = pl.ds(0, blk)
            else:
                keys = pl.ds(pl.multiple_of((n - 1) * blk, blk), 2 * blk)
            kc.append(k_ref[0, 0, r, keys, :])
            vc.append(v_ref[0, 0, r, keys, :])
        sc = [lax.dot_general(qq[i], kc[i], (((1,), (1,)), ((), ())), preferred_element_type=F32)
              + (bias_first if specs[i][2] else bias_rest) for i in ids]
        m = [jnp.max(sc[i], axis=1, keepdims=True) for i in ids]
        p = [jnp.exp(sc[i] - m[i]) for i in ids]
        l = [jnp.sum(p[i], axis=1, keepdims=True) for i in ids]
        pv = [jnp.dot(p[i].astype(BF16), vc[i], preferred_element_type=F32) * (1.0 / l[i])
              for i in ids]
        for i in ids:
            r, n, _ = specs[i]
            lse = m[i] + jnp.log(l[i])
            o_blk = jnp.where(head0, pv[i][:blk], pv[i][blk:])
            lse_blk = jnp.where(head0, jnp.broadcast_to(lse[:blk], (blk, LANES)),
                                jnp.broadcast_to(lse[blk:], (blk, LANES)))
            start = n * (blk * dil) + r
            if dil == 1:
                idx = pl.ds(pl.multiple_of(start, blk), blk)
            else:
                idx = pl.ds(start, blk, stride=dil)
            oacc[g, idx, :] = o_blk
            lacc[g, idx, :] = lse_blk

    for g, dil in enumerate(dils):
        q_ref, k_ref, v_ref = qkv[3 * g:3 * g + 3]
        nb = s_len // dil // blk

        def per_step(rr, carry, g=g, dil=dil, nb=nb, q_ref=q_ref, k_ref=k_ref, v_ref=v_ref):
            if dil == 1:
                res, width = [0], 4
            else:
                res, width = [2 * rr, 2 * rr + 1], 2
            if nb < width:
                run_blocks(g, q_ref, k_ref, v_ref, dil, [(r, 0, True) for r in res])
                return carry
            run_blocks(g, q_ref, k_ref, v_ref, dil,
                       [(r, n, n == 0) for r in res for n in range(width)])

            def per_group(i, c2):
                run_blocks(g, q_ref, k_ref, v_ref, dil,
                           [(r, width * i + n, False) for r in res for n in range(width)])
                return c2

            lax.fori_loop(1, nb // width, per_group, 0)
            return carry

        lax.fori_loop(0, max(dil // 2, 1), per_step, 0)

    tc = 256

    def mix(i, carry):
        rows = pl.ds(pl.multiple_of(i * tc, tc), tc)
        ls = [lacc[g, rows, :] for g in range(n_g)]
        mx = functools.reduce(jnp.maximum, ls)
        es = [jnp.exp(l - mx) for l in ls]
        den = functools.reduce(lambda a, b2: a + b2, es)
        num = es[0] * oacc[0, rows, :]
        for g in range(1, n_g):
            num = num + es[g] * oacc[g, rows, :]
        o_ref[0, 0, rows, :] = (num * (1.0 / den)).astype(o_ref.dtype)
        return carry

    lax.fori_loop(0, s_len // tc, mix, 0)


def attn_core(qkvs, dils, s_len):
    b, n_hp = qkvs[0].shape[:2]
    in_specs = []
    for g, dil in enumerate(dils):
        spec = pl.BlockSpec((1, 1, dil, s_len // dil, LANES), lambda i, j: (i, j, 0, 0, 0))
        in_specs += [spec] * 3
    n_g = len(dils)
    return pl.pallas_call(
        functools.partial(_attn_kernel, s_len=s_len, dils=tuple(dils)),
        grid=(b, n_hp),
        in_specs=in_specs,
        out_specs=pl.BlockSpec((1, 1, s_len, LANES), lambda i, j: (i, j, 0, 0)),
        out_shape=jax.ShapeDtypeStruct((b, n_hp, s_len, LANES), BF16),
        scratch_shapes=[pltpu.VMEM((n_g, s_len, LANES), F32),
                        pltpu.VMEM((n_g, s_len, LANES), F32)],
        compiler_params=_params("parallel", "parallel"),
        name="attn_core",
    )(*qkvs)


def _layer_norm(z, g, b):
    mu = jnp.mean(z, axis=-1, keepdims=True)
    zc = z - mu
    var = jnp.mean(zc * zc, axis=-1, keepdims=True)
    return zc * lax.rsqrt(var + LN_EPS) * g + b


def _out_ln_kernel(o_ref, w_ref, x_ref, gate_ref, g_ref, b_ref, y_ref, *, alpha):
    n_hp = o_ref.shape[1]
    o = jnp.concatenate([o_ref[0, hp] for hp in range(n_hp)], axis=-1)
    y = jnp.dot(o, w_ref[...], preferred_element_type=F32)
    z = alpha * x_ref[0] + gate_ref[0] * y
    y_ref[0] = _layer_norm(z, g_ref[...], b_ref[...])


def out_proj_ln(o_heads, w_out, x, gate1p, ln_g, ln_b, alpha, tm=512):
    b, s, d = x.shape
    n_hp = o_heads.shape[1]
    return pl.pallas_call(
        functools.partial(_out_ln_kernel, alpha=alpha),
        grid=(b, s // tm),
        in_specs=[
            pl.BlockSpec((1, n_hp, tm, LANES), lambda i, j: (i, 0, j, 0)),
            pl.BlockSpec((d, d), lambda i, j: (0, 0)),
            pl.BlockSpec((1, tm, d), lambda i, j: (i, j, 0)),
            pl.BlockSpec((1, 1, d), lambda i, j: (i, 0, 0)),
            pl.BlockSpec((1, d), lambda i, j: (0, 0)),
            pl.BlockSpec((1, d), lambda i, j: (0, 0)),
        ],
        out_specs=pl.BlockSpec((1, tm, d), lambda i, j: (i, j, 0)),
        out_shape=jax.ShapeDtypeStruct((b, s, d), F32),
        compiler_params=_params("parallel", "parallel"),
        name="out_proj_ln",
    )(o_heads, w_out, x, gate1p, ln_g.reshape(1, d), ln_b.reshape(1, d))


def attention_layer(x, mods, cos_t, sin_t, w_in, w_out, ln_g, ln_b, alpha):
    b, s, d = x.shape
    shift = mods[:, None, 0:d]
    scale = mods[:, None, d:2 * d]
    gate1p = 1.0 + mods[:, None, 2 * d:3 * d]
    w_in = w_in.astype(BF16)
    dils = [dil for _, dil in DILATED_GROUPS]
    qkvs = []
    for g, dil in enumerate(dils):
        qkvs += attn_project(x, shift, scale, cos_t, sin_t,
                             w_in[:, g * 3 * d:(g + 1) * 3 * d], dil)
    o = attn_core(qkvs, dils, s)
    return out_proj_ln(o, w_out.astype(BF16), x, gate1p, ln_g, ln_b, alpha)


def _split_bf16(a):
    hi = a.astype(BF16)
    lo = (a - hi.astype(F32)).astype(BF16)
    return hi, lo


def _gdn_proj_kernel(x_ref, shift_ref, scale_ref, w_ref, wab_hi_ref, wab_lo_ref,
                     wabt_hi_ref, wabt_lo_ref, q_ref, k_ref, v_ref, z_ref, ab_ref, abt_ref, *, tm):
    d = x_ref.shape[-1]
    h = x_ref[0] * (1.0 + scale_ref[0]) + shift_ref[0]
    h_hi, h_lo = _split_bf16(h)
    for j, out_ref in enumerate((q_ref, k_ref, v_ref, z_ref)):
        res = jnp.dot(h_hi, w_ref[:, j * d:(j + 1) * d], preferred_element_type=F32)
        for hd in range(d // LANES):
            out_ref[0, hd] = res[:, hd * LANES:(hd + 1) * LANES].astype(BF16)
    ab = jnp.dot(h_hi, wab_hi_ref[...], preferred_element_type=F32)
    ab += jnp.dot(h_lo, wab_hi_ref[...], preferred_element_type=F32)
    ab += jnp.dot(h_hi, wab_lo_ref[...], preferred_element_type=F32)
    ab_ref[0] = ab
    nt = (((1,), (1,)), ((), ()))
    abt = lax.dot_general(wabt_hi_ref[...], h_hi, nt, preferred_element_type=F32)
    abt += lax.dot_general(wabt_hi_ref[...], h_lo, nt, preferred_element_type=F32)
    abt += lax.dot_general(wabt_lo_ref[...], h_hi, nt, preferred_element_type=F32)
    for n in range(tm // GDN_CHUNK):
        abt_ref[0, n] = abt[:, n * GDN_CHUNK:(n + 1) * GDN_CHUNK]


def gdn_project(x, shift, scale, w_in, tm=512):
    b, s, d = x.shape
    n_h = d // LANES
    w_main = w_in[:, :4 * d].astype(BF16)
    w_ab = jnp.pad(w_in[:, 4 * d:], ((0, 0), (0, LANES - 2 * GDN_HEADS)))
    w_abt = w_in[:, 4 * d:].T
    wab_hi, wab_lo = _split_bf16(w_ab)
    wabt_hi, wabt_lo = _split_bf16(w_abt)
    head_sds = jax.ShapeDtypeStruct((b, n_h, s, LANES), BF16)
    head_spec = pl.BlockSpec((1, n_h, tm, LANES), lambda i, j: (i, 0, j, 0))
    n_ab = 2 * GDN_HEADS
    return pl.pallas_call(
        functools.partial(_gdn_proj_kernel, tm=tm),
        grid=(b, s // tm),
        in_specs=[
            pl.BlockSpec((1, tm, d), lambda i, j: (i, j, 0)),
            pl.BlockSpec((1, 1, d), lambda i, j: (i, 0, 0)),
            pl.BlockSpec((1, 1, d), lambda i, j: (i, 0, 0)),
            pl.BlockSpec((d, 4 * d), lambda i, j: (0, 0)),
            pl.BlockSpec((d, LANES), lambda i, j: (0, 0)),
            pl.BlockSpec((d, LANES), lambda i, j: (0, 0)),
            pl.BlockSpec((n_ab, d), lambda i, j: (0, 0)),
            pl.BlockSpec((n_ab, d), lambda i, j: (0, 0)),
        ],
        out_specs=[head_spec] * 4 + [
            pl.BlockSpec((1, tm, LANES), lambda i, j: (i, j, 0)),
            pl.BlockSpec((1, tm // GDN_CHUNK, n_ab, GDN_CHUNK), lambda i, j: (i, j, 0, 0)),
        ],
        out_shape=[head_sds] * 4 + [
            jax.ShapeDtypeStruct((b, s, LANES), F32),
            jax.ShapeDtypeStruct((b, s // GDN_CHUNK, n_ab, GDN_CHUNK), F32),
        ],
        compiler_params=_params("parallel", "parallel"),
        name="gdn_project",
    )(x, shift, scale, w_main, wab_hi, wab_lo, wabt_hi, wabt_lo)


def _softplus(t):
    return jnp.maximum(t, 0.0) + jnp.log(1.0 + jnp.exp(-jnp.abs(t)))


def _sigmoid(t):
    return 1.0 / (1.0 + jnp.exp(-t))


def _dot3_exact_lhs(lhs_bf16, rhs_f32):
    acc = None
    rem = rhs_f32
    for _ in range(3):
        part = rem.astype(BF16)
        rem = rem - part.astype(F32)
        t = jnp.dot(lhs_bf16, part, preferred_element_type=F32)
        acc = t if acc is None else acc + t
    return acc


def _dot3_exact_rhs(lhs_f32, rhs_bf16):
    acc = None
    rem = lhs_f32
    for _ in range(3):
        part = rem.astype(BF16)
        rem = rem - part.astype(F32)
        t = jnp.dot(part, rhs_bf16, preferred_element_type=F32)
        acc = t if acc is None else acc + t
    return acc


def _gdn_kernel(q_ref, k_ref, v_ref, z_ref, ab_ref, abt_ref, convw_ref, alog_l_ref, dtb_l_ref,
                alog_c_ref, dtb_c_ref, normw_ref, o_ref,
                state, carry, qs, ks, vs, gcol_s, beta_s, *, tc):
    n_h = GDN_HEADS
    ch = GDN_CHUNK
    dh = GDN_HEAD_DIM
    halo = 8

    @pl.when(pl.program_id(1) == 0)
    def _():
        state[...] = jnp.zeros_like(state)
        carry[:, :, 0:halo, :] = jnp.zeros((3, n_h, halo, dh), F32)

    for a, (src, dst) in enumerate(((q_ref, qs), (k_ref, ks), (v_ref, vs))):
        for hd in range(n_h):
            xb = carry.at[a, hd]
            xb[halo:halo + tc, :] = src[0, hd].astype(F32)
            w = convw_ref[a, hd]
            y = xb[halo:halo + tc, :] * w[CONV_WIDTH - 1:CONV_WIDTH]
            for j in range(CONV_WIDTH - 1):
                off = halo - (CONV_WIDTH - 1) + j
                y = y + xb[off:off + tc, :] * w[j:j + 1]
            xb[0:halo, :] = xb[tc:tc + halo, :]
            y = y * _sigmoid(y)
            if a < 2:
                y = y * lax.rsqrt(jnp.sum(y * y, axis=-1, keepdims=True) + RMS_EPS)
                if a == 0:
                    y = y * (dh ** -0.5)
            dst[hd] = y

    ab = ab_ref[0]
    gcol_s[...] = -jnp.exp(alog_l_ref[...]) * _softplus(ab + dtb_l_ref[...])
    beta_s[...] = _sigmoid(ab)

    ri = lax.broadcasted_iota(jnp.int32, (ch, ch), 0)
    ci = lax.broadcasted_iota(jnp.int32, (ch, ch), 1)
    causal = ri >= ci
    strict = ri > ci
    tril = jnp.where(causal, 1.0, 0.0).astype(BF16)
    triu = jnp.where(ri <= ci, 1.0, 0.0).astype(BF16)
    eye = jnp.where(ri == ci, 1.0, 0.0).astype(F32)
    nt = (((1,), (1,)), ((), ()))
    tn = (((0,), (0,)), ((), ()))

    n_par = 4 if (tc // ch) % 4 == 0 else 1
    heads = range(n_h)

    def chunk_group(gidx, carry_unused):
        cids = [gidx * n_par + j for j in range(n_par)]
        rows = [pl.ds(pl.multiple_of(c * ch, ch), ch) for c in cids]
        gc = [_dot3_exact_lhs(tril, gcol_s[r, :]) for r in rows]
        gc_t = []
        for c in cids:
            a_t = abt_ref[0, c][0:n_h, :]
            g_t = -jnp.exp(alog_c_ref[...]) * _softplus(a_t + dtb_c_ref[...])
            gc_t.append(_dot3_exact_rhs(g_t, triu))
        beta = [beta_s[r, :] for r in rows]
        items = [(j, hd) for j in range(n_par) for hd in heads]
        ids = range(len(items))
        gcol = [gc[j][:, hd:hd + 1] for j, hd in items]
        bcol = [beta[j][:, n_h + hd:n_h + hd + 1] for j, hd in items]
        decay = [jnp.exp(jnp.where(causal, gcol[i] - gc_t[j][hd:hd + 1, :], NEG_BIG))
                 for i, (j, hd) in enumerate(items)]
        qc = [qs[hd, rows[j], :] for j, hd in items]
        kc = [ks[hd, rows[j], :] for j, hd in items]
        kb = [kc[i] * bcol[i] for i in ids]
        both = [lax.dot_general(jnp.concatenate([kb[i], qc[i]], axis=0).astype(BF16),
                                kc[i].astype(BF16), nt, preferred_element_type=F32)
                for i in ids]
        lmat = [jnp.where(strict, both[i][:ch] * decay[i], 0.0) for i in ids]
        qk = [(both[i][ch:] * decay[i]).astype(BF16) for i in ids]
        pm = [-lmat[i] for i in ids]
        mm = lmat
        for _ in range(5):
            mb = [mm[i].astype(BF16) for i in ids]
            mm = [jnp.dot(mb[i], mb[i], preferred_element_type=F32) for i in ids]
            pm = [pm[i] + mm[i] + jnp.dot(pm[i].astype(BF16), mm[i].astype(BF16),
                                          preferred_element_type=F32) for i in ids]
        eg = [jnp.exp(gcol[i]) for i in ids]
        rhs = [jnp.concatenate([vs[hd, rows[j], :] * bcol[i], kb[i] * eg[i]], axis=1)
               for i, (j, hd) in enumerate(items)]
        sol = [rhs[i] + jnp.dot(pm[i].astype(BF16), rhs[i].astype(BF16),
                                preferred_element_type=F32) for i in ids]
        glast = [gcol[i][ch - 1:ch, :] for i in ids]
        kdec = [(kc[i] * jnp.exp(glast[i] - gcol[i])).astype(BF16) for i in ids]
        wqg = [jnp.concatenate([sol[i][:, dh:], qc[i] * eg[i]], axis=0).astype(BF16)
               for i in ids]
        for j in range(n_par):
            sel = [i for i in ids if items[i][0] == j]
            s_prev = [state[hd] for hd in heads]
            wq = [jnp.dot(wqg[i], s_prev[hd].astype(BF16), preferred_element_type=F32)
                  for hd, i in enumerate(sel)]
            vb = [(sol[i][:, :dh] - wq[hd][:ch]).astype(BF16) for hd, i in enumerate(sel)]
            o = [wq[hd][ch:] + jnp.dot(qk[i], vb[hd], preferred_element_type=F32)
                 for hd, i in enumerate(sel)]
            for hd, i in enumerate(sel):
                state[hd] = s_prev[hd] * jnp.exp(glast[i]) + lax.dot_general(
                    kdec[i], vb[hd], tn, preferred_element_type=F32)
            for hd in heads:
                on = o[hd] * lax.rsqrt(jnp.mean(o[hd] * o[hd], axis=-1, keepdims=True)
                                       + RMS_EPS) * normw_ref[...]
                zc = z_ref[0, hd, rows[j], :].astype(F32)
                o_ref[0, hd, rows[j], :] = (on * (zc * _sigmoid(zc))).astype(o_ref.dtype)
        return carry_unused

    lax.fori_loop(0, tc // ch // n_par, chunk_group, 0)


def gdn_core(q, k, v, z, ab, abt, conv_w, a_log, dt_bias, norm_w, tc=512):
    b, n_h, s, dh = q.shape
    n_ab = 2 * n_h
    convw = conv_w.reshape(CONV_WIDTH, 3, n_h, dh).transpose(1, 2, 0, 3)
    pad = (0, LANES - n_h)
    alog_l = jnp.pad(a_log.astype(F32), pad).reshape(1, LANES)
    dtb_l = jnp.pad(dt_bias.astype(F32), pad).reshape(1, LANES)
    alog_c = a_log.astype(F32).reshape(n_h, 1)
    dtb_c = dt_bias.astype(F32).reshape(n_h, 1)
    head_spec = pl.BlockSpec((1, n_h, tc, dh), lambda i, j: (i, 0, j, 0))

    def full(shape):
        return pl.BlockSpec(shape, lambda i, j: (0,) * len(shape))

    return pl.pallas_call(
        functools.partial(_gdn_kernel, tc=tc),
        grid=(b, s // tc),
        in_specs=[head_spec] * 4 + [
            pl.BlockSpec((1, tc, LANES), lambda i, j: (i, j, 0)),
            pl.BlockSpec((1, tc // GDN_CHUNK, n_ab, GDN_CHUNK), lambda i, j: (i, j, 0, 0)),
            full((3, n_h, CONV_WIDTH, dh)),
            full((1, LANES)), full((1, LANES)), full((n_h, 1)), full((n_h, 1)), full((1, dh)),
        ],
        out_specs=head_spec,
        out_shape=jax.ShapeDtypeStruct((b, n_h, s, dh), BF16),
        scratch_shapes=[
            pltpu.VMEM((n_h, dh, dh), F32),
            pltpu.VMEM((3, n_h, 8 + tc, dh), F32),
            pltpu.VMEM((n_h, tc, dh), F32),
            pltpu.VMEM((n_h, tc, dh), F32),
            pltpu.VMEM((n_h, tc, dh), F32),
            pltpu.VMEM((tc, LANES), F32),
            pltpu.VMEM((tc, LANES), F32),
        ],
        compiler_params=_params("parallel", "arbitrary"),
        name="gdn_core",
    )(q, k, v, z, ab, abt, convw, alog_l, dtb_l, alog_c, dtb_c, norm_w.reshape(1, dh).astype(F32))


def gdn_layer(x, mods, w_in, conv_w, a_log, dt_bias, norm_w, w_out, ln_g, ln_b, alpha):
    b, s, d = x.shape
    shift = mods[:, None, 0:d]
    scale = mods[:, None, d:2 * d]
    gate1p = 1.0 + mods[:, None, 2 * d:3 * d]
    q, k, v, z, ab, abt = gdn_project(x, shift, scale, w_in)
    o = gdn_core(q, k, v, z, ab, abt, conv_w, a_log, dt_bias, norm_w)
    return out_proj_ln(o, w_out.astype(BF16), x, gate1p, ln_g, ln_b, alpha)


def _first_argmax(vals, idx, n):
    mx = jnp.max(vals, axis=0, keepdims=True)
    first = jnp.min(jnp.where(vals == mx, idx, n), axis=0, keepdims=True)
    return mx, first


def _router_kernel(x_ref, shift_ref, scale_ref, rwt_hi_ref, rwt_lo_ref, bias_ref,
                   eidx_ref, rank_ref, gcol_ref, cnt_ref, cnt_scr, *, tm):
    n_e = N_EXPERTS
    per_g = n_e // N_EXPERT_GROUPS

    @pl.when(pl.program_id(0) == 0)
    def _():
        cnt_scr[...] = jnp.zeros_like(cnt_scr)

    h = x_ref[...] * (1.0 + scale_ref[0]) + shift_ref[0]
    h_hi, h_lo = _split_bf16(h)
    nt = (((1,), (1,)), ((), ()))
    logits = lax.dot_general(rwt_hi_ref[...], h_hi, nt, preferred_element_type=F32)
    logits += lax.dot_general(rwt_hi_ref[...], h_lo, nt, preferred_element_type=F32)
    logits += lax.dot_general(rwt_lo_ref[...], h_hi, nt, preferred_element_type=F32)
    scores = _sigmoid(logits)
    biased = scores + bias_ref[...]

    r8 = lax.broadcasted_iota(jnp.int32, (per_g, tm), 0)
    gscores = []
    for g in range(N_EXPERT_GROUPS):
        blk = biased[g * per_g:(g + 1) * per_g]
        m1, i1 = _first_argmax(blk, r8, per_g)
        m2 = jnp.max(jnp.where(r8 == i1, NEG_BIG, blk), axis=0, keepdims=True)
        gscores.append(m1 + m2)
    vals = jnp.concatenate(gscores, axis=0)
    rg = lax.broadcasted_iota(jnp.int32, (N_EXPERT_GROUPS, tm), 0)
    gsel = jnp.zeros((N_EXPERT_GROUPS, tm), F32)
    for _ in range(TOPK_GROUPS):
        _, idx = _first_argmax(vals, rg, N_EXPERT_GROUPS)
        hit = rg == idx
        gsel = jnp.where(hit, 1.0, gsel)
        vals = jnp.where(hit, NEG_BIG, vals)
    emask = jnp.concatenate(
        [jnp.broadcast_to(gsel[g:g + 1], (per_g, tm)) for g in range(N_EXPERT_GROUPS)], axis=0)
    masked = jnp.where(emask > 0.5, biased, NEG_BIG)

    re = lax.broadcasted_iota(jnp.int32, (n_e, tm), 0)
    hits, eidx, gates = [], [], []
    for _ in range(TOP_K):
        _, idx = _first_argmax(masked, re, n_e)
        hit = re == idx
        gates.append(jnp.sum(jnp.where(hit, scores, 0.0), axis=0, keepdims=True))
        masked = jnp.where(hit, NEG_BIG, masked)
        hits.append(hit)
        eidx.append(idx)
    gates = jnp.concatenate(gates, axis=0)
    gates = gates / jnp.sum(gates, axis=0, keepdims=True) * ROUTED_SCALE
    onehot = jnp.zeros((n_e, tm), F32)
    for hit in hits:
        onehot = jnp.where(hit, 1.0, onehot)

    ti = lax.broadcasted_iota(jnp.int32, (tm, tm), 0)
    tj = lax.broadcasted_iota(jnp.int32, (tm, tm), 1)
    before = jnp.where(ti < tj, 1.0, 0.0).astype(BF16)
    prefix = jnp.dot(onehot.astype(BF16), before, preferred_element_type=F32) + cnt_scr[...]
    ranks = [jnp.sum(jnp.where(hit, prefix, 0.0), axis=0, keepdims=True) for hit in hits]
    cnt_scr[...] = cnt_scr[...] + jnp.sum(onehot, axis=1, keepdims=True)

    eidx_ref[...] = jnp.concatenate(eidx, axis=0)
    rank_ref[...] = jnp.concatenate(ranks, axis=0).astype(jnp.int32)
    cnt_ref[...] = jnp.broadcast_to(cnt_scr[...], cnt_ref.shape)
    gpad = jnp.concatenate([gates, jnp.zeros((LANES - TOP_K, tm), F32)], axis=0)
    gcol_ref[...] = gpad.T


def moe_route(xt, shift, scale, router_w, router_bias, s_len, tm=512):
    t, d = xt.shape
    n_e = router_w.shape[1]
    rwt_hi, rwt_lo = _split_bf16(router_w.T)
    per_b = s_len // tm
    return pl.pallas_call(
        functools.partial(_router_kernel, tm=tm),
        grid=(t // tm,),
        in_specs=[
            pl.BlockSpec((tm, d), lambda i: (i, 0)),
            pl.BlockSpec((1, 1, d), lambda i: (i // per_b, 0, 0)),
            pl.BlockSpec((1, 1, d), lambda i: (i // per_b, 0, 0)),
            pl.BlockSpec((n_e, d), lambda i: (0, 0)),
            pl.BlockSpec((n_e, d), lambda i: (0, 0)),
            pl.BlockSpec((n_e, 1), lambda i: (0, 0)),
        ],
        out_specs=[
            pl.BlockSpec((TOP_K, tm), lambda i: (0, i)),
            pl.BlockSpec((TOP_K, tm), lambda i: (0, i)),
            pl.BlockSpec((tm, LANES), lambda i: (i, 0)),
            pl.BlockSpec((n_e, LANES), lambda i: (0, 0)),
        ],
        out_shape=[
            jax.ShapeDtypeStruct((TOP_K, t), jnp.int32),
            jax.ShapeDtypeStruct((TOP_K, t), jnp.int32),
            jax.ShapeDtypeStruct((t, LANES), F32),
            jax.ShapeDtypeStruct((n_e, LANES), F32),
        ],
        scratch_shapes=[pltpu.VMEM((n_e, 1), F32)],
        compiler_params=_params("arbitrary"),
        name="moe_route",
    )(xt, shift, scale, rwt_hi, rwt_lo, router_bias.reshape(n_e, 1).astype(F32))


def _slots_kernel(cnt_ref, eidx_ref, rank_ref, dest_ref, blk_e_ref, nused_ref, seg_ref,
                  *, block_rows):
    n_e = N_EXPERTS
    cnt = cnt_ref[...].astype(jnp.int32)
    padded = ((cnt + (block_rows - 1)) // block_rows) * block_rows
    ri = lax.broadcasted_iota(jnp.int32, (n_e, n_e), 0)
    ci = lax.broadcasted_iota(jnp.int32, (n_e, n_e), 1)
    tril = jnp.where(ri >= ci, 1.0, 0.0).astype(BF16)
    pend = _dot3_exact_lhs(tril, padded.astype(F32))
    pstart = pend.astype(jnp.int32) - padded
    eidx = eidx_ref[...]
    dest = rank_ref[...]
    for e in range(n_e):
        dest = dest + jnp.where(eidx == e, pstart[e:e + 1, 0:1], 0)
    dest_ref[...] = dest
    nb = blk_e_ref.shape[1]
    first_row = lax.broadcasted_iota(jnp.int32, (n_e, nb), 1) * block_rows
    pend_i = pend.astype(jnp.int32)[:, 0:1]
    owner = jnp.sum(jnp.where(pend_i <= first_row, 1, 0), axis=0, keepdims=True)
    blk_e_ref[...] = jnp.minimum(owner, n_e - 1)
    nused_ref[...] = pend_i[n_e - 1:n_e, :] // block_rows + jnp.zeros(nused_ref.shape, jnp.int32)
    lane = lax.broadcasted_iota(jnp.int32, seg_ref.shape, 1)
    seg_ref[...] = jnp.where(lane == 0, pend.astype(jnp.int32), padded)


def moe_slots(counts, eidx, rank, n_blocks, block_rows, tt=2048):
    t = eidx.shape[1]
    tt = min(tt, t)
    nb_pad = -(-n_blocks // LANES) * LANES
    n_e = counts.shape[0]
    return pl.pallas_call(
        functools.partial(_slots_kernel, block_rows=block_rows),
        grid=(t // tt,),
        in_specs=[
            pl.BlockSpec((n_e, LANES), lambda i: (0, 0)),
            pl.BlockSpec((TOP_K, tt), lambda i: (0, i)),
            pl.BlockSpec((TOP_K, tt), lambda i: (0, i)),
        ],
        out_specs=[
            pl.BlockSpec((TOP_K, tt), lambda i: (0, i)),
            pl.BlockSpec((1, nb_pad), lambda i: (0, 0)),
            pl.BlockSpec((1, LANES), lambda i: (0, 0)),
            pl.BlockSpec((n_e, LANES), lambda i: (0, 0)),
        ],
        out_shape=[
            jax.ShapeDtypeStruct((TOP_K, t), jnp.int32),
            jax.ShapeDtypeStruct((1, nb_pad), jnp.int32),
            jax.ShapeDtypeStruct((1, LANES), jnp.int32),
            jax.ShapeDtypeStruct((n_e, LANES), jnp.int32),
        ],
        compiler_params=_params("arbitrary"),
        name="moe_slots",
    )(counts, eidx, rank)


TILE_ROWS = 4
U32 = jnp.uint32


def _pack_bf16_pairs(v):
    half = v.shape[1] // 2
    lo = lax.bitcast_convert_type(v[:, :half].astype(BF16).astype(F32), U32)
    hi = lax.bitcast_convert_type(v[:, half:].astype(BF16).astype(F32), U32)
    return hi | (lo >> 16)


def _unpack_bf16_pairs(u):
    lo = lax.bitcast_convert_type(u << 16, F32)
    hi = lax.bitcast_convert_type(u & U32(0xFFFF0000), F32)
    return jnp.concatenate([lo, hi], axis=1)


def _tile_copy(src, src_tok, dst, dst_tok, sem):
    return pltpu.make_async_copy(
        src.at[pl.ds(pl.multiple_of(src_tok * TILE_ROWS, TILE_ROWS), TILE_ROWS)],
        dst.at[pl.ds(pl.multiple_of(dst_tok * TILE_ROWS, TILE_ROWS), TILE_ROWS)], sem)


def _to_token_tiles(ref, val, n_tok):
    for cidx in range(TILE_ROWS):
        ref[pl.ds(cidx, n_tok, stride=TILE_ROWS), :] = val[:, cidx * LANES:(cidx + 1) * LANES]


def _from_token_tiles(ref, n_tok):
    return jnp.concatenate(
        [ref[pl.ds(cidx, n_tok, stride=TILE_ROWS), :] for cidx in range(TILE_ROWS)], axis=1)


def _dispatch_kernel(dest_ref, seg_end_ref, seg_len_ref, nused_ref, x_ref, shift_ref, scale_ref,
                     xs_hbm, hs, zeros, sem, zsem, *, tm, t_total, block_rows, n_blocks):
    base = pl.program_id(0) * tm
    blk = block_rows * TILE_ROWS

    @pl.when(pl.program_id(0) == 0)
    def _():
        zeros[...] = jnp.zeros_like(zeros)

        def zero_block(first_tok):
            return pltpu.make_async_copy(
                zeros, xs_hbm.at[pl.ds(pl.multiple_of(first_tok * TILE_ROWS, blk), blk)], zsem)

        def seg_start(e, carry):
            @pl.when(seg_len_ref[e] > 0)
            def _():
                zero_block(seg_end_ref[e] - block_rows).start()
            return carry

        def seg_wait(e, carry):
            @pl.when(seg_len_ref[e] > 0)
            def _():
                zero_block(seg_end_ref[e] - block_rows).wait()
            return carry

        def tail_start(j, carry):
            zero_block(j * block_rows).start()
            return carry

        def tail_wait(j, carry):
            zero_block(j * block_rows).wait()
            return carry

        lax.fori_loop(0, N_EXPERTS, seg_start, 0)
        lax.fori_loop(nused_ref[0], n_blocks, tail_start, 0)
        lax.fori_loop(0, N_EXPERTS, seg_wait, 0)
        lax.fori_loop(nused_ref[0], n_blocks, tail_wait, 0)

    _to_token_tiles(hs, _pack_bf16_pairs(x_ref[...] * (1.0 + scale_ref[0]) + shift_ref[0]), tm)

    def issue(t, carry):
        for k in range(TOP_K):
            _tile_copy(hs, t, xs_hbm, dest_ref[k * t_total + base + t], sem).start()
        return carry

    lax.fori_loop(0, tm, issue, 0, unroll=2)
    for k in range(TOP_K):
        pltpu.make_async_copy(hs, xs_hbm.at[pl.ds(0, tm * TILE_ROWS)], sem).wait()


def moe_dispatch(dest_flat, seg_end, seg_len, nused, xt, shift, scale, n_blocks, block_rows,
                 s_len, tm=256):
    t, d = xt.shape
    per_b = s_len // tm
    grid_spec = pltpu.PrefetchScalarGridSpec(
        num_scalar_prefetch=4,
        grid=(t // tm,),
        in_specs=[
            pl.BlockSpec((tm, d), lambda i, *_: (i, 0)),
            pl.BlockSpec((1, 1, d), lambda i, *_: (i // per_b, 0, 0)),
            pl.BlockSpec((1, 1, d), lambda i, *_: (i // per_b, 0, 0)),
        ],
        out_specs=pl.BlockSpec(memory_space=pl.ANY),
        scratch_shapes=[pltpu.VMEM((tm * TILE_ROWS, LANES), U32),
                        pltpu.VMEM((block_rows * TILE_ROWS, LANES), U32),
                        pltpu.SemaphoreType.DMA, pltpu.SemaphoreType.DMA],
    )
    assert d == 2 * TILE_ROWS * LANES
    return pl.pallas_call(
        functools.partial(_dispatch_kernel, tm=tm, t_total=t, block_rows=block_rows,
                          n_blocks=n_blocks),
        grid_spec=grid_spec,
        out_shape=jax.ShapeDtypeStruct((n_blocks * block_rows * TILE_ROWS, LANES), U32),
        compiler_params=_params("arbitrary"),
        name="moe_dispatch",
    )(dest_flat, seg_end, seg_len, nused, xt, shift, scale)


def _expert_kernel(blk_e_ref, nused_ref, xs_ref, w1_ref, w3_ref, w2_ref, ys_ref,
                   w1b, w3b, w2b, *, block_rows):
    i = pl.program_id(0)
    prev = blk_e_ref[jnp.maximum(i - 1, 0)]

    @pl.when((i == 0) | (blk_e_ref[i] != prev))
    def _():
        w1b[...] = w1_ref[0].astype(BF16)
        w3b[...] = w3_ref[0].astype(BF16)
        w2b[...] = w2_ref[0].astype(BF16)

    @pl.when(i < nused_ref[0])
    def _():
        xb = _unpack_bf16_pairs(_from_token_tiles(xs_ref, block_rows)).astype(BF16)
        h1 = jnp.dot(xb, w1b[...], preferred_element_type=F32)
        h3 = jnp.dot(xb, w3b[...], preferred_element_type=F32)
        act = (h1 * _sigmoid(h1)) * h3
        y = jnp.dot(act.astype(BF16), w2b[...], preferred_element_type=F32)
        _to_token_tiles(ys_ref, _pack_bf16_pairs(y), block_rows)

    @pl.when(i >= nused_ref[0])
    def _():
        ys_ref[...] = jnp.zeros_like(ys_ref)


def moe_experts(blk_e, nused, xs, w1, w3, w2, block_rows, first_expert):
    d, de = w1.shape[-2:]
    n_rows = xs.shape[0] // TILE_ROWS
    n_blocks = n_rows // block_rows

    def row_map(i, be, nu):
        return (jnp.minimum(i, nu[0] - 1), 0)

    def w_map(i, be, nu):
        return (be[i] + first_expert, 0, 0)

    grid_spec = pltpu.PrefetchScalarGridSpec(
        num_scalar_prefetch=2,
        grid=(n_blocks,),
        in_specs=[
            pl.BlockSpec((block_rows * TILE_ROWS, LANES), row_map),
            pl.BlockSpec((1, d, de), w_map),
            pl.BlockSpec((1, d, de), w_map),
            pl.BlockSpec((1, de, d), w_map),
        ],
        out_specs=pl.BlockSpec((block_rows * TILE_ROWS, LANES), lambda i, be, nu: (i, 0)),
        scratch_shapes=[pltpu.VMEM((d, de), BF16), pltpu.VMEM((d, de), BF16),
                        pltpu.VMEM((de, d), BF16)],
    )
    return pl.pallas_call(
        functools.partial(_expert_kernel, block_rows=block_rows),
        grid_spec=grid_spec,
        out_shape=jax.ShapeDtypeStruct((n_rows * TILE_ROWS, LANES), U32),
        compiler_params=_params("arbitrary"),
        name="moe_experts",
    )(blk_e, nused, xs, w1, w3, w2)


def _combine_kernel(dest_ref, ys_hbm, x_ref, shift_ref, scale_ref, gate_ref, gcol_ref,
                    ws1_ref, ws3_ref, ws2_ref, g_ref, b_ref, o_ref, buf, sem,
                    *, tm, t_total, alpha):
    step = pl.program_id(0)
    slot = step % 2

    def start_gather(for_step, into):
        base = for_step * tm

        def issue(t, carry):
            for k in range(TOP_K):
                _tile_copy(ys_hbm, dest_ref[k * t_total + base + t], buf.at[into, k], t,
                           sem.at[into]).start()
            return carry

        lax.fori_loop(0, tm, issue, 0, unroll=2)

    @pl.when(step == 0)
    def _():
        start_gather(0, 0)

    @pl.when(step + 1 < pl.num_programs(0))
    def _():
        start_gather(step + 1, 1 - slot)

    x = x_ref[...]
    hb = (x * (1.0 + scale_ref[0]) + shift_ref[0]).astype(BF16)
    h1 = jnp.dot(hb, ws1_ref[...], preferred_element_type=F32)
    h3 = jnp.dot(hb, ws3_ref[...], preferred_element_type=F32)
    act = (h1 * _sigmoid(h1)) * h3
    y = jnp.dot(act.astype(BF16), ws2_ref[...], preferred_element_type=F32)

    for k in range(TOP_K):
        pltpu.make_async_copy(ys_hbm.at[pl.ds(0, tm * TILE_ROWS)], buf.at[slot, k],
                              sem.at[slot]).wait()

    gcol = gcol_ref[...]
    for k in range(TOP_K):
        y = y + _unpack_bf16_pairs(_from_token_tiles(buf.at[slot, k], tm)) * gcol[:, k:k + 1]
    z = alpha * x + gate_ref[0] * y
    o_ref[...] = _layer_norm(z, g_ref[...], b_ref[...])


def moe_combine(dest_flat, ys, xt, shift, scale, gate1p, gcol, ws1, ws3, ws2, ln_g, ln_b,
                s_len, alpha, tm=128):
    t, d = xt.shape
    ds_ = ws1.shape[1]
    per_b = s_len // tm

    def const(shape):
        return pl.BlockSpec(shape, lambda i, dest: (0,) * len(shape))

    mod_spec = pl.BlockSpec((1, 1, d), lambda i, dest: (i // per_b, 0, 0))
    grid_spec = pltpu.PrefetchScalarGridSpec(
        num_scalar_prefetch=1,
        grid=(t // tm,),
        in_specs=[
            pl.BlockSpec(memory_space=pl.ANY),
            pl.BlockSpec((tm, d), lambda i, dest: (i, 0)),
            mod_spec, mod_spec, mod_spec,
            pl.BlockSpec((tm, LANES), lambda i, dest: (i, 0)),
            const((d, ds_)), const((d, ds_)), const((ds_, d)),
            const((1, d)), const((1, d)),
        ],
        out_specs=pl.BlockSpec((tm, d), lambda i, dest: (i, 0)),
        scratch_shapes=[pltpu.VMEM((2, TOP_K, tm * TILE_ROWS, LANES), U32),
                        pltpu.SemaphoreType.DMA((2,))],
    )
    return pl.pallas_call(
        functools.partial(_combine_kernel, tm=tm, t_total=t, alpha=alpha),
        grid_spec=grid_spec,
        out_shape=jax.ShapeDtypeStruct((t, d), F32),
        compiler_params=_params("arbitrary"),
        name="moe_combine",
    )(dest_flat, ys, xt, shift, scale, gate1p, gcol, ws1, ws3, ws2,
      ln_g.reshape(1, d), ln_b.reshape(1, d))


def moe_layer(x, mods, router_w, router_bias, w1, w3, w2, ws1, ws3, ws2, ln_g, ln_b, alpha,
              layer):
    b, s, d = x.shape
    t = b * s
    shift = mods[:, None, 0:d]
    scale = mods[:, None, d:2 * d]
    gate1p = 1.0 + mods[:, None, 2 * d:3 * d]
    xt = x.reshape(t, d)
    n_blocks = t * TOP_K // MOE_BLOCK + N_EXPERTS
    eidx, rank, gcol, counts = moe_route(xt, shift, scale, router_w, router_bias, s)
    dest, blk_e, nused, seg = moe_slots(counts, eidx, rank, n_blocks, MOE_BLOCK)
    dest_flat = dest.reshape(-1)
    nused = nused.reshape(-1)
    xs = moe_dispatch(dest_flat, seg[:, 0], seg[:, 1], nused, xt, shift, scale, n_blocks,
                      MOE_BLOCK, s)
    n_e = w1.shape[1]
    ys = moe_experts(blk_e.reshape(-1), nused, xs,
                     w1.reshape((-1,) + w1.shape[2:]), w3.reshape((-1,) + w3.shape[2:]),
                     w2.reshape((-1,) + w2.shape[2:]), MOE_BLOCK, layer * n_e)
    out = moe_combine(dest_flat, ys, xt, shift, scale, gate1p, gcol,
                      ws1.astype(BF16), ws3.astype(BF16), ws2.astype(BF16), ln_g, ln_b, s, alpha)
    return out.reshape(b, s, d)


def kernel(x, c, positions, ada_w, ada_b, ln_g, ln_b, attn_w_in, attn_w_out, gdn_w_in, gdn_conv_w,
           gdn_a_log, gdn_dt_bias, gdn_norm_w, gdn_w_out, router_w, router_bias, expert_w1,
           expert_w3, expert_w2, shared_w1, shared_w3, shared_w2):
    depth = ada_w.shape[0]
    alpha = (2 * depth) ** 0.25
    mods = ada_vectors(c, ada_w, ada_b)
    cos_t, sin_t = rope_tables(positions)
    for i in range(depth):
        j = i // 2
        if i % 2 == 0:
            x = attention_layer(x, mods[2 * i], cos_t, sin_t, attn_w_in[j], attn_w_out[j],
                                ln_g[i, 0], ln_b[i, 0], alpha)
        else:
            x = gdn_layer(x, mods[2 * i], gdn_w_in[j], gdn_conv_w[j], gdn_a_log[j],
                          gdn_dt_bias[j], gdn_norm_w[j], gdn_w_out[j], ln_g[i, 0], ln_b[i, 0],
                          alpha)
        x = moe_layer(x, mods[2 * i + 1], router_w[i], router_bias[i], expert_w1,
                      expert_w3, expert_w2, shared_w1[i], shared_w3[i], shared_w2[i],
                      ln_g[i, 1], ln_b[i, 1], alpha, i)
    return x
```

```python
import functools
import math

import jax
import jax.numpy as jnp
from jax import lax
from jax.experimental import pallas as pl
from jax.experimental.pallas import tpu as pltpu

F32 = jnp.float32
BF16 = jnp.bfloat16

LANES = 128

DILATED_GROUPS = ((128, 1), (512, 4), (2048, 16))
A_HEADS = 16
A_HEAD_DIM = 64
ATTN_BLOCK = 128
ROPE_THETA = 10000.0
GDN_HEADS = 8
GDN_HEAD_DIM = 128
CONV_WIDTH = 4
GDN_CHUNK = 64
N_EXPERTS = 64
TOP_K = 8
N_EXPERT_GROUPS = 8
TOPK_GROUPS = 4
ROUTED_SCALE = 2.5
MOE_BLOCK = 512
LN_EPS = 1e-5
RMS_EPS = 1e-6
NEG_BIG = -1e30

VMEM_LIMIT = 56 * 1024 * 1024


def _params(*sem):
    return pltpu.CompilerParams(dimension_semantics=sem, vmem_limit_bytes=VMEM_LIMIT)


def _ada_kernel(c_ref, w_ref, b_ref, o_ref):
    c = c_ref[...]
    c_hi = c.astype(BF16)
    c_lo = (c - c_hi.astype(F32)).astype(BF16)
    w = w_ref[0]
    w_hi = w.astype(BF16)
    w_lo = (w - w_hi.astype(F32)).astype(BF16)
    acc = jnp.dot(c_hi, w_hi, preferred_element_type=F32)
    acc += jnp.dot(c_hi, w_lo, preferred_element_type=F32)
    acc += jnp.dot(c_lo, w_hi, preferred_element_type=F32)
    o_ref[0] = acc + b_ref[0]


def ada_vectors(c, ada_w, ada_b):
    depth2 = ada_w.shape[0] * ada_w.shape[1]
    b, d = c.shape
    d3 = ada_w.shape[-1]
    tn = d
    w = ada_w.reshape(depth2, d, d3)
    bias = ada_b.reshape(depth2, 1, d3)
    return pl.pallas_call(
        _ada_kernel,
        grid=(depth2, d3 // tn),
        in_specs=[
            pl.BlockSpec((b, d), lambda l, j: (0, 0)),
            pl.BlockSpec((1, d, tn), lambda l, j: (l, 0, j)),
            pl.BlockSpec((1, 1, tn), lambda l, j: (l, 0, j)),
        ],
        out_specs=pl.BlockSpec((1, b, tn), lambda l, j: (l, 0, j)),
        out_shape=jax.ShapeDtypeStruct((depth2, b, d3), F32),
        compiler_params=_params("parallel", "parallel"),
        name="ada_vectors",
    )(c, w, bias)


def _rope_kernel(pos_ref, freq_ref, sign_ref, cos_ref, sin_ref):
    ang = pos_ref[0].astype(F32) * freq_ref[...]
    cos_ref[0] = jnp.cos(ang)
    sin_ref[0] = jnp.sin(ang) * sign_ref[...]


def rope_tables(positions):
    b, s = positions.shape
    half = A_HEAD_DIM // 2
    inv_freq = ROPE_THETA ** (-jnp.arange(half, dtype=F32) / half)
    freq = jnp.tile(inv_freq, LANES // half).reshape(1, LANES)
    sign = jnp.tile(jnp.concatenate([-jnp.ones((half,), F32), jnp.ones((half,), F32)]),
                    LANES // A_HEAD_DIM).reshape(1, LANES)
    ts = min(s, 1024)
    return pl.pallas_call(
        _rope_kernel,
        grid=(b, s // ts),
        in_specs=[
            pl.BlockSpec((1, ts, 1), lambda i, j: (i, j, 0)),
            pl.BlockSpec((1, LANES), lambda i, j: (0, 0)),
            pl.BlockSpec((1, LANES), lambda i, j: (0, 0)),
        ],
        out_specs=[pl.BlockSpec((1, ts, LANES), lambda i, j: (i, j, 0))] * 2,
        out_shape=[jax.ShapeDtypeStruct((b, s, LANES), F32)] * 2,
        compiler_params=_params("parallel", "parallel"),
        name="rope_tables",
    )(positions.reshape(b, s, 1), freq, sign)


def _residue_major(ref, lead, dil, tl):
    if dil == 1:
        return ref[lead] if lead is not None else ref[...]
    parts = []
    for r in range(dil):
        if lead is None:
            parts.append(ref[pl.ds(r, tl, stride=dil), :])
        else:
            parts.append(ref[lead, pl.ds(r, tl, stride=dil), :])
    return jnp.concatenate(parts, axis=0)


def _attn_proj_kernel(x_ref, shift_ref, scale_ref, cos_ref, sin_ref, w_ref,
                      q_ref, k_ref, v_ref, h_scr, *, dil, tm):
    tl = tm // dil
    d = x_ref.shape[-1]
    h = x_ref[0] * (1.0 + scale_ref[0]) + shift_ref[0]
    if dil == 1:
        hb = h.astype(BF16)
    else:
        for cidx in range(d // LANES):
            h_scr[cidx] = h[:, cidx * LANES:(cidx + 1) * LANES]
        hb = jnp.concatenate(
            [_residue_major(h_scr, cidx, dil, tl).astype(BF16) for cidx in range(d // LANES)],
            axis=1)
    cosp = _residue_major(cos_ref, 0, dil, tl)
    sinp = _residue_major(sin_ref, 0, dil, tl)
    lane = lax.broadcasted_iota(jnp.int32, (tm, LANES), 1)
    first_half = (lane % A_HEAD_DIM) < (A_HEAD_DIM // 2)
    qscale = A_HEAD_DIM ** -0.5
    for j, out_ref in enumerate((q_ref, k_ref, v_ref)):
        res = jnp.dot(hb, w_ref[:, j * d:(j + 1) * d], preferred_element_type=F32)
        for hp in range(d // LANES):
            blk = res[:, hp * LANES:(hp + 1) * LANES]
            if j < 2:
                swapped = jnp.where(first_half,
                                    pltpu.roll(blk, LANES - A_HEAD_DIM // 2, 1),
                                    pltpu.roll(blk, A_HEAD_DIM // 2, 1))
                blk = blk * cosp + swapped * sinp
                if j == 0:
                    blk = blk * qscale
            blk = blk.astype(BF16)
            for r in range(dil):
                out_ref[0, hp, r] = blk[r * tl:(r + 1) * tl]


def attn_project(x, shift, scale, cos_t, sin_t, w_g, dil, tm=512):
    b, s, d = x.shape
    tl = tm // dil
    n_hp = d // LANES
    out_sds = jax.ShapeDtypeStruct((b, n_hp, dil, s // dil, LANES), BF16)
    out_spec = pl.BlockSpec((1, n_hp, dil, tl, LANES), lambda i, j: (i, 0, 0, j, 0))
    return pl.pallas_call(
        functools.partial(_attn_proj_kernel, dil=dil, tm=tm),
        grid=(b, s // tm),
        in_specs=[
            pl.BlockSpec((1, tm, d), lambda i, j: (i, j, 0)),
            pl.BlockSpec((1, 1, d), lambda i, j: (i, 0, 0)),
            pl.BlockSpec((1, 1, d), lambda i, j: (i, 0, 0)),
            pl.BlockSpec((1, tm, LANES), lambda i, j: (i, j, 0)),
            pl.BlockSpec((1, tm, LANES), lambda i, j: (i, j, 0)),
            pl.BlockSpec((d, 3 * d), lambda i, j: (0, 0)),
        ],
        out_specs=[out_spec] * 3,
        out_shape=[out_sds] * 3,
        scratch_shapes=[pltpu.VMEM((d // LANES, tm, LANES), F32)],
        compiler_params=_params("parallel", "parallel"),
        name=f"attn_project_d{dil}",
    )(x, shift, scale, cos_t, sin_t, w_g)


def _attn_kernel(*refs, s_len, dils):
    n_g = len(dils)
    qkv = refs[:3 * n_g]
    o_ref = refs[3 * n_g]
    oacc, lacc = refs[3 * n_g + 1:]
    blk = ATTN_BLOCK

    lane = lax.broadcasted_iota(jnp.int32, (blk, LANES), 1)
    head0 = lane < A_HEAD_DIM
    row = lax.broadcasted_iota(jnp.int32, (2 * blk, 2 * blk), 0) % blk
    col = lax.broadcasted_iota(jnp.int32, (2 * blk, 2 * blk), 1)
    bias_rest = jnp.where((col >= row) & (col <= row + blk), 0.0, NEG_BIG).astype(F32)
    bias_first = bias_rest[:, blk:]

    def run_blocks(g, q_ref, k_ref, v_ref, dil, specs):
        ids = range(len(specs))
        qq, kc, vc = [], [], []
        for r, n, first in specs:
            q = q_ref[0, 0, r, pl.ds(pl.multiple_of(n * blk, blk), blk), :]
            zero = jnp.zeros_like(q)
            qq.append(jnp.concatenate([jnp.where(head0, q, zero), jnp.where(head0, zero, q)],
                                      axis=0))
            if first:
                keys = pl.ds(0, blk)
            else:
                keys = pl.ds(pl.multiple_of((n - 1) * blk, blk), 2 * blk)
            kc.append(k_ref[0, 0, r, keys, :])
            vc.append(v_ref[0, 0, r, keys, :])
        sc = [lax.dot_general(qq[i], kc[i], (((1,), (1,)), ((), ())), preferred_element_type=F32)
              + (bias_first if specs[i][2] else bias_rest) for i in ids]
        m = [jnp.max(sc[i], axis=1, keepdims=True) for i in ids]
        p = [jnp.exp(sc[i] - m[i]) for i in ids]
        l = [jnp.sum(p[i], axis=1, keepdims=True) for i in ids]
        pv = [jnp.dot(p[i].astype(BF16), vc[i], preferred_element_type=F32) * (1.0 / l[i])
              for i in ids]
        for i in ids:
            r, n, _ = specs[i]
            lse = m[i] + jnp.log(l[i])
            o_blk = jnp.where(head0, pv[i][:blk], pv[i][blk:])
            lse_blk = jnp.where(head0, jnp.broadcast_to(lse[:blk], (blk, LANES)),
                                jnp.broadcast_to(lse[blk:], (blk, LANES)))
            start = n * (blk * dil) + r
            if dil == 1:
                idx = pl.ds(pl.multiple_of(start, blk), blk)
            else:
                idx = pl.ds(start, blk, stride=dil)
            oacc[g, idx, :] = o_blk
            lacc[g, idx, :] = lse_blk

    for g, dil in enumerate(dils):
        q_ref, k_ref, v_ref = qkv[3 * g:3 * g + 3]
        nb = s_len // dil // blk

        def per_step(rr, carry, g=g, dil=dil, nb=nb, q_ref=q_ref, k_ref=k_ref, v_ref=v_ref):
            if dil == 1:
                res, width = [0], 4
            else:
                res, width = [2 * rr, 2 * rr + 1], 2
            if nb < width:
                run_blocks(g, q_ref, k_ref, v_ref, dil, [(r, 0, True) for r in res])
                return carry
            run_blocks(g, q_ref, k_ref, v_ref, dil,
                       [(r, n, n == 0) for r in res for n in range(width)])

            def per_group(i, c2):
                run_blocks(g, q_ref, k_ref, v_ref, dil,
                           [(r, width * i + n, False) for r in res for n in range(width)])
                return c2

            lax.fori_loop(1, nb // width, per_group, 0)
            return carry

        lax.fori_loop(0, max(dil // 2, 1), per_step, 0)

    tc = 256

    def mix(i, carry):
        rows = pl.ds(pl.multiple_of(i * tc, tc), tc)
        ls = [lacc[g, rows, :] for g in range(n_g)]
        mx = functools.reduce(jnp.maximum, ls)
        es = [jnp.exp(l - mx) for l in ls]
        den = functools.reduce(lambda a, b2: a + b2, es)
        num = es[0] * oacc[0, rows, :]
        for g in range(1, n_g):
            num = num + es[g] * oacc[g, rows, :]
        o_ref[0, 0, rows, :] = (num * (1.0 / den)).astype(o_ref.dtype)
        return carry

    lax.fori_loop(0, s_len // tc, mix, 0)


def attn_core(qkvs, dils, s_len):
    b, n_hp = qkvs[0].shape[:2]
    in_specs = []
    for g, dil in enumerate(dils):
        spec = pl.BlockSpec((1, 1, dil, s_len // dil, LANES), lambda i, j: (i, j, 0, 0, 0))
        in_specs += [spec] * 3
    n_g = len(dils)
    return pl.pallas_call(
        functools.partial(_attn_kernel, s_len=s_len, dils=tuple(dils)),
        grid=(b, n_hp),
        in_specs=in_specs,
        out_specs=pl.BlockSpec((1, 1, s_len, LANES), lambda i, j: (i, j, 0, 0)),
        out_shape=jax.ShapeDtypeStruct((b, n_hp, s_len, LANES), BF16),
        scratch_shapes=[pltpu.VMEM((n_g, s_len, LANES), F32),
                        pltpu.VMEM((n_g, s_len, LANES), F32)],
        compiler_params=_params("parallel", "parallel"),
        name="attn_core",
    )(*qkvs)


def _layer_norm(z, g, b):
    mu = jnp.mean(z, axis=-1, keepdims=True)
    zc = z - mu
    var = jnp.mean(zc * zc, axis=-1, keepdims=True)
    return zc * lax.rsqrt(var + LN_EPS) * g + b


def _out_ln_kernel(o_ref, w_ref, x_ref, gate_ref, g_ref, b_ref, y_ref, *, alpha):
    n_hp = o_ref.shape[1]
    o = jnp.concatenate([o_ref[0, hp] for hp in range(n_hp)], axis=-1)
    y = jnp.dot(o, w_ref[...], preferred_element_type=F32)
    z = alpha * x_ref[0] + gate_ref[0] * y
    y_ref[0] = _layer_norm(z, g_ref[...], b_ref[...])


def out_proj_ln(o_heads, w_out, x, gate1p, ln_g, ln_b, alpha, tm=512):
    b, s, d = x.shape
    n_hp = o_heads.shape[1]
    return pl.pallas_call(
        functools.partial(_out_ln_kernel, alpha=alpha),
        grid=(b, s // tm),
        in_specs=[
            pl.BlockSpec((1, n_hp, tm, LANES), lambda i, j: (i, 0, j, 0)),
            pl.BlockSpec((d, d), lambda i, j: (0, 0)),
            pl.BlockSpec((1, tm, d), lambda i, j: (i, j, 0)),
            pl.BlockSpec((1, 1, d), lambda i, j: (i, 0, 0)),
            pl.BlockSpec((1, d), lambda i, j: (0, 0)),
            pl.BlockSpec((1, d), lambda i, j: (0, 0)),
        ],
        out_specs=pl.BlockSpec((1, tm, d), lambda i, j: (i, j, 0)),
        out_shape=jax.ShapeDtypeStruct((b, s, d), F32),
        compiler_params=_params("parallel", "parallel"),
        name="out_proj_ln",
    )(o_heads, w_out, x, gate1p, ln_g.reshape(1, d), ln_b.reshape(1, d))


def attention_layer(x, mods, cos_t, sin_t, w_in, w_out, ln_g, ln_b, alpha):
    b, s, d = x.shape
    shift = mods[:, None, 0:d]
    scale = mods[:, None, d:2 * d]
    gate1p = 1.0 + mods[:, None, 2 * d:3 * d]
    w_in = w_in.astype(BF16)
    dils = [dil for _, dil in DILATED_GROUPS]
    qkvs = []
    for g, dil in enumerate(dils):
        qkvs += attn_project(x, shift, scale, cos_t, sin_t,
                             w_in[:, g * 3 * d:(g + 1) * 3 * d], dil)
    o = attn_core(qkvs, dils, s)
    return out_proj_ln(o, w_out.astype(BF16), x, gate1p, ln_g, ln_b, alpha)


def _split_bf16(a):
    hi = a.astype(BF16)
    lo = (a - hi.astype(F32)).astype(BF16)
    return hi, lo


def _gdn_proj_kernel(x_ref, shift_ref, scale_ref, w_ref, wab_hi_ref, wab_lo_ref,
                     wabt_hi_ref, wabt_lo_ref, q_ref, k_ref, v_ref, z_ref, ab_ref, abt_ref, *, tm):
    d = x_ref.shape[-1]
    h = x_ref[0] * (1.0 + scale_ref[0]) + shift_ref[0]
    h_hi, h_lo = _split_bf16(h)
    for j, out_ref in enumerate((q_ref, k_ref, v_ref, z_ref)):
        res = jnp.dot(h_hi, w_ref[:, j * d:(j + 1) * d], preferred_element_type=F32)
        for hd in range(d // LANES):
            out_ref[0, hd] = res[:, hd * LANES:(hd + 1) * LANES].astype(BF16)
    ab = jnp.dot(h_hi, wab_hi_ref[...], preferred_element_type=F32)
    ab += jnp.dot(h_lo, wab_hi_ref[...], preferred_element_type=F32)
    ab += jnp.dot(h_hi, wab_lo_ref[...], preferred_element_type=F32)
    ab_ref[0] = ab
    nt = (((1,), (1,)), ((), ()))
    abt = lax.dot_general(wabt_hi_ref[...], h_hi, nt, preferred_element_type=F32)
    abt += lax.dot_general(wabt_hi_ref[...], h_lo, nt, preferred_element_type=F32)
    abt += lax.dot_general(wabt_lo_ref[...], h_hi, nt, preferred_element_type=F32)
    for n in range(tm // GDN_CHUNK):
        abt_ref[0, n] = abt[:, n * GDN_CHUNK:(n + 1) * GDN_CHUNK]


def gdn_project(x, shift, scale, w_in, tm=512):
    b, s, d = x.shape
    n_h = d // LANES
    w_main = w_in[:, :4 * d].astype(BF16)
    w_ab = jnp.pad(w_in[:, 4 * d:], ((0, 0), (0, LANES - 2 * GDN_HEADS)))
    w_abt = w_in[:, 4 * d:].T
    wab_hi, wab_lo = _split_bf16(w_ab)
    wabt_hi, wabt_lo = _split_bf16(w_abt)
    head_sds = jax.ShapeDtypeStruct((b, n_h, s, LANES), BF16)
    head_spec = pl.BlockSpec((1, n_h, tm, LANES), lambda i, j: (i, 0, j, 0))
    n_ab = 2 * GDN_HEADS
    return pl.pallas_call(
        functools.partial(_gdn_proj_kernel, tm=tm),
        grid=(b, s // tm),
        in_specs=[
            pl.BlockSpec((1, tm, d), lambda i, j: (i, j, 0)),
            pl.BlockSpec((1, 1, d), lambda i, j: (i, 0, 0)),
            pl.BlockSpec((1, 1, d), lambda i, j: (i, 0, 0)),
            pl.BlockSpec((d, 4 * d), lambda i, j: (0, 0)),
            pl.BlockSpec((d, LANES), lambda i, j: (0, 0)),
            pl.BlockSpec((d, LANES), lambda i, j: (0, 0)),
            pl.BlockSpec((n_ab, d), lambda i, j: (0, 0)),
            pl.BlockSpec((n_ab, d), lambda i, j: (0, 0)),
        ],
        out_specs=[head_spec] * 4 + [
            pl.BlockSpec((1, tm, LANES), lambda i, j: (i, j, 0)),
            pl.BlockSpec((1, tm // GDN_CHUNK, n_ab, GDN_CHUNK), lambda i, j: (i, j, 0, 0)),
        ],
        out_shape=[head_sds] * 4 + [
            jax.ShapeDtypeStruct((b, s, LANES), F32),
            jax.ShapeDtypeStruct((b, s // GDN_CHUNK, n_ab, GDN_CHUNK), F32),
        ],
        compiler_params=_params("parallel", "parallel"),
        name="gdn_project",
    )(x, shift, scale, w_main, wab_hi, wab_lo, wabt_hi, wabt_lo)


def _softplus(t):
    return jnp.maximum(t, 0.0) + jnp.log(1.0 + jnp.exp(-jnp.abs(t)))


def _sigmoid(t):
    return 1.0 / (1.0 + jnp.exp(-t))


def _dot3_exact_lhs(lhs_bf16, rhs_f32):
    acc = None
    rem = rhs_f32
    for _ in range(3):
        part = rem.astype(BF16)
        rem = rem - part.astype(F32)
        t = jnp.dot(lhs_bf16, part, preferred_element_type=F32)
        acc = t if acc is None else acc + t
    return acc


def _dot3_exact_rhs(lhs_f32, rhs_bf16):
    acc = None
    rem = lhs_f32
    for _ in range(3):
        part = rem.astype(BF16)
        rem = rem - part.astype(F32)
        t = jnp.dot(part, rhs_bf16, preferred_element_type=F32)
        acc = t if acc is None else acc + t
    return acc


def _gdn_kernel(q_ref, k_ref, v_ref, z_ref, ab_ref, abt_ref, convw_ref, alog_l_ref, dtb_l_ref,
                alog_c_ref, dtb_c_ref, normw_ref, o_ref,
                state, carry, qs, ks, vs, gcol_s, beta_s, *, tc):
    n_h = GDN_HEADS
    ch = GDN_CHUNK
    dh = GDN_HEAD_DIM
    halo = 8

    @pl.when(pl.program_id(1) == 0)
    def _():
        state[...] = jnp.zeros_like(state)
        carry[:, :, 0:halo, :] = jnp.zeros((3, n_h, halo, dh), F32)

    for a, (src, dst) in enumerate(((q_ref, qs), (k_ref, ks), (v_ref, vs))):
        for hd in range(n_h):
            xb = carry.at[a, hd]
            xb[halo:halo + tc, :] = src[0, hd].astype(F32)
            w = convw_ref[a, hd]
            y = xb[halo:halo + tc, :] * w[CONV_WIDTH - 1:CONV_WIDTH]
            for j in range(CONV_WIDTH - 1):
                off = halo - (CONV_WIDTH - 1) + j
                y = y + xb[off:off + tc, :] * w[j:j + 1]
            xb[0:halo, :] = xb[tc:tc + halo, :]
            y = y * _sigmoid(y)
            if a < 2:
                y = y * lax.rsqrt(jnp.sum(y * y, axis=-1, keepdims=True) + RMS_EPS)
                if a == 0:
                    y = y * (dh ** -0.5)
            dst[hd] = y

    ab = ab_ref[0]
    gcol_s[...] = -jnp.exp(alog_l_ref[...]) * _softplus(ab + dtb_l_ref[...])
    beta_s[...] = _sigmoid(ab)

    ri = lax.broadcasted_iota(jnp.int32, (ch, ch), 0)
    ci = lax.broadcasted_iota(jnp.int32, (ch, ch), 1)
    causal = ri >= ci
    strict = ri > ci
    tril = jnp.where(causal, 1.0, 0.0).astype(BF16)
    triu = jnp.where(ri <= ci, 1.0, 0.0).astype(BF16)
    eye = jnp.where(ri == ci, 1.0, 0.0).astype(F32)
    nt = (((1,), (1,)), ((), ()))
    tn = (((0,), (0,)), ((), ()))

    n_par = 4 if (tc // ch) % 4 == 0 else 1
    heads = range(n_h)

    def chunk_group(gidx, carry_unused):
        cids = [gidx * n_par + j for j in range(n_par)]
        rows = [pl.ds(pl.multiple_of(c * ch, ch), ch) for c in cids]
        gc = [_dot3_exact_lhs(tril, gcol_s[r, :]) for r in rows]
        gc_t = []
        for c in cids:
            a_t = abt_ref[0, c][0:n_h, :]
            g_t = -jnp.exp(alog_c_ref[...]) * _softplus(a_t + dtb_c_ref[...])
            gc_t.append(_dot3_exact_rhs(g_t, triu))
        beta = [beta_s[r, :] for r in rows]
        items = [(j, hd) for j in range(n_par) for hd in heads]
        ids = range(len(items))
        gcol = [gc[j][:, hd:hd + 1] for j, hd in items]
        bcol = [beta[j][:, n_h + hd:n_h + hd + 1] for j, hd in items]
        decay = [jnp.exp(jnp.where(causal, gcol[i] - gc_t[j][hd:hd + 1, :], NEG_BIG))
                 for i, (j, hd) in enumerate(items)]
        qc = [qs[hd, rows[j], :] for j, hd in items]
        kc = [ks[hd, rows[j], :] for j, hd in items]
        kb = [kc[i] * bcol[i] for i in ids]
        both = [lax.dot_general(jnp.concatenate([kb[i], qc[i]], axis=0).astype(BF16),
                                kc[i].astype(BF16), nt, preferred_element_type=F32)
                for i in ids]
        lmat = [jnp.where(strict, both[i][:ch] * decay[i], 0.0) for i in ids]
        qk = [(both[i][ch:] * decay[i]).astype(BF16) for i in ids]
        pm = [-lmat[i] for i in ids]
        mm = lmat
        for _ in range(5):
            mb = [mm[i].astype(BF16) for i in ids]
            mm = [jnp.dot(mb[i], mb[i], preferred_element_type=F32) for i in ids]
            pm = [pm[i] + mm[i] + jnp.dot(pm[i].astype(BF16), mm[i].astype(BF16),
                                          preferred_element_type=F32) for i in ids]
        eg = [jnp.exp(gcol[i]) for i in ids]
        rhs = [jnp.concatenate([vs[hd, rows[j], :] * bcol[i], kb[i] * eg[i]], axis=1)
               for i, (j, hd) in enumerate(items)]
        sol = [rhs[i] + jnp.dot(pm[i].astype(BF16), rhs[i].astype(BF16),
                                preferred_element_type=F32) for i in ids]
        glast = [gcol[i][ch - 1:ch, :] for i in ids]
        kdec = [(kc[i] * jnp.exp(glast[i] - gcol[i])).astype(BF16) for i in ids]
        wqg = [jnp.concatenate([sol[i][:, dh:], qc[i] * eg[i]], axis=0).astype(BF16)
               for i in ids]
        for j in range(n_par):
            sel = [i for i in ids if items[i][0] == j]
            s_prev = [state[hd] for hd in heads]
            wq = [jnp.dot(wqg[i], s_prev[hd].astype(BF16), preferred_element_type=F32)
                  for hd, i in enumerate(sel)]
            vb = [(sol[i][:, :dh] - wq[hd][:ch]).astype(BF16) for hd, i in enumerate(sel)]
            o = [wq[hd][ch:] + jnp.dot(qk[i], vb[hd], preferred_element_type=F32)
                 for hd, i in enumerate(sel)]
            for hd, i in enumerate(sel):
                state[hd] = s_prev[hd] * jnp.exp(glast[i]) + lax.dot_general(
                    kdec[i], vb[hd], tn, preferred_element_type=F32)
            for hd in heads:
                on = o[hd] * lax.rsqrt(jnp.mean(o[hd] * o[hd], axis=-1, keepdims=True)
                                       + RMS_EPS) * normw_ref[...]
                zc = z_ref[0, hd, rows[j], :].astype(F32)
                o_ref[0, hd, rows[j], :] = (on * (zc * _sigmoid(zc))).astype(o_ref.dtype)
        return carry_unused

    lax.fori_loop(0, tc // ch // n_par, chunk_group, 0)


def gdn_core(q, k, v, z, ab, abt, conv_w, a_log, dt_bias, norm_w, tc=512):
    b, n_h, s, dh = q.shape
    n_ab = 2 * n_h
    convw = conv_w.reshape(CONV_WIDTH, 3, n_h, dh).transpose(1, 2, 0, 3)
    pad = (0, LANES - n_h)
    alog_l = jnp.pad(a_log.astype(F32), pad).reshape(1, LANES)
    dtb_l = jnp.pad(dt_bias.astype(F32), pad).reshape(1, LANES)
    alog_c = a_log.astype(F32).reshape(n_h, 1)
    dtb_c = dt_bias.astype(F32).reshape(n_h, 1)
    head_spec = pl.BlockSpec((1, n_h, tc, dh), lambda i, j: (i, 0, j, 0))

    def full(shape):
        return pl.BlockSpec(shape, lambda i, j: (0,) * len(shape))

    return pl.pallas_call(
        functools.partial(_gdn_kernel, tc=tc),
        grid=(b, s // tc),
        in_specs=[head_spec] * 4 + [
            pl.BlockSpec((1, tc, LANES), lambda i, j: (i, j, 0)),
            pl.BlockSpec((1, tc // GDN_CHUNK, n_ab, GDN_CHUNK), lambda i, j: (i, j, 0, 0)),
            full((3, n_h, CONV_WIDTH, dh)),
            full((1, LANES)), full((1, LANES)), full((n_h, 1)), full((n_h, 1)), full((1, dh)),
        ],
        out_specs=head_spec,
        out_shape=jax.ShapeDtypeStruct((b, n_h, s, dh), BF16),
        scratch_shapes=[
            pltpu.VMEM((n_h, dh, dh), F32),
            pltpu.VMEM((3, n_h, 8 + tc, dh), F32),
            pltpu.VMEM((n_h, tc, dh), F32),
            pltpu.VMEM((n_h, tc, dh), F32),
            pltpu.VMEM((n_h, tc, dh), F32),
            pltpu.VMEM((tc, LANES), F32),
            pltpu.VMEM((tc, LANES), F32),
        ],
        compiler_params=_params("parallel", "arbitrary"),
        name="gdn_core",
    )(q, k, v, z, ab, abt, convw, alog_l, dtb_l, alog_c, dtb_c, norm_w.reshape(1, dh).astype(F32))


def gdn_layer(x, mods, w_in, conv_w, a_log, dt_bias, norm_w, w_out, ln_g, ln_b, alpha):
    b, s, d = x.shape
    shift = mods[:, None, 0:d]
    scale = mods[:, None, d:2 * d]
    gate1p = 1.0 + mods[:, None, 2 * d:3 * d]
    q, k, v, z, ab, abt = gdn_project(x, shift, scale, w_in)
    o = gdn_core(q, k, v, z, ab, abt, conv_w, a_log, dt_bias, norm_w)
    return out_proj_ln(o, w_out.astype(BF16), x, gate1p, ln_g, ln_b, alpha)


def _first_argmax(vals, idx, n):
    mx = jnp.max(vals, axis=0, keepdims=True)
    first = jnp.min(jnp.where(vals == mx, idx, n), axis=0, keepdims=True)
    return mx, first


def _router_kernel(x_ref, shift_ref, scale_ref, rwt_hi_ref, rwt_lo_ref, bias_ref,
                   eidx_ref, rank_ref, gcol_ref, cnt_ref, cnt_scr, *, tm):
    n_e = N_EXPERTS
    per_g = n_e // N_EXPERT_GROUPS

    @pl.when(pl.program_id(0) == 0)
    def _():
        cnt_scr[...] = jnp.zeros_like(cnt_scr)

    h = x_ref[...] * (1.0 + scale_ref[0]) + shift_ref[0]
    h_hi, h_lo = _split_bf16(h)
    nt = (((1,), (1,)), ((), ()))
    logits = lax.dot_general(rwt_hi_ref[...], h_hi, nt, preferred_element_type=F32)
    logits += lax.dot_general(rwt_hi_ref[...], h_lo, nt, preferred_element_type=F32)
    logits += lax.dot_general(rwt_lo_ref[...], h_hi, nt, preferred_element_type=F32)
    scores = _sigmoid(logits)
    biased = scores + bias_ref[...]

    r8 = lax.broadcasted_iota(jnp.int32, (per_g, tm), 0)
    gscores = []
    for g in range(N_EXPERT_GROUPS):
        blk = biased[g * per_g:(g + 1) * per_g]
        m1, i1 = _first_argmax(blk, r8, per_g)
        m2 = jnp.max(jnp.where(r8 == i1, NEG_BIG, blk), axis=0, keepdims=True)
        gscores.append(m1 + m2)
    vals = jnp.concatenate(gscores, axis=0)
    rg = lax.broadcasted_iota(jnp.int32, (N_EXPERT_GROUPS, tm), 0)
    gsel = jnp.zeros((N_EXPERT_GROUPS, tm), F32)
    for _ in range(TOPK_GROUPS):
        _, idx = _first_argmax(vals, rg, N_EXPERT_GROUPS)
        hit = rg == idx
        gsel = jnp.where(hit, 1.0, gsel)
        vals = jnp.where(hit, NEG_BIG, vals)
    emask = jnp.concatenate(
        [jnp.broadcast_to(gsel[g:g + 1], (per_g, tm)) for g in range(N_EXPERT_GROUPS)], axis=0)
    masked = jnp.where(emask > 0.5, biased, NEG_BIG)

    re = lax.broadcasted_iota(jnp.int32, (n_e, tm), 0)
    hits, eidx, gates = [], [], []
    for _ in range(TOP_K):
        _, idx = _first_argmax(masked, re, n_e)
        hit = re == idx
        gates.append(jnp.sum(jnp.where(hit, scores, 0.0), axis=0, keepdims=True))
        masked = jnp.where(hit, NEG_BIG, masked)
        hits.append(hit)
        eidx.append(idx)
    gates = jnp.concatenate(gates, axis=0)
    gates = gates / jnp.sum(gates, axis=0, keepdims=True) * ROUTED_SCALE
    onehot = jnp.zeros((n_e, tm), F32)
    for hit in hits:
        onehot = jnp.where(hit, 1.0, onehot)

    ti = lax.broadcasted_iota(jnp.int32, (tm, tm), 0)
    tj = lax.broadcasted_iota(jnp.int32, (tm, tm), 1)
    before = jnp.where(ti < tj, 1.0, 0.0).astype(BF16)
    prefix = jnp.dot(onehot.astype(BF16), before, preferred_element_type=F32) + cnt_scr[...]
    ranks = [jnp.sum(jnp.where(hit, prefix, 0.0), axis=0, keepdims=True) for hit in hits]
    cnt_scr[...] = cnt_scr[...] + jnp.sum(onehot, axis=1, keepdims=True)

    eidx_ref[...] = jnp.concatenate(eidx, axis=0)
    rank_ref[...] = jnp.concatenate(ranks, axis=0).astype(jnp.int32)
    cnt_ref[...] = jnp.broadcast_to(cnt_scr[...], cnt_ref.shape)
    gpad = jnp.concatenate([gates, jnp.zeros((LANES - TOP_K, tm), F32)], axis=0)
    gcol_ref[...] = gpad.T


def moe_route(xt, shift, scale, router_w, router_bias, s_len, tm=512):
    t, d = xt.shape
    n_e = router_w.shape[1]
    rwt_hi, rwt_lo = _split_bf16(router_w.T)
    per_b = s_len // tm
    return pl.pallas_call(
        functools.partial(_router_kernel, tm=tm),
        grid=(t // tm,),
        in_specs=[
            pl.BlockSpec((tm, d), lambda i: (i, 0)),
            pl.BlockSpec((1, 1, d), lambda i: (i // per_b, 0, 0)),
            pl.BlockSpec((1, 1, d), lambda i: (i // per_b, 0, 0)),
            pl.BlockSpec((n_e, d), lambda i: (0, 0)),
            pl.BlockSpec((n_e, d), lambda i: (0, 0)),
            pl.BlockSpec((n_e, 1), lambda i: (0, 0)),
        ],
        out_specs=[
            pl.BlockSpec((TOP_K, tm), lambda i: (0, i)),
            pl.BlockSpec((TOP_K, tm), lambda i: (0, i)),
            pl.BlockSpec((tm, LANES), lambda i: (i, 0)),
            pl.BlockSpec((n_e, LANES), lambda i: (0, 0)),
        ],
        out_shape=[
            jax.ShapeDtypeStruct((TOP_K, t), jnp.int32),
            jax.ShapeDtypeStruct((TOP_K, t), jnp.int32),
            jax.ShapeDtypeStruct((t, LANES), F32),
            jax.ShapeDtypeStruct((n_e, LANES), F32),
        ],
        scratch_shapes=[pltpu.VMEM((n_e, 1), F32)],
        compiler_params=_params("arbitrary"),
        name="moe_route",
    )(xt, shift, scale, rwt_hi, rwt_lo, router_bias.reshape(n_e, 1).astype(F32))


def _slots_kernel(cnt_ref, eidx_ref, rank_ref, dest_ref, blk_e_ref, nused_ref, seg_ref,
                  *, block_rows):
    n_e = N_EXPERTS
    cnt = cnt_ref[...].astype(jnp.int32)
    padded = ((cnt + (block_rows - 1)) // block_rows) * block_rows
    ri = lax.broadcasted_iota(jnp.int32, (n_e, n_e), 0)
    ci = lax.broadcasted_iota(jnp.int32, (n_e, n_e), 1)
    tril = jnp.where(ri >= ci, 1.0, 0.0).astype(BF16)
    pend = _dot3_exact_lhs(tril, padded.astype(F32))
    pstart = pend.astype(jnp.int32) - padded
    eidx = eidx_ref[...]
    dest = rank_ref[...]
    for e in range(n_e):
        dest = dest + jnp.where(eidx == e, pstart[e:e + 1, 0:1], 0)
    dest_ref[...] = dest
    nb = blk_e_ref.shape[1]
    first_row = lax.broadcasted_iota(jnp.int32, (n_e, nb), 1) * block_rows
    pend_i = pend.astype(jnp.int32)[:, 0:1]
    owner = jnp.sum(jnp.where(pend_i <= first_row, 1, 0), axis=0, keepdims=True)
    blk_e_ref[...] = jnp.minimum(owner, n_e - 1)
    nused_ref[...] = pend_i[n_e - 1:n_e, :] // block_rows + jnp.zeros(nused_ref.shape, jnp.int32)
    lane = lax.broadcasted_iota(jnp.int32, seg_ref.shape, 1)
    seg_ref[...] = jnp.where(lane == 0, pend.astype(jnp.int32), padded)


def moe_slots(counts, eidx, rank, n_blocks, block_rows, tt=2048):
    t = eidx.shape[1]
    tt = min(tt, t)
    nb_pad = -(-n_blocks // LANES) * LANES
    n_e = counts.shape[0]
    return pl.pallas_call(
        functools.partial(_slots_kernel, block_rows=block_rows),
        grid=(t // tt,),
        in_specs=[
            pl.BlockSpec((n_e, LANES), lambda i: (0, 0)),
            pl.BlockSpec((TOP_K, tt), lambda i: (0, i)),
            pl.BlockSpec((TOP_K, tt), lambda i: (0, i)),
        ],
        out_specs=[
            pl.BlockSpec((TOP_K, tt), lambda i: (0, i)),
            pl.BlockSpec((1, nb_pad), lambda i: (0, 0)),
            pl.BlockSpec((1, LANES), lambda i: (0, 0)),
            pl.BlockSpec((n_e, LANES), lambda i: (0, 0)),
        ],
        out_shape=[
            jax.ShapeDtypeStruct((TOP_K, t), jnp.int32),
            jax.ShapeDtypeStruct((1, nb_pad), jnp.int32),
            jax.ShapeDtypeStruct((1, LANES), jnp.int32),
            jax.ShapeDtypeStruct((n_e, LANES), jnp.int32),
        ],
        compiler_params=_params("arbitrary"),
        name="moe_slots",
    )(counts, eidx, rank)


TILE_ROWS = 4
U32 = jnp.uint32


def _pack_bf16_pairs(v):
    half = v.shape[1] // 2
    lo = lax.bitcast_convert_type(v[:, :half].astype(BF16).astype(F32), U32)
    hi = lax.bitcast_convert_type(v[:, half:].astype(BF16).astype(F32), U32)
    return hi | (lo >> 16)


def _unpack_bf16_pairs(u):
    lo = lax.bitcast_convert_type(u << 16, F32)
    hi = lax.bitcast_convert_type(u & U32(0xFFFF0000), F32)
    return jnp.concatenate([lo, hi], axis=1)


def _tile_copy(src, src_tok, dst, dst_tok, sem):
    return pltpu.make_async_copy(
        src.at[pl.ds(pl.multiple_of(src_tok * TILE_ROWS, TILE_ROWS), TILE_ROWS)],
        dst.at[pl.ds(pl.multiple_of(dst_tok * TILE_ROWS, TILE_ROWS), TILE_ROWS)], sem)


def _to_token_tiles(ref, val, n_tok):
    for cidx in range(TILE_ROWS):
        ref[pl.ds(cidx, n_tok, stride=TILE_ROWS), :] = val[:, cidx * LANES:(cidx + 1) * LANES]


def _from_token_tiles(ref, n_tok):
    return jnp.concatenate(
        [ref[pl.ds(cidx, n_tok, stride=TILE_ROWS), :] for cidx in range(TILE_ROWS)], axis=1)


def _dispatch_kernel(dest_ref, seg_end_ref, seg_len_ref, nused_ref, x_ref, shift_ref, scale_ref,
                     xs_hbm, hs, zeros, sem, zsem, *, tm, t_total, block_rows, n_blocks):
    base = pl.program_id(0) * tm
    blk = block_rows * TILE_ROWS

    @pl.when(pl.program_id(0) == 0)
    def _():
        zeros[...] = jnp.zeros_like(zeros)

        def zero_block(first_tok):
            return pltpu.make_async_copy(
                zeros, xs_hbm.at[pl.ds(pl.multiple_of(first_tok * TILE_ROWS, blk), blk)], zsem)

        def seg_start(e, carry):
            @pl.when(seg_len_ref[e] > 0)
            def _():
                zero_block(seg_end_ref[e] - block_rows).start()
            return carry

        def seg_wait(e, carry):
            @pl.when(seg_len_ref[e] > 0)
            def _():
                zero_block(seg_end_ref[e] - block_rows).wait()
            return carry

        def tail_start(j, carry):
            zero_block(j * block_rows).start()
            return carry

        def tail_wait(j, carry):
            zero_block(j * block_rows).wait()
            return carry

        lax.fori_loop(0, N_EXPERTS, seg_start, 0)
        lax.fori_loop(nused_ref[0], n_blocks, tail_start, 0)
        lax.fori_loop(0, N_EXPERTS, seg_wait, 0)
        lax.fori_loop(nused_ref[0], n_blocks, tail_wait, 0)

    _to_token_tiles(hs, _pack_bf16_pairs(x_ref[...] * (1.0 + scale_ref[0]) + shift_ref[0]), tm)

    def issue(t, carry):
        for k in range(TOP_K):
            _tile_copy(hs, t, xs_hbm, dest_ref[k * t_total + base + t], sem).start(priority=k % 2)
        return carry

    lax.fori_loop(0, tm, issue, 0, unroll=2)
    for k in range(TOP_K):
        pltpu.make_async_copy(hs, xs_hbm.at[pl.ds(0, tm * TILE_ROWS)], sem).wait()


def moe_dispatch(dest_flat, seg_end, seg_len, nused, xt, shift, scale, n_blocks, block_rows,
                 s_len, tm=256):
    t, d = xt.shape
    per_b = s_len // tm
    grid_spec = pltpu.PrefetchScalarGridSpec(
        num_scalar_prefetch=4,
        grid=(t // tm,),
        in_specs=[
            pl.BlockSpec((tm, d), lambda i, *_: (i, 0)),
            pl.BlockSpec((1, 1, d), lambda i, *_: (i // per_b, 0, 0)),
            pl.BlockSpec((1, 1, d), lambda i, *_: (i // per_b, 0, 0)),
        ],
        out_specs=pl.BlockSpec(memory_space=pl.ANY),
        scratch_shapes=[pltpu.VMEM((tm * TILE_ROWS, LANES), U32),
                        pltpu.VMEM((block_rows * TILE_ROWS, LANES), U32),
                        pltpu.SemaphoreType.DMA, pltpu.SemaphoreType.DMA],
    )
    assert d == 2 * TILE_ROWS * LANES
    return pl.pallas_call(
        functools.partial(_dispatch_kernel, tm=tm, t_total=t, block_rows=block_rows,
                          n_blocks=n_blocks),
        grid_spec=grid_spec,
        out_shape=jax.ShapeDtypeStruct((n_blocks * block_rows * TILE_ROWS, LANES), U32),
        compiler_params=_params("arbitrary"),
        name="moe_dispatch",
    )(dest_flat, seg_end, seg_len, nused, xt, shift, scale)


def _expert_kernel(blk_e_ref, nused_ref, xs_ref, w1_ref, w3_ref, w2_ref, ys_ref,
                   w1b, w3b, w2b, *, block_rows):
    i = pl.program_id(0)
    prev = blk_e_ref[jnp.maximum(i - 1, 0)]

    @pl.when((i == 0) | (blk_e_ref[i] != prev))
    def _():
        w1b[...] = w1_ref[0].astype(BF16)
        w3b[...] = w3_ref[0].astype(BF16)
        w2b[...] = w2_ref[0].astype(BF16)

    @pl.when(i < nused_ref[0])
    def _():
        xb = _unpack_bf16_pairs(_from_token_tiles(xs_ref, block_rows)).astype(BF16)
        h1 = jnp.dot(xb, w1b[...], preferred_element_type=F32)
        h3 = jnp.dot(xb, w3b[...], preferred_element_type=F32)
        act = (h1 * _sigmoid(h1)) * h3
        y = jnp.dot(act.astype(BF16), w2b[...], preferred_element_type=F32)
        _to_token_tiles(ys_ref, _pack_bf16_pairs(y), block_rows)

    @pl.when(i >= nused_ref[0])
    def _():
        ys_ref[...] = jnp.zeros_like(ys_ref)


def moe_experts(blk_e, nused, xs, w1, w3, w2, block_rows, first_expert):
    d, de = w1.shape[-2:]
    n_rows = xs.shape[0] // TILE_ROWS
    n_blocks = n_rows // block_rows

    def row_map(i, be, nu):
        return (jnp.minimum(i, nu[0] - 1), 0)

    def w_map(i, be, nu):
        return (be[i] + first_expert, 0, 0)

    grid_spec = pltpu.PrefetchScalarGridSpec(
        num_scalar_prefetch=2,
        grid=(n_blocks,),
        in_specs=[
            pl.BlockSpec((block_rows * TILE_ROWS, LANES), row_map),
            pl.BlockSpec((1, d, de), w_map),
            pl.BlockSpec((1, d, de), w_map),
            pl.BlockSpec((1, de, d), w_map),
        ],
        out_specs=pl.BlockSpec((block_rows * TILE_ROWS, LANES), lambda i, be, nu: (i, 0)),
        scratch_shapes=[pltpu.VMEM((d, de), BF16), pltpu.VMEM((d, de), BF16),
                        pltpu.VMEM((de, d), BF16)],
    )
    return pl.pallas_call(
        functools.partial(_expert_kernel, block_rows=block_rows),
        grid_spec=grid_spec,
        out_shape=jax.ShapeDtypeStruct((n_rows * TILE_ROWS, LANES), U32),
        compiler_params=_params("arbitrary"),
        name="moe_experts",
    )(blk_e, nused, xs, w1, w3, w2)


def _combine_kernel(dest_ref, ys_hbm, x_ref, shift_ref, scale_ref, gate_ref, gcol_ref,
                    ws1_ref, ws3_ref, ws2_ref, g_ref, b_ref, o_ref, buf, sem,
                    *, tm, t_total, alpha):
    step = pl.program_id(0)
    slot = step % 2

    def start_gather(for_step, into):
        base = for_step * tm

        def issue(t, carry):
            for k in range(TOP_K):
                _tile_copy(ys_hbm, dest_ref[k * t_total + base + t], buf.at[into, k], t,
                           sem.at[into]).start(priority=k % 2)
            return carry

        lax.fori_loop(0, tm, issue, 0, unroll=2)

    @pl.when(step == 0)
    def _():
        start_gather(0, 0)

    @pl.when(step + 1 < pl.num_programs(0))
    def _():
        start_gather(step + 1, 1 - slot)

    x = x_ref[...]
    hb = (x * (1.0 + scale_ref[0]) + shift_ref[0]).astype(BF16)
    h1 = jnp.dot(hb, ws1_ref[...], preferred_element_type=F32)
    h3 = jnp.dot(hb, ws3_ref[...], preferred_element_type=F32)
    act = (h1 * _sigmoid(h1)) * h3
    y = jnp.dot(act.astype(BF16), ws2_ref[...], preferred_element_type=F32)

    for k in range(TOP_K):
        pltpu.make_async_copy(ys_hbm.at[pl.ds(0, tm * TILE_ROWS)], buf.at[slot, k],
                              sem.at[slot]).wait()

    gcol = gcol_ref[...]
    for k in range(TOP_K):
        y = y + _unpack_bf16_pairs(_from_token_tiles(buf.at[slot, k], tm)) * gcol[:, k:k + 1]
    z = alpha * x + gate_ref[0] * y
    o_ref[...] = _layer_norm(z, g_ref[...], b_ref[...])


def moe_combine(dest_flat, ys, xt, shift, scale, gate1p, gcol, ws1, ws3, ws2, ln_g, ln_b,
                s_len, alpha, tm=128):
    t, d = xt.shape
    ds_ = ws1.shape[1]
    per_b = s_len // tm

    def const(shape):
        return pl.BlockSpec(shape, lambda i, dest: (0,) * len(shape))

    mod_spec = pl.BlockSpec((1, 1, d), lambda i, dest: (i // per_b, 0, 0))
    grid_spec = pltpu.PrefetchScalarGridSpec(
        num_scalar_prefetch=1,
        grid=(t // tm,),
        in_specs=[
            pl.BlockSpec(memory_space=pl.ANY),
            pl.BlockSpec((tm, d), lambda i, dest: (i, 0)),
            mod_spec, mod_spec, mod_spec,
            pl.BlockSpec((tm, LANES), lambda i, dest: (i, 0)),
            const((d, ds_)), const((d, ds_)), const((ds_, d)),
            const((1, d)), const((1, d)),
        ],
        out_specs=pl.BlockSpec((tm, d), lambda i, dest: (i, 0)),
        scratch_shapes=[pltpu.VMEM((2, TOP_K, tm * TILE_ROWS, LANES), U32),
                        pltpu.SemaphoreType.DMA((2,))],
    )
    return pl.pallas_call(
        functools.partial(_combine_kernel, tm=tm, t_total=t, alpha=alpha),
        grid_spec=grid_spec,
        out_shape=jax.ShapeDtypeStruct((t, d), F32),
        compiler_params=_params("arbitrary"),
        name="moe_combine",
    )(dest_flat, ys, xt, shift, scale, gate1p, gcol, ws1, ws3, ws2,
      ln_g.reshape(1, d), ln_b.reshape(1, d))


def moe_layer(x, mods, router_w, router_bias, w1, w3, w2, ws1, ws3, ws2, ln_g, ln_b, alpha,
              layer):
    b, s, d = x.shape
    t = b * s
    shift = mods[:, None, 0:d]
    scale = mods[:, None, d:2 * d]
    gate1p = 1.0 + mods[:, None, 2 * d:3 * d]
    xt = x.reshape(t, d)
    n_blocks = t * TOP_K // MOE_BLOCK + N_EXPERTS
    eidx, rank, gcol, counts = moe_route(xt, shift, scale, router_w, router_bias, s)
    dest, blk_e, nused, seg = moe_slots(counts, eidx, rank, n_blocks, MOE_BLOCK)
    dest_flat = dest.reshape(-1)
    nused = nused.reshape(-1)
    xs = moe_dispatch(dest_flat, seg[:, 0], seg[:, 1], nused, xt, shift, scale, n_blocks,
                      MOE_BLOCK, s)
    n_e = w1.shape[1]
    ys = moe_experts(blk_e.reshape(-1), nused, xs,
                     w1.reshape((-1,) + w1.shape[2:]), w3.reshape((-1,) + w3.shape[2:]),
                     w2.reshape((-1,) + w2.shape[2:]), MOE_BLOCK, layer * n_e)
    out = moe_combine(dest_flat, ys, xt, shift, scale, gate1p, gcol,
                      ws1.astype(BF16), ws3.astype(BF16), ws2.astype(BF16), ln_g, ln_b, s, alpha)
    return out.reshape(b, s, d)


def kernel(x, c, positions, ada_w, ada_b, ln_g, ln_b, attn_w_in, attn_w_out, gdn_w_in, gdn_conv_w,
           gdn_a_log, gdn_dt_bias, gdn_norm_w, gdn_w_out, router_w, router_bias, expert_w1,
           expert_w3, expert_w2, shared_w1, shared_w3, shared_w2):
    depth = ada_w.shape[0]
    alpha = (2 * depth) ** 0.25
    mods = ada_vectors(c, ada_w, ada_b)
    cos_t, sin_t = rope_tables(positions)
    for i in range(depth):
        j = i // 2
        if i % 2 == 0:
            x = attention_layer(x, mods[2 * i], cos_t, sin_t, attn_w_in[j], attn_w_out[j],
                                ln_g[i, 0], ln_b[i, 0], alpha)
        else:
            x = gdn_layer(x, mods[2 * i], gdn_w_in[j], gdn_conv_w[j], gdn_a_log[j],
                          gdn_dt_bias[j], gdn_norm_w[j], gdn_w_out[j], ln_g[i, 0], ln_b[i, 0],
                          alpha)
        x = moe_layer(x, mods[2 * i + 1], router_w[i], router_bias[i], expert_w1,
                      expert_w3, expert_w2, shared_w1[i], shared_w3[i], shared_w2[i],
                      ln_g[i, 1], ln_b[i, 1], alpha, i)
    return x
```

```python
import functools
import math

import jax
import jax.numpy as jnp
from jax import lax
from jax.experimental import pallas as pl
from jax.experimental.pallas import tpu as pltpu

F32 = jnp.float32
BF16 = jnp.bfloat16

LANES = 128

DILATED_GROUPS = ((128, 1), (512, 4), (2048, 16))
A_HEADS = 16
A_HEAD_DIM = 64
ATTN_BLOCK = 128
ROPE_THETA = 10000.0
GDN_HEADS = 8
GDN_HEAD_DIM = 128
CONV_WIDTH = 4
GDN_CHUNK = 64
N_EXPERTS = 64
TOP_K = 8
N_EXPERT_GROUPS = 8
TOPK_GROUPS = 4
ROUTED_SCALE = 2.5
MOE_BLOCK = 512
LN_EPS = 1e-5
RMS_EPS = 1e-6
NEG_BIG = -1e30

VMEM_LIMIT = 56 * 1024 * 1024


def _params(*sem):
    return pltpu.CompilerParams(dimension_semantics=sem, vmem_limit_bytes=VMEM_LIMIT)


def _ada_kernel(c_ref, w_ref, b_ref, o_ref):
    c = c_ref[...]
    c_hi = c.astype(BF16)
    c_lo = (c - c_hi.astype(F32)).astype(BF16)
    w = w_ref[0]
    w_hi = w.astype(BF16)
    w_lo = (w - w_hi.astype(F32)).astype(BF16)
    acc = jnp.dot(c_hi, w_hi, preferred_element_type=F32)
    acc += jnp.dot(c_hi, w_lo, preferred_element_type=F32)
    acc += jnp.dot(c_lo, w_hi, preferred_element_type=F32)
    o_ref[0] = acc + b_ref[0]


def ada_vectors(c, ada_w, ada_b):
    depth2 = ada_w.shape[0] * ada_w.shape[1]
    b, d = c.shape
    d3 = ada_w.shape[-1]
    tn = d
    w = ada_w.reshape(depth2, d, d3)
    bias = ada_b.reshape(depth2, 1, d3)
    return pl.pallas_call(
        _ada_kernel,
        grid=(depth2, d3 // tn),
        in_specs=[
            pl.BlockSpec((b, d), lambda l, j: (0, 0)),
            pl.BlockSpec((1, d, tn), lambda l, j: (l, 0, j)),
            pl.BlockSpec((1, 1, tn), lambda l, j: (l, 0, j)),
        ],
        out_specs=pl.BlockSpec((1, b, tn), lambda l, j: (l, 0, j)),
        out_shape=jax.ShapeDtypeStruct((depth2, b, d3), F32),
        compiler_params=_params("parallel", "parallel"),
        name="ada_vectors",
    )(c, w, bias)


def _rope_kernel(pos_ref, freq_ref, sign_ref, cos_ref, sin_ref):
    ang = pos_ref[0].astype(F32) * freq_ref[...]
    cos_ref[0] = jnp.cos(ang)
    sin_ref[0] = jnp.sin(ang) * sign_ref[...]


def rope_tables(positions):
    b, s = positions.shape
    half = A_HEAD_DIM // 2
    inv_freq = ROPE_THETA ** (-jnp.arange(half, dtype=F32) / half)
    freq = jnp.tile(inv_freq, LANES // half).reshape(1, LANES)
    sign = jnp.tile(jnp.concatenate([-jnp.ones((half,), F32), jnp.ones((half,), F32)]),
                    LANES // A_HEAD_DIM).reshape(1, LANES)
    ts = min(s, 1024)
    return pl.pallas_call(
        _rope_kernel,
        grid=(b, s // ts),
        in_specs=[
            pl.BlockSpec((1, ts, 1), lambda i, j: (i, j, 0)),
            pl.BlockSpec((1, LANES), lambda i, j: (0, 0)),
            pl.BlockSpec((1, LANES), lambda i, j: (0, 0)),
        ],
        out_specs=[pl.BlockSpec((1, ts, LANES), lambda i, j: (i, j, 0))] * 2,
        out_shape=[jax.ShapeDtypeStruct((b, s, LANES), F32)] * 2,
        compiler_params=_params("parallel", "parallel"),
        name="rope_tables",
    )(positions.reshape(b, s, 1), freq, sign)


def _residue_major(ref, lead, dil, tl):
    if dil == 1:
        return ref[lead] if lead is not None else ref[...]
    parts = []
    for r in range(dil):
        if lead is None:
            parts.append(ref[pl.ds(r, tl, stride=dil), :])
        else:
            parts.append(ref[lead, pl.ds(r, tl, stride=dil), :])
    return jnp.concatenate(parts, axis=0)


def _attn_proj_kernel(x_ref, shift_ref, scale_ref, cos_ref, sin_ref, w_ref,
                      q_ref, k_ref, v_ref, h_scr, *, dil, tm):
    tl = tm // dil
    d = x_ref.shape[-1]
    h = x_ref[0] * (1.0 + scale_ref[0]) + shift_ref[0]
    if dil == 1:
        hb = h.astype(BF16)
    else:
        for cidx in range(d // LANES):
            h_scr[cidx] = h[:, cidx * LANES:(cidx + 1) * LANES]
        hb = jnp.concatenate(
            [_residue_major(h_scr, cidx, dil, tl).astype(BF16) for cidx in range(d // LANES)],
            axis=1)
    cosp = _residue_major(cos_ref, 0, dil, tl)
    sinp = _residue_major(sin_ref, 0, dil, tl)
    lane = lax.broadcasted_iota(jnp.int32, (tm, LANES), 1)
    first_half = (lane % A_HEAD_DIM) < (A_HEAD_DIM // 2)
    qscale = A_HEAD_DIM ** -0.5
    for j, out_ref in enumerate((q_ref, k_ref, v_ref)):
        res = jnp.dot(hb, w_ref[:, j * d:(j + 1) * d], preferred_element_type=F32)
        for hp in range(d // LANES):
            blk = res[:, hp * LANES:(hp + 1) * LANES]
            if j < 2:
                swapped = jnp.where(first_half,
                                    pltpu.roll(blk, LANES - A_HEAD_DIM // 2, 1),
                                    pltpu.roll(blk, A_HEAD_DIM // 2, 1))
                blk = blk * cosp + swapped * sinp
                if j == 0:
                    blk = blk * qscale
            blk = blk.astype(BF16)
            for r in range(dil):
                out_ref[0, hp, r] = blk[r * tl:(r + 1) * tl]


def attn_project(x, shift, scale, cos_t, sin_t, w_g, dil, tm=512):
    b, s, d = x.shape
    tl = tm // dil
    n_hp = d // LANES
    out_sds = jax.ShapeDtypeStruct((b, n_hp, dil, s // dil, LANES), BF16)
    out_spec = pl.BlockSpec((1, n_hp, dil, tl, LANES), lambda i, j: (i, 0, 0, j, 0))
    return pl.pallas_call(
        functools.partial(_attn_proj_kernel, dil=dil, tm=tm),
        grid=(b, s // tm),
        in_specs=[
            pl.BlockSpec((1, tm, d), lambda i, j: (i, j, 0)),
            pl.BlockSpec((1, 1, d), lambda i, j: (i, 0, 0)),
            pl.BlockSpec((1, 1, d), lambda i, j: (i, 0, 0)),
            pl.BlockSpec((1, tm, LANES), lambda i, j: (i, j, 0)),
            pl.BlockSpec((1, tm, LANES), lambda i, j: (i, j, 0)),
            pl.BlockSpec((d, 3 * d), lambda i, j: (0, 0)),
        ],
        out_specs=[out_spec] * 3,
        out_shape=[out_sds] * 3,
        scratch_shapes=[pltpu.VMEM((d // LANES, tm, LANES), F32)],
        compiler_params=_params("parallel", "parallel"),
        name=f"attn_project_d{dil}",
    )(x, shift, scale, cos_t, sin_t, w_g)


def _attn_kernel(*refs, s_len, dils):
    n_g = len(dils)
    qkv = refs[:3 * n_g]
    o_ref = refs[3 * n_g]
    oacc, lacc = refs[3 * n_g + 1:]
    blk = ATTN_BLOCK

    lane = lax.broadcasted_iota(jnp.int32, (blk, LANES), 1)
    head0 = lane < A_HEAD_DIM
    row = lax.broadcasted_iota(jnp.int32, (2 * blk, 2 * blk), 0) % blk
    col = lax.broadcasted_iota(jnp.int32, (2 * blk, 2 * blk), 1)
    bias_rest = jnp.where((col >= row) & (col <= row + blk), 0.0, NEG_BIG).astype(F32)
    bias_first = bias_rest[:, blk:]

    def run_blocks(g, q_ref, k_ref, v_ref, dil, specs):
        ids = range(len(specs))
        qq, kc, vc = [], [], []
        for r, n, first in specs:
            q = q_ref[0, 0, r, pl.ds(pl.multiple_of(n * blk, blk), blk), :]
            zero = jnp.zeros_like(q)
            qq.append(jnp.concatenate([jnp.where(head0, q, zero), jnp.where(head0, zero, q)],
                                      axis=0))
            if first:
                keys = pl.ds(0, blk)
            else:
                keys = pl.ds(pl.multiple_of((n - 1) * blk, blk), 2 * blk)
            kc.append(k_ref[0, 0, r, keys, :])
            vc.append(v_ref[0, 0, r, keys, :])
        sc = [lax.dot_general(qq[i], kc[i], (((1,), (1,)), ((), ())), preferred_element_type=F32)
              + (bias_first if specs[i][2] else bias_rest) for i in ids]
        m = [jnp.max(sc[i], axis=1, keepdims=True) for i in ids]
        p = [jnp.exp(sc[i] - m[i]) for i in ids]
        l = [jnp.sum(p[i], axis=1, keepdims=True) for i in ids]
        pv = [jnp.dot(p[i].astype(BF16), vc[i], preferred_element_type=F32) * (1.0 / l[i])
              for i in ids]
        for i in ids:
            r, n, _ = specs[i]
            lse = m[i] + jnp.log(l[i])
            o_blk = jnp.where(head0, pv[i][:blk], pv[i][blk:])
            lse_blk = jnp.where(head0, jnp.broadcast_to(lse[:blk], (blk, LANES)),
                                jnp.broadcast_to(lse[blk:], (blk, LANES)))
            start = n * (blk * dil) + r
            if dil == 1:
                idx = pl.ds(pl.multiple_of(start, blk), blk)
            else:
                idx = pl.ds(start, blk, stride=dil)
            oacc[g, idx, :] = o_blk
            lacc[g, idx, :] = lse_blk

    for g, dil in enumerate(dils):
        q_ref, k_ref, v_ref = qkv[3 * g:3 * g + 3]
        nb = s_len // dil // blk

        def per_step(rr, carry, g=g, dil=dil, nb=nb, q_ref=q_ref, k_ref=k_ref, v_ref=v_ref):
            if dil == 1:
                res, width = [0], 4
            else:
                res, width = [2 * rr, 2 * rr + 1], 2
            if nb < width:
                run_blocks(g, q_ref, k_ref, v_ref, dil, [(r, 0, True) for r in res])
                return carry
            run_blocks(g, q_ref, k_ref, v_ref, dil,
                       [(r, n, n == 0) for r in res for n in range(width)])

            def per_group(i, c2):
                run_blocks(g, q_ref, k_ref, v_ref, dil,
                           [(r, width * i + n, False) for r in res for n in range(width)])
                return c2

            lax.fori_loop(1, nb // width, per_group, 0)
            return carry

        lax.fori_loop(0, max(dil // 2, 1), per_step, 0)

    tc = 256

    def mix(i, carry):
        rows = pl.ds(pl.multiple_of(i * tc, tc), tc)
        ls = [lacc[g, rows, :] for g in range(n_g)]
        mx = functools.reduce(jnp.maximum, ls)
        es = [jnp.exp(l - mx) for l in ls]
        den = functools.reduce(lambda a, b2: a + b2, es)
        num = es[0] * oacc[0, rows, :]
        for g in range(1, n_g):
            num = num + es[g] * oacc[g, rows, :]
        o_ref[0, 0, rows, :] = (num * (1.0 / den)).astype(o_ref.dtype)
        return carry

    lax.fori_loop(0, s_len // tc, mix, 0)


def attn_core(qkvs, dils, s_len):
    b, n_hp = qkvs[0].shape[:2]
    in_specs = []
    for g, dil in enumerate(dils):
        spec = pl.BlockSpec((1, 1, dil, s_len // dil, LANES), lambda i, j: (i, j, 0, 0, 0))
        in_specs += [spec] * 3
    n_g = len(dils)
    return pl.pallas_call(
        functools.partial(_attn_kernel, s_len=s_len, dils=tuple(dils)),
        grid=(b, n_hp),
        in_specs=in_specs,
        out_specs=pl.BlockSpec((1, 1, s_len, LANES), lambda i, j: (i, j, 0, 0)),
        out_shape=jax.ShapeDtypeStruct((b, n_hp, s_len, LANES), BF16),
        scratch_shapes=[pltpu.VMEM((n_g, s_len, LANES), F32),
                        pltpu.VMEM((n_g, s_len, LANES), F32)],
        compiler_params=_params("parallel", "parallel"),
        name="attn_core",
    )(*qkvs)


def _layer_norm(z, g, b):
    mu = jnp.mean(z, axis=-1, keepdims=True)
    zc = z - mu
    var = jnp.mean(zc * zc, axis=-1, keepdims=True)
    return zc * lax.rsqrt(var + LN_EPS) * g + b


def _out_ln_kernel(o_ref, w_ref, x_ref, gate_ref, g_ref, b_ref, y_ref, *, alpha):
    n_hp = o_ref.shape[1]
    o = jnp.concatenate([o_ref[0, hp] for hp in range(n_hp)], axis=-1)
    y = jnp.dot(o, w_ref[...], preferred_element_type=F32)
    z = alpha * x_ref[0] + gate_ref[0] * y
    y_ref[0] = _layer_norm(z, g_ref[...], b_ref[...])


def out_proj_ln(o_heads, w_out, x, gate1p, ln_g, ln_b, alpha, tm=512):
    b, s, d = x.shape
    n_hp = o_heads.shape[1]
    return pl.pallas_call(
        functools.partial(_out_ln_kernel, alpha=alpha),
        grid=(b, s // tm),
        in_specs=[
            pl.BlockSpec((1, n_hp, tm, LANES), lambda i, j: (i, 0, j, 0)),
            pl.BlockSpec((d, d), lambda i, j: (0, 0)),
            pl.BlockSpec((1, tm, d), lambda i, j: (i, j, 0)),
            pl.BlockSpec((1, 1, d), lambda i, j: (i, 0, 0)),
            pl.BlockSpec((1, d), lambda i, j: (0, 0)),
            pl.BlockSpec((1, d), lambda i, j: (0, 0)),
        ],
        out_specs=pl.BlockSpec((1, tm, d), lambda i, j: (i, j, 0)),
        out_shape=jax.ShapeDtypeStruct((b, s, d), F32),
        compiler_params=_params("parallel", "parallel"),
        name="out_proj_ln",
    )(o_heads, w_out, x, gate1p, ln_g.reshape(1, d), ln_b.reshape(1, d))


def attention_layer(x, mods, cos_t, sin_t, w_in, w_out, ln_g, ln_b, alpha):
    b, s, d = x.shape
    shift = mods[:, None, 0:d]
    scale = mods[:, None, d:2 * d]
    gate1p = 1.0 + mods[:, None, 2 * d:3 * d]
    w_in = w_in.astype(BF16)
    dils = [dil for _, dil in DILATED_GROUPS]
    qkvs = []
    for g, dil in enumerate(dils):
        qkvs += attn_project(x, shift, scale, cos_t, sin_t,
                             w_in[:, g * 3 * d:(g + 1) * 3 * d], dil)
    o = attn_core(qkvs, dils, s)
    return out_proj_ln(o, w_out.astype(BF16), x, gate1p, ln_g, ln_b, alpha)


def _split_bf16(a):
    hi = a.astype(BF16)
    lo = (a - hi.astype(F32)).astype(BF16)
    return hi, lo


def _gdn_proj_kernel(x_ref, shift_ref, scale_ref, w_ref, wab_hi_ref, wab_lo_ref,
                     wabt_hi_ref, wabt_lo_ref, q_ref, k_ref, v_ref, z_ref, ab_ref, abt_ref, *, tm):
    d = x_ref.shape[-1]
    h = x_ref[0] * (1.0 + scale_ref[0]) + shift_ref[0]
    h_hi, h_lo = _split_bf16(h)
    for j, out_ref in enumerate((q_ref, k_ref, v_ref, z_ref)):
        res = jnp.dot(h_hi, w_ref[:, j * d:(j + 1) * d], preferred_element_type=F32)
        for hd in range(d // LANES):
            out_ref[0, hd] = res[:, hd * LANES:(hd + 1) * LANES].astype(BF16)
    ab = jnp.dot(h_hi, wab_hi_ref[...], preferred_element_type=F32)
    ab += jnp.dot(h_lo, wab_hi_ref[...], preferred_element_type=F32)
    ab += jnp.dot(h_hi, wab_lo_ref[...], preferred_element_type=F32)
    ab_ref[0] = ab
    nt = (((1,), (1,)), ((), ()))
    abt = lax.dot_general(wabt_hi_ref[...], h_hi, nt, preferred_element_type=F32)
    abt += lax.dot_general(wabt_hi_ref[...], h_lo, nt, preferred_element_type=F32)
    abt += lax.dot_general(wabt_lo_ref[...], h_hi, nt, preferred_element_type=F32)
    for n in range(tm // GDN_CHUNK):
        abt_ref[0, n] = abt[:, n * GDN_CHUNK:(n + 1) * GDN_CHUNK]


def gdn_project(x, shift, scale, w_in, tm=512):
    b, s, d = x.shape
    n_h = d // LANES
    w_main = w_in[:, :4 * d].astype(BF16)
    w_ab = jnp.pad(w_in[:, 4 * d:], ((0, 0), (0, LANES - 2 * GDN_HEADS)))
    w_abt = w_in[:, 4 * d:].T
    wab_hi, wab_lo = _split_bf16(w_ab)
    wabt_hi, wabt_lo = _split_bf16(w_abt)
    head_sds = jax.ShapeDtypeStruct((b, n_h, s, LANES), BF16)
    head_spec = pl.BlockSpec((1, n_h, tm, LANES), lambda i, j: (i, 0, j, 0))
    n_ab = 2 * GDN_HEADS
    return pl.pallas_call(
        functools.partial(_gdn_proj_kernel, tm=tm),
        grid=(b, s // tm),
        in_specs=[
            pl.BlockSpec((1, tm, d), lambda i, j: (i, j, 0)),
            pl.BlockSpec((1, 1, d), lambda i, j: (i, 0, 0)),
            pl.BlockSpec((1, 1, d), lambda i, j: (i, 0, 0)),
            pl.BlockSpec((d, 4 * d), lambda i, j: (0, 0)),
            pl.BlockSpec((d, LANES), lambda i, j: (0, 0)),
            pl.BlockSpec((d, LANES), lambda i, j: (0, 0)),
            pl.BlockSpec((n_ab, d), lambda i, j: (0, 0)),
            pl.BlockSpec((n_ab, d), lambda i, j: (0, 0)),
        ],
        out_specs=[head_spec] * 4 + [
            pl.BlockSpec((1, tm, LANES), lambda i, j: (i, j, 0)),
            pl.BlockSpec((1, tm // GDN_CHUNK, n_ab, GDN_CHUNK), lambda i, j: (i, j, 0, 0)),
        ],
        out_shape=[head_sds] * 4 + [
            jax.ShapeDtypeStruct((b, s, LANES), F32),
            jax.ShapeDtypeStruct((b, s // GDN_CHUNK, n_ab, GDN_CHUNK), F32),
        ],
        compiler_params=_params("parallel", "parallel"),
        name="gdn_project",
    )(x, shift, scale, w_main, wab_hi, wab_lo, wabt_hi, wabt_lo)


def _softplus(t):
    return jnp.maximum(t, 0.0) + jnp.log(1.0 + jnp.exp(-jnp.abs(t)))


def _sigmoid(t):
    return 1.0 / (1.0 + jnp.exp(-t))


def _dot3_exact_lhs(lhs_bf16, rhs_f32):
    acc = None
    rem = rhs_f32
    for _ in range(3):
        part = rem.astype(BF16)
        rem = rem - part.astype(F32)
        t = jnp.dot(lhs_bf16, part, preferred_element_type=F32)
        acc = t if acc is None else acc + t
    return acc


def _dot3_exact_rhs(lhs_f32, rhs_bf16):
    acc = None
    rem = lhs_f32
    for _ in range(3):
        part = rem.astype(BF16)
        rem = rem - part.astype(F32)
        t = jnp.dot(part, rhs_bf16, preferred_element_type=F32)
        acc = t if acc is None else acc + t
    return acc


def _gdn_kernel(q_ref, k_ref, v_ref, z_ref, ab_ref, abt_ref, convw_ref, alog_l_ref, dtb_l_ref,
                alog_c_ref, dtb_c_ref, normw_ref, o_ref,
                state, carry, qs, ks, vs, gcol_s, beta_s, *, tc):
    n_h = GDN_HEADS
    ch = GDN_CHUNK
    dh = GDN_HEAD_DIM
    halo = 8

    @pl.when(pl.program_id(1) == 0)
    def _():
        state[...] = jnp.zeros_like(state)
        carry[:, :, 0:halo, :] = jnp.zeros((3, n_h, halo, dh), F32)

    for a, (src, dst) in enumerate(((q_ref, qs), (k_ref, ks), (v_ref, vs))):
        for hd in range(n_h):
            xb = carry.at[a, hd]
            xb[halo:halo + tc, :] = src[0, hd].astype(F32)
            w = convw_ref[a, hd]
            y = xb[halo:halo + tc, :] * w[CONV_WIDTH - 1:CONV_WIDTH]
            for j in range(CONV_WIDTH - 1):
                off = halo - (CONV_WIDTH - 1) + j
                y = y + xb[off:off + tc, :] * w[j:j + 1]
            xb[0:halo, :] = xb[tc:tc + halo, :]
            y = y * _sigmoid(y)
            if a < 2:
                y = y * lax.rsqrt(jnp.sum(y * y, axis=-1, keepdims=True) + RMS_EPS)
                if a == 0:
                    y = y * (dh ** -0.5)
            dst[hd] = y

    ab = ab_ref[0]
    gcol_s[...] = -jnp.exp(alog_l_ref[...]) * _softplus(ab + dtb_l_ref[...])
    beta_s[...] = _sigmoid(ab)

    ri = lax.broadcasted_iota(jnp.int32, (ch, ch), 0)
    ci = lax.broadcasted_iota(jnp.int32, (ch, ch), 1)
    causal = ri >= ci
    strict = ri > ci
    tril = jnp.where(causal, 1.0, 0.0).astype(BF16)
    triu = jnp.where(ri <= ci, 1.0, 0.0).astype(BF16)
    eye = jnp.where(ri == ci, 1.0, 0.0).astype(F32)
    nt = (((1,), (1,)), ((), ()))
    tn = (((0,), (0,)), ((), ()))

    n_par = 4 if (tc // ch) % 4 == 0 else 1
    heads = range(n_h)

    def chunk_group(gidx, carry_unused):
        cids = [gidx * n_par + j for j in range(n_par)]
        rows = [pl.ds(pl.multiple_of(c * ch, ch), ch) for c in cids]
        gc = [_dot3_exact_lhs(tril, gcol_s[r, :]) for r in rows]
        gc_t = []
        for c in cids:
            a_t = abt_ref[0, c][0:n_h, :]
            g_t = -jnp.exp(alog_c_ref[...]) * _softplus(a_t + dtb_c_ref[...])
            gc_t.append(_dot3_exact_rhs(g_t, triu))
        beta = [beta_s[r, :] for r in rows]
        items = [(j, hd) for j in range(n_par) for hd in heads]
        ids = range(len(items))
        gcol = [gc[j][:, hd:hd + 1] for j, hd in items]
        bcol = [beta[j][:, n_h + hd:n_h + hd + 1] for j, hd in items]
        decay = [jnp.exp(jnp.where(causal, gcol[i] - gc_t[j][hd:hd + 1, :], NEG_BIG))
                 for i, (j, hd) in enumerate(items)]
        qc = [qs[hd, rows[j], :] for j, hd in items]
        kc = [ks[hd, rows[j], :] for j, hd in items]
        kb = [kc[i] * bcol[i] for i in ids]
        both = [lax.dot_general(jnp.concatenate([kb[i], qc[i]], axis=0).astype(BF16),
                                kc[i].astype(BF16), nt, preferred_element_type=F32)
                for i in ids]
        lmat = [jnp.where(strict, both[i][:ch] * decay[i], 0.0) for i in ids]
        qk = [(both[i][ch:] * decay[i]).astype(BF16) for i in ids]
        pm = [-lmat[i] for i in ids]
        mm = lmat
        for _ in range(5):
            mb = [mm[i].astype(BF16) for i in ids]
            mm = [jnp.dot(mb[i], mb[i], preferred_element_type=F32) for i in ids]
            pm = [pm[i] + mm[i] + jnp.dot(pm[i].astype(BF16), mm[i].astype(BF16),
                                          preferred_element_type=F32) for i in ids]
        eg = [jnp.exp(gcol[i]) for i in ids]
        rhs = [jnp.concatenate([vs[hd, rows[j], :] * bcol[i], kb[i] * eg[i]], axis=1)
               for i, (j, hd) in enumerate(items)]
        sol = [rhs[i] + jnp.dot(pm[i].astype(BF16), rhs[i].astype(BF16),
                                preferred_element_type=F32) for i in ids]
        glast = [gcol[i][ch - 1:ch, :] for i in ids]
        kdec = [(kc[i] * jnp.exp(glast[i] - gcol[i])).astype(BF16) for i in ids]
        wqg = [jnp.concatenate([sol[i][:, dh:], qc[i] * eg[i]], axis=0).astype(BF16)
               for i in ids]
        for j in range(n_par):
            sel = [i for i in ids if items[i][0] == j]
            s_prev = [state[hd] for hd in heads]
            wq = [jnp.dot(wqg[i], s_prev[hd].astype(BF16), preferred_element_type=F32)
                  for hd, i in enumerate(sel)]
            vb = [(sol[i][:, :dh] - wq[hd][:ch]).astype(BF16) for hd, i in enumerate(sel)]
            o = [wq[hd][ch:] + jnp.dot(qk[i], vb[hd], preferred_element_type=F32)
                 for hd, i in enumerate(sel)]
            for hd, i in enumerate(sel):
                state[hd] = s_prev[hd] * jnp.exp(glast[i]) + lax.dot_general(
                    kdec[i], vb[hd], tn, preferred_element_type=F32)
            for hd in heads:
                on = o[hd] * lax.rsqrt(jnp.mean(o[hd] * o[hd], axis=-1, keepdims=True)
                                       + RMS_EPS) * normw_ref[...]
                zc = z_ref[0, hd, rows[j], :].astype(F32)
                o_ref[0, hd, rows[j], :] = (on * (zc * _sigmoid(zc))).astype(o_ref.dtype)
        return carry_unused

    lax.fori_loop(0, tc // ch // n_par, chunk_group, 0)


def gdn_core(q, k, v, z, ab, abt, conv_w, a_log, dt_bias, norm_w, tc=512):
    b, n_h, s, dh = q.shape
    n_ab = 2 * n_h
    convw = conv_w.reshape(CONV_WIDTH, 3, n_h, dh).transpose(1, 2, 0, 3)
    pad = (0, LANES - n_h)
    alog_l = jnp.pad(a_log.astype(F32), pad).reshape(1, LANES)
    dtb_l = jnp.pad(dt_bias.astype(F32), pad).reshape(1, LANES)
    alog_c = a_log.astype(F32).reshape(n_h, 1)
    dtb_c = dt_bias.astype(F32).reshape(n_h, 1)
    head_spec = pl.BlockSpec((1, n_h, tc, dh), lambda i, j: (i, 0, j, 0))

    def full(shape):
        return pl.BlockSpec(shape, lambda i, j: (0,) * len(shape))

    return pl.pallas_call(
        functools.partial(_gdn_kernel, tc=tc),
        grid=(b, s // tc),
        in_specs=[head_spec] * 4 + [
            pl.BlockSpec((1, tc, LANES), lambda i, j: (i, j, 0)),
            pl.BlockSpec((1, tc // GDN_CHUNK, n_ab, GDN_CHUNK), lambda i, j: (i, j, 0, 0)),
            full((3, n_h, CONV_WIDTH, dh)),
            full((1, LANES)), full((1, LANES)), full((n_h, 1)), full((n_h, 1)), full((1, dh)),
        ],
        out_specs=head_spec,
        out_shape=jax.ShapeDtypeStruct((b, n_h, s, dh), BF16),
        scratch_shapes=[
            pltpu.VMEM((n_h, dh, dh), F32),
            pltpu.VMEM((3, n_h, 8 + tc, dh), F32),
            pltpu.VMEM((n_h, tc, dh), F32),
            pltpu.VMEM((n_h, tc, dh), F32),
            pltpu.VMEM((n_h, tc, dh), F32),
            pltpu.VMEM((tc, LANES), F32),
            pltpu.VMEM((tc, LANES), F32),
        ],
        compiler_params=_params("parallel", "arbitrary"),
        name="gdn_core",
    )(q, k, v, z, ab, abt, convw, alog_l, dtb_l, alog_c, dtb_c, norm_w.reshape(1, dh).astype(F32))


def gdn_layer(x, mods, w_in, conv_w, a_log, dt_bias, norm_w, w_out, ln_g, ln_b, alpha):
    b, s, d = x.shape
    shift = mods[:, None, 0:d]
    scale = mods[:, None, d:2 * d]
    gate1p = 1.0 + mods[:, None, 2 * d:3 * d]
    q, k, v, z, ab, abt = gdn_project(x, shift, scale, w_in)
    o = gdn_core(q, k, v, z, ab, abt, conv_w, a_log, dt_bias, norm_w)
    return out_proj_ln(o, w_out.astype(BF16), x, gate1p, ln_g, ln_b, alpha)


def _first_argmax(vals, idx, n):
    mx = jnp.max(vals, axis=0, keepdims=True)
    first = jnp.min(jnp.where(vals == mx, idx, n), axis=0, keepdims=True)
    return mx, first


def _router_kernel(x_ref, shift_ref, scale_ref, rwt_hi_ref, rwt_lo_ref, bias_ref,
                   eidx_ref, rank_ref, gcol_ref, cnt_ref, cnt_scr, *, tm):
    n_e = N_EXPERTS
    per_g = n_e // N_EXPERT_GROUPS

    @pl.when(pl.program_id(0) == 0)
    def _():
        cnt_scr[...] = jnp.zeros_like(cnt_scr)

    h = x_ref[...] * (1.0 + scale_ref[0]) + shift_ref[0]
    h_hi, h_lo = _split_bf16(h)
    nt = (((1,), (1,)), ((), ()))
    logits = lax.dot_general(rwt_hi_ref[...], h_hi, nt, preferred_element_type=F32)
    logits += lax.dot_general(rwt_hi_ref[...], h_lo, nt, preferred_element_type=F32)
    logits += lax.dot_general(rwt_lo_ref[...], h_hi, nt, preferred_element_type=F32)
    scores = _sigmoid(logits)
    biased = scores + bias_ref[...]

    r8 = lax.broadcasted_iota(jnp.int32, (per_g, tm), 0)
    gscores = []
    for g in range(N_EXPERT_GROUPS):
        blk = biased[g * per_g:(g + 1) * per_g]
        m1, i1 = _first_argmax(blk, r8, per_g)
        m2 = jnp.max(jnp.where(r8 == i1, NEG_BIG, blk), axis=0, keepdims=True)
        gscores.append(m1 + m2)
    vals = jnp.concatenate(gscores, axis=0)
    rg = lax.broadcasted_iota(jnp.int32, (N_EXPERT_GROUPS, tm), 0)
    gsel = jnp.zeros((N_EXPERT_GROUPS, tm), F32)
    for _ in range(TOPK_GROUPS):
        _, idx = _first_argmax(vals, rg, N_EXPERT_GROUPS)
        hit = rg == idx
        gsel = jnp.where(hit, 1.0, gsel)
        vals = jnp.where(hit, NEG_BIG, vals)
    emask = jnp.concatenate(
        [jnp.broadcast_to(gsel[g:g + 1], (per_g, tm)) for g in range(N_EXPERT_GROUPS)], axis=0)
    masked = jnp.where(emask > 0.5, biased, NEG_BIG)

    re = lax.broadcasted_iota(jnp.int32, (n_e, tm), 0)
    hits, eidx, gates = [], [], []
    for _ in range(TOP_K):
        _, idx = _first_argmax(masked, re, n_e)
        hit = re == idx
        gates.append(jnp.sum(jnp.where(hit, scores, 0.0), axis=0, keepdims=True))
        masked = jnp.where(hit, NEG_BIG, masked)
        hits.append(hit)
        eidx.append(idx)
    gates = jnp.concatenate(gates, axis=0)
    gates = gates / jnp.sum(gates, axis=0, keepdims=True) * ROUTED_SCALE
    onehot = jnp.zeros((n_e, tm), F32)
    for hit in hits:
        onehot = jnp.where(hit, 1.0, onehot)

    ti = lax.broadcasted_iota(jnp.int32, (tm, tm), 0)
    tj = lax.broadcasted_iota(jnp.int32, (tm, tm), 1)
    before = jnp.where(ti < tj, 1.0, 0.0).astype(BF16)
    prefix = jnp.dot(onehot.astype(BF16), before, preferred_element_type=F32) + cnt_scr[...]
    ranks = [jnp.sum(jnp.where(hit, prefix, 0.0), axis=0, keepdims=True) for hit in hits]
    cnt_scr[...] = cnt_scr[...] + jnp.sum(onehot, axis=1, keepdims=True)

    eidx_ref[...] = jnp.concatenate(eidx, axis=0)
    rank_ref[...] = jnp.concatenate(ranks, axis=0).astype(jnp.int32)
    cnt_ref[...] = jnp.broadcast_to(cnt_scr[...], cnt_ref.shape)
    gpad = jnp.concatenate([gates, jnp.zeros((LANES - TOP_K, tm), F32)], axis=0)
    gcol_ref[...] = gpad.T


def moe_route(xt, shift, scale, router_w, router_bias, s_len, tm=512):
    t, d = xt.shape
    n_e = router_w.shape[1]
    rwt_hi, rwt_lo = _split_bf16(router_w.T)
    per_b = s_len // tm
    return pl.pallas_call(
        functools.partial(_router_kernel, tm=tm),
        grid=(t // tm,),
        in_specs=[
            pl.BlockSpec((tm, d), lambda i: (i, 0)),
            pl.BlockSpec((1, 1, d), lambda i: (i // per_b, 0, 0)),
            pl.BlockSpec((1, 1, d), lambda i: (i // per_b, 0, 0)),
            pl.BlockSpec((n_e, d), lambda i: (0, 0)),
            pl.BlockSpec((n_e, d), lambda i: (0, 0)),
            pl.BlockSpec((n_e, 1), lambda i: (0, 0)),
        ],
        out_specs=[
            pl.BlockSpec((TOP_K, tm), lambda i: (0, i)),
            pl.BlockSpec((TOP_K, tm), lambda i: (0, i)),
            pl.BlockSpec((tm, LANES), lambda i: (i, 0)),
            pl.BlockSpec((n_e, LANES), lambda i: (0, 0)),
        ],
        out_shape=[
            jax.ShapeDtypeStruct((TOP_K, t), jnp.int32),
            jax.ShapeDtypeStruct((TOP_K, t), jnp.int32),
            jax.ShapeDtypeStruct((t, LANES), F32),
            jax.ShapeDtypeStruct((n_e, LANES), F32),
        ],
        scratch_shapes=[pltpu.VMEM((n_e, 1), F32)],
        compiler_params=_params("arbitrary"),
        name="moe_route",
    )(xt, shift, scale, rwt_hi, rwt_lo, router_bias.reshape(n_e, 1).astype(F32))


def _slots_kernel(cnt_ref, eidx_ref, rank_ref, dest_ref, blk_e_ref, nused_ref, seg_ref,
                  *, block_rows):
    n_e = N_EXPERTS
    cnt = cnt_ref[...].astype(jnp.int32)
    padded = ((cnt + (block_rows - 1)) // block_rows) * block_rows
    ri = lax.broadcasted_iota(jnp.int32, (n_e, n_e), 0)
    ci = lax.broadcasted_iota(jnp.int32, (n_e, n_e), 1)
    tril = jnp.where(ri >= ci, 1.0, 0.0).astype(BF16)
    pend = _dot3_exact_lhs(tril, padded.astype(F32))
    pstart = pend.astype(jnp.int32) - padded
    eidx = eidx_ref[...]
    dest = rank_ref[...]
    for e in range(n_e):
        dest = dest + jnp.where(eidx == e, pstart[e:e + 1, 0:1], 0)
    dest_ref[...] = dest
    nb = blk_e_ref.shape[1]
    first_row = lax.broadcasted_iota(jnp.int32, (n_e, nb), 1) * block_rows
    pend_i = pend.astype(jnp.int32)[:, 0:1]
    owner = jnp.sum(jnp.where(pend_i <= first_row, 1, 0), axis=0, keepdims=True)
    blk_e_ref[...] = jnp.minimum(owner, n_e - 1)
    nused_ref[...] = pend_i[n_e - 1:n_e, :] // block_rows + jnp.zeros(nused_ref.shape, jnp.int32)
    lane = lax.broadcasted_iota(jnp.int32, seg_ref.shape, 1)
    seg_ref[...] = jnp.where(lane == 0, pend.astype(jnp.int32), padded)


def moe_slots(counts, eidx, rank, n_blocks, block_rows, tt=2048):
    t = eidx.shape[1]
    tt = min(tt, t)
    nb_pad = -(-n_blocks // LANES) * LANES
    n_e = counts.shape[0]
    return pl.pallas_call(
        functools.partial(_slots_kernel, block_rows=block_rows),
        grid=(t // tt,),
        in_specs=[
            pl.BlockSpec((n_e, LANES), lambda i: (0, 0)),
            pl.BlockSpec((TOP_K, tt), lambda i: (0, i)),
            pl.BlockSpec((TOP_K, tt), lambda i: (0, i)),
        ],
        out_specs=[
            pl.BlockSpec((TOP_K, tt), lambda i: (0, i)),
            pl.BlockSpec((1, nb_pad), lambda i: (0, 0)),
            pl.BlockSpec((1, LANES), lambda i: (0, 0)),
            pl.BlockSpec((n_e, LANES), lambda i: (0, 0)),
        ],
        out_shape=[
            jax.ShapeDtypeStruct((TOP_K, t), jnp.int32),
            jax.ShapeDtypeStruct((1, nb_pad), jnp.int32),
            jax.ShapeDtypeStruct((1, LANES), jnp.int32),
            jax.ShapeDtypeStruct((n_e, LANES), jnp.int32),
        ],
        compiler_params=_params("arbitrary"),
        name="moe_slots",
    )(counts, eidx, rank)


TILE_ROWS = 4
U32 = jnp.uint32


def _pack_bf16_pairs(v):
    half = v.shape[1] // 2
    lo = lax.bitcast_convert_type(v[:, :half].astype(BF16).astype(F32), U32)
    hi = lax.bitcast_convert_type(v[:, half:].astype(BF16).astype(F32), U32)
    return hi | (lo >> 16)


def _unpack_bf16_pairs(u):
    lo = lax.bitcast_convert_type(u << 16, F32)
    hi = lax.bitcast_convert_type(u & U32(0xFFFF0000), F32)
    return jnp.concatenate([lo, hi], axis=1)


def _tile_copy(src, src_tok, dst, dst_tok, sem):
    return pltpu.make_async_copy(
        src.at[pl.ds(pl.multiple_of(src_tok * TILE_ROWS, TILE_ROWS), TILE_ROWS)],
        dst.at[pl.ds(pl.multiple_of(dst_tok * TILE_ROWS, TILE_ROWS), TILE_ROWS)], sem)


def _to_token_tiles(ref, val, n_tok):
    for cidx in range(TILE_ROWS):
        ref[pl.ds(cidx, n_tok, stride=TILE_ROWS), :] = val[:, cidx * LANES:(cidx + 1) * LANES]


def _from_token_tiles(ref, n_tok):
    return jnp.concatenate(
        [ref[pl.ds(cidx, n_tok, stride=TILE_ROWS), :] for cidx in range(TILE_ROWS)], axis=1)


def _dispatch_kernel(dest_ref, seg_end_ref, seg_len_ref, nused_ref, x_ref, shift_ref, scale_ref,
                     xs_hbm, hs, zeros, sem, zsem, *, tm, t_total, block_rows, n_blocks):
    base = pl.program_id(0) * tm
    blk = block_rows * TILE_ROWS

    @pl.when(pl.program_id(0) == 0)
    def _():
        zeros[...] = jnp.zeros_like(zeros)

        def zero_block(first_tok):
            return pltpu.make_async_copy(
                zeros, xs_hbm.at[pl.ds(pl.multiple_of(first_tok * TILE_ROWS, blk), blk)], zsem)

        def seg_start(e, carry):
            @pl.when(seg_len_ref[e] > 0)
            def _():
                zero_block(seg_end_ref[e] - block_rows).start()
            return carry

        def seg_wait(e, carry):
            @pl.when(seg_len_ref[e] > 0)
            def _():
                zero_block(seg_end_ref[e] - block_rows).wait()
            return carry

        def tail_start(j, carry):
            zero_block(j * block_rows).start()
            return carry

        def tail_wait(j, carry):
            zero_block(j * block_rows).wait()
            return carry

        lax.fori_loop(0, N_EXPERTS, seg_start, 0)
        lax.fori_loop(nused_ref[0], n_blocks, tail_start, 0)
        lax.fori_loop(0, N_EXPERTS, seg_wait, 0)
        lax.fori_loop(nused_ref[0], n_blocks, tail_wait, 0)

    _to_token_tiles(hs, _pack_bf16_pairs(x_ref[...] * (1.0 + scale_ref[0]) + shift_ref[0]), tm)

    def issue(t, carry):
        for k in range(TOP_K):
            _tile_copy(hs, t, xs_hbm, dest_ref[k * t_total + base + t], sem).start(priority=k % 2)
        return carry

    lax.fori_loop(0, tm, issue, 0, unroll=2)
    for k in range(TOP_K):
        pltpu.make_async_copy(hs, xs_hbm.at[pl.ds(0, tm * TILE_ROWS)], sem).wait()


def moe_dispatch(dest_flat, seg_end, seg_len, nused, xt, shift, scale, n_blocks, block_rows,
                 s_len, tm=256):
    t, d = xt.shape
    per_b = s_len // tm
    grid_spec = pltpu.PrefetchScalarGridSpec(
        num_scalar_prefetch=4,
        grid=(t // tm,),
        in_specs=[
            pl.BlockSpec((tm, d), lambda i, *_: (i, 0)),
            pl.BlockSpec((1, 1, d), lambda i, *_: (i // per_b, 0, 0)),
            pl.BlockSpec((1, 1, d), lambda i, *_: (i // per_b, 0, 0)),
        ],
        out_specs=pl.BlockSpec(memory_space=pl.ANY),
        scratch_shapes=[pltpu.VMEM((tm * TILE_ROWS, LANES), U32),
                        pltpu.VMEM((block_rows * TILE_ROWS, LANES), U32),
                        pltpu.SemaphoreType.DMA, pltpu.SemaphoreType.DMA],
    )
    assert d == 2 * TILE_ROWS * LANES
    return pl.pallas_call(
        functools.partial(_dispatch_kernel, tm=tm, t_total=t, block_rows=block_rows,
                          n_blocks=n_blocks),
        grid_spec=grid_spec,
        out_shape=jax.ShapeDtypeStruct((n_blocks * block_rows * TILE_ROWS, LANES), U32),
        compiler_params=_params("arbitrary"),
        name="moe_dispatch",
    )(dest_flat, seg_end, seg_len, nused, xt, shift, scale)


def _expert_kernel(blk_e_ref, nused_ref, xs_ref, w1_ref, w3_ref, w2_ref, ys_ref,
                   w1b, w3b, w2b, *, block_rows):
    i = pl.program_id(0)
    prev = blk_e_ref[jnp.maximum(i - 1, 0)]

    @pl.when((i == 0) | (blk_e_ref[i] != prev))
    def _():
        w1b[...] = w1_ref[0].astype(BF16)
        w3b[...] = w3_ref[0].astype(BF16)
        w2b[...] = w2_ref[0].astype(BF16)

    @pl.when(i < nused_ref[0])
    def _():
        xb = _unpack_bf16_pairs(_from_token_tiles(xs_ref, block_rows)).astype(BF16)
        h1 = jnp.dot(xb, w1b[...], preferred_element_type=F32)
        h3 = jnp.dot(xb, w3b[...], preferred_element_type=F32)
        act = (h1 * _sigmoid(h1)) * h3
        y = jnp.dot(act.astype(BF16), w2b[...], preferred_element_type=F32)
        _to_token_tiles(ys_ref, _pack_bf16_pairs(y), block_rows)

    @pl.when(i >= nused_ref[0])
    def _():
        ys_ref[...] = jnp.zeros_like(ys_ref)


def moe_experts(blk_e, nused, xs, w1, w3, w2, block_rows, first_expert):
    d, de = w1.shape[-2:]
    n_rows = xs.shape[0] // TILE_ROWS
    n_blocks = n_rows // block_rows

    def row_map(i, be, nu):
        return (jnp.minimum(i, nu[0] - 1), 0)

    def w_map(i, be, nu):
        return (be[i] + first_expert, 0, 0)

    grid_spec = pltpu.PrefetchScalarGridSpec(
        num_scalar_prefetch=2,
        grid=(n_blocks,),
        in_specs=[
            pl.BlockSpec((block_rows * TILE_ROWS, LANES), row_map),
            pl.BlockSpec((1, d, de), w_map),
            pl.BlockSpec((1, d, de), w_map),
            pl.BlockSpec((1, de, d), w_map),
        ],
        out_specs=pl.BlockSpec((block_rows * TILE_ROWS, LANES), lambda i, be, nu: (i, 0)),
        scratch_shapes=[pltpu.VMEM((d, de), BF16), pltpu.VMEM((d, de), BF16),
                        pltpu.VMEM((de, d), BF16)],
    )
    return pl.pallas_call(
        functools.partial(_expert_kernel, block_rows=block_rows),
        grid_spec=grid_spec,
        out_shape=jax.ShapeDtypeStruct((n_rows * TILE_ROWS, LANES), U32),
        compiler_params=_params("arbitrary"),
        name="moe_experts",
    )(blk_e, nused, xs, w1, w3, w2)


def _combine_kernel(dest_ref, ys_hbm, x_ref, shift_ref, scale_ref, gate_ref, gcol_ref,
                    ws1_ref, ws3_ref, ws2_ref, g_ref, b_ref, o_ref, buf, sem,
                    *, tm, t_total, alpha):
    step = pl.program_id(0)
    slot = step % 2

    def start_gather(for_step, into):
        base = for_step * tm

        def issue(t, carry):
            for k in range(TOP_K):
                _tile_copy(ys_hbm, dest_ref[k * t_total + base + t], buf.at[into, k], t,
                           sem.at[into]).start(priority=k % 2)
            return carry

        lax.fori_loop(0, tm, issue, 0, unroll=2)

    @pl.when(step == 0)
    def _():
        start_gather(0, 0)

    @pl.when(step + 1 < pl.num_programs(0))
    def _():
        start_gather(step + 1, 1 - slot)

    x = x_ref[...]
    hb = (x * (1.0 + scale_ref[0]) + shift_ref[0]).astype(BF16)
    h1 = jnp.dot(hb, ws1_ref[...], preferred_element_type=F32)
    h3 = jnp.dot(hb, ws3_ref[...], preferred_element_type=F32)
    act = (h1 * _sigmoid(h1)) * h3
    y = jnp.dot(act.astype(BF16), ws2_ref[...], preferred_element_type=F32)

    for k in range(TOP_K):
        pltpu.make_async_copy(ys_hbm.at[pl.ds(0, tm * TILE_ROWS)], buf.at[slot, k],
                              sem.at[slot]).wait()

    gcol = gcol_ref[...]
    for k in range(TOP_K):
        y = y + _unpack_bf16_pairs(_from_token_tiles(buf.at[slot, k], tm)) * gcol[:, k:k + 1]
    z = alpha * x + gate_ref[0] * y
    o_ref[...] = _layer_norm(z, g_ref[...], b_ref[...])


def moe_combine(dest_flat, ys, xt, shift, scale, gate1p, gcol, ws1, ws3, ws2, ln_g, ln_b,
                s_len, alpha, tm=256):
    t, d = xt.shape
    ds_ = ws1.shape[1]
    per_b = s_len // tm

    def const(shape):
        return pl.BlockSpec(shape, lambda i, dest: (0,) * len(shape))

    mod_spec = pl.BlockSpec((1, 1, d), lambda i, dest: (i // per_b, 0, 0))
    grid_spec = pltpu.PrefetchScalarGridSpec(
        num_scalar_prefetch=1,
        grid=(t // tm,),
        in_specs=[
            pl.BlockSpec(memory_space=pl.ANY),
            pl.BlockSpec((tm, d), lambda i, dest: (i, 0)),
            mod_spec, mod_spec, mod_spec,
            pl.BlockSpec((tm, LANES), lambda i, dest: (i, 0)),
            const((d, ds_)), const((d, ds_)), const((ds_, d)),
            const((1, d)), const((1, d)),
        ],
        out_specs=pl.BlockSpec((tm, d), lambda i, dest: (i, 0)),
        scratch_shapes=[pltpu.VMEM((2, TOP_K, tm * TILE_ROWS, LANES), U32),
                        pltpu.SemaphoreType.DMA((2,))],
    )
    return pl.pallas_call(
        functools.partial(_combine_kernel, tm=tm, t_total=t, alpha=alpha),
        grid_spec=grid_spec,
        out_shape=jax.ShapeDtypeStruct((t, d), F32),
        compiler_params=_params("arbitrary"),
        name="moe_combine",
    )(dest_flat, ys, xt, shift, scale, gate1p, gcol, ws1, ws3, ws2,
      ln_g.reshape(1, d), ln_b.reshape(1, d))


def moe_layer(x, mods, router_w, router_bias, w1, w3, w2, ws1, ws3, ws2, ln_g, ln_b, alpha,
              layer):
    b, s, d = x.shape
    t = b * s
    shift = mods[:, None, 0:d]
    scale = mods[:, None, d:2 * d]
    gate1p = 1.0 + mods[:, None, 2 * d:3 * d]
    xt = x.reshape(t, d)
    n_blocks = t * TOP_K // MOE_BLOCK + N_EXPERTS
    eidx, rank, gcol, counts = moe_route(xt, shift, scale, router_w, router_bias, s)
    dest, blk_e, nused, seg = moe_slots(counts, eidx, rank, n_blocks, MOE_BLOCK)
    dest_flat = dest.reshape(-1)
    nused = nused.reshape(-1)
    xs = moe_dispatch(dest_flat, seg[:, 0], seg[:, 1], nused, xt, shift, scale, n_blocks,
                      MOE_BLOCK, s)
    n_e = w1.shape[1]
    ys = moe_experts(blk_e.reshape(-1), nused, xs,
                     w1.reshape((-1,) + w1.shape[2:]), w3.reshape((-1,) + w3.shape[2:]),
                     w2.reshape((-1,) + w2.shape[2:]), MOE_BLOCK, layer * n_e)
    out = moe_combine(dest_flat, ys, xt, shift, scale, gate1p, gcol,
                      ws1.astype(BF16), ws3.astype(BF16), ws2.astype(BF16), ln_g, ln_b, s, alpha)
    return out.reshape(b, s, d)


def kernel(x, c, positions, ada_w, ada_b, ln_g, ln_b, attn_w_in, attn_w_out, gdn_w_in, gdn_conv_w,
           gdn_a_log, gdn_dt_bias, gdn_norm_w, gdn_w_out, router_w, router_bias, expert_w1,
           expert_w3, expert_w2, shared_w1, shared_w3, shared_w2):
    depth = ada_w.shape[0]
    alpha = (2 * depth) ** 0.25
    mods = ada_vectors(c, ada_w, ada_b)
    cos_t, sin_t = rope_tables(positions)
    for i in range(depth):
        j = i // 2
        if i % 2 == 0:
            x = attention_layer(x, mods[2 * i], cos_t, sin_t, attn_w_in[j], attn_w_out[j],
                                ln_g[i, 0], ln_b[i, 0], alpha)
        else:
            x = gdn_layer(x, mods[2 * i], gdn_w_in[j], gdn_conv_w[j], gdn_a_log[j],
                          gdn_dt_bias[j], gdn_norm_w[j], gdn_w_out[j], ln_g[i, 0], ln_b[i, 0],
                          alpha)
        x = moe_layer(x, mods[2 * i + 1], router_w[i], router_bias[i], expert_w1,
                      expert_w3, expert_w2, shared_w1[i], shared_w3[i], shared_w2[i],
                      ln_g[i, 1], ln_b[i, 1], alpha, i)
    return x
```
